```python
import jax, jax.numpy as jnp
from jax import lax
import numpy as np

D_MODEL = 2048
BATCH = 2
SEQ = 4096
DEPTH = 1
DEC_BATCH = 2
DEC_SEQ = 16384
PAST_LEN = 128

MIX_WIDTH = D_MODEL
RET_WIDTH = MIX_WIDTH // 2
CONV_WIDTH = MIX_WIDTH - RET_WIDTH
RET_HEADS = 8
RET_HEAD_DIM = RET_WIDTH // RET_HEADS
CONV_GROUPS = 16
CONV_K = 3
CHUNK = 128
N_EXPERTS = 16
EC_CAPACITY_FACTOR = 2
EXPERT_FF = D_MODEL
ROPE_BASE = 10000.0
EPS = 1e-6
IN_SPLITS = [RET_WIDTH, 2 * RET_WIDTH, 3 * RET_WIDTH, 4 * RET_WIDTH,
             4 * RET_WIDTH + CONV_WIDTH, 4 * RET_WIDTH + 2 * CONV_WIDTH]
IN_COLS = 4 * RET_WIDTH + 3 * CONV_WIDTH

kernel_name = "hybrid_retention_shortconv_ec_moe_encoder"


def rmsnorm(x, g):
    xf = x.astype(jnp.float32)
    y = xf * lax.rsqrt(jnp.mean(xf * xf, axis=-1, keepdims=True) + EPS)
    return (y * g.astype(jnp.float32)).astype(x.dtype)


def rotary(t, pos):
    half = t.shape[-1] // 2
    inv = ROPE_BASE ** (-jnp.arange(half, dtype=jnp.float32) / half)
    ang = pos[:, None] * inv[None, :]
    cos = jnp.cos(ang)[None, :, None, :]
    sin = jnp.sin(ang)[None, :, None, :]
    t1, t2 = t[..., :half], t[..., half:]
    return jnp.concatenate([t1 * cos - t2 * sin, t1 * sin + t2 * cos], axis=-1)


def retention_one_direction(q, k, v, log_gamma, strict):
    bsz, h, s, d = q.shape
    dv = v.shape[-1]
    n_chunks = s // CHUNK

    def to_chunks(t):
        return t.reshape(bsz, h, n_chunks, CHUNK, t.shape[-1]).transpose(2, 0, 1, 3, 4)

    idx = jnp.arange(CHUNK, dtype=jnp.float32)
    diff = idx[:, None] - idx[None, :]
    mask = (diff > 0) if strict else (diff >= 0)
    inner_decay = jnp.where(mask[None], jnp.exp(log_gamma[:, None, None] * jnp.where(mask, diff, 0.0)[None]), 0.0)
    xi = jnp.exp(log_gamma[:, None] * (idx[None, :] + 1.0))
    zeta = jnp.exp(log_gamma[:, None] * (CHUNK - 1.0 - idx[None, :]))
    gamma_c = jnp.exp(log_gamma * CHUNK)

    def step(state, qkv):
        qc, kc, vc = qkv
        scores = jnp.einsum('bhid,bhjd->bhij', qc, kc) * inner_decay[None]
        o = (jnp.einsum('bhij,bhje->bhie', scores, vc)
             + jnp.einsum('bhid,bhde->bhie', qc, state) * xi[None, :, :, None])
        state = (state * gamma_c[None, :, None, None]
                 + jnp.einsum('bhjd,bhje->bhde', kc * zeta[None, :, :, None], vc))
        return state, o

    state0 = jnp.zeros((bsz, h, d, dv), jnp.float32)
    _, o = lax.scan(step, state0, (to_chunks(q), to_chunks(k), to_chunks(v)))
    return o.transpose(1, 2, 0, 3, 4).reshape(bsz, h, s, dv)


def bidirectional_retention(q, k, v, decay_fwd, decay_bwd):
    lg_f = -jnp.exp(decay_fwd.astype(jnp.float32))
    lg_b = -jnp.exp(decay_bwd.astype(jnp.float32))
    fwd = retention_one_direction(q, k, v, lg_f, strict=False)
    flip = lambda t: jnp.flip(t, axis=2)
    bwd = flip(retention_one_direction(flip(q), flip(k), flip(v), lg_b, strict=True))
    return fwd + bwd


def token_mixer(h, w_in, conv_w, decay_fwd, decay_bwd, gn_g, w_o):
    bsz, s, _ = h.shape
    proj = h @ w_in
    q, k, v, g, cb, cc, ch = jnp.split(proj, IN_SPLITS, axis=-1)

    pos = jnp.arange(s, dtype=jnp.float32)
    heads = lambda t: t.reshape(bsz, s, RET_HEADS, RET_HEAD_DIM).astype(jnp.float32)
    qh = rotary(heads(q), pos).transpose(0, 2, 1, 3)
    kh = (rotary(heads(k), pos) * (RET_HEAD_DIM ** -0.5)).transpose(0, 2, 1, 3)
    vh = heads(v).transpose(0, 2, 1, 3)
    o = bidirectional_retention(qh, kh, vh, decay_fwd, decay_bwd).transpose(0, 2, 1, 3)
    mu = jnp.mean(o, axis=-1, keepdims=True)
    var = jnp.mean(jnp.square(o - mu), axis=-1, keepdims=True)
    o = ((o - mu) * lax.rsqrt(var + EPS)).reshape(bsz, s, RET_WIDTH) * gn_g.astype(jnp.float32)
    ret_out = (jax.nn.silu(g.astype(jnp.float32)) * o).astype(h.dtype)

    z = cc * ch
    zp = jnp.pad(z, ((0, 0), (1, 1), (0, 0)))
    zc = zp[:, :-2] * conv_w[0] + zp[:, 1:-1] * conv_w[1] + zp[:, 2:] * conv_w[2]
    conv_out = cb * zc

    return jnp.concatenate([ret_out, conv_out], axis=-1) @ w_o


def expert_choice_ffn(h, w_router, w_gate, w_up, w_down):
    bsz, s, d = h.shape
    t = bsz * s
    cap = EC_CAPACITY_FACTOR * t // N_EXPERTS
    tok = h.reshape(t, d)
    affinity = jax.nn.softmax((tok @ w_router).astype(jnp.float32), axis=-1)
    gates, idx = lax.top_k(affinity.T, cap)
    xe = tok[idx]
    hid = jax.nn.silu(jnp.einsum('ecd,edf->ecf', xe, w_gate)) * jnp.einsum('ecd,edf->ecf', xe, w_up)
    ye = jnp.einsum('ecf,efd->ecd', hid, w_down) * gates[..., None].astype(h.dtype)
    out = jnp.zeros((t, d), h.dtype).at[idx.reshape(-1)].add(ye.reshape(-1, d))
    return out.reshape(bsz, s, d)


def trunk(x, norm_mix_g, w_in, conv_w, ret_decay_fwd, ret_decay_bwd, ret_gn_g, w_o,
          norm_ffn_g, w_router, w_gate, w_up, w_down, final_norm_g):
    for l in range(DEPTH):
        x = x + token_mixer(rmsnorm(x, norm_mix_g[l]), w_in[l], conv_w[l], ret_decay_fwd[l],
                            ret_decay_bwd[l], ret_gn_g[l], w_o[l])
        x = x + expert_choice_ffn(rmsnorm(x, norm_ffn_g[l]), w_router[l], w_gate[l], w_up[l], w_down[l])
    return rmsnorm(x, final_norm_g)


def setup_inputs(seed: int = 0) -> dict:
    key = jax.random.key(seed)
    ks = jax.random.split(key, 16)
    f32 = jnp.float32
    nrm = lambda k, shape, scale: jax.random.normal(k, shape, f32) * scale
    base_decay = np.log(-np.log(1.0 - 2.0 ** (-5.0 - np.arange(RET_HEADS, dtype=np.float32)))).astype(np.float32)
    base_decay = jnp.asarray(base_decay)[None, :]
    return {
        "x_prompt": jax.random.normal(ks[0], (BATCH, SEQ, D_MODEL), f32),
        "x_sample": jax.random.normal(ks[1], (DEC_BATCH, DEC_SEQ, D_MODEL), f32),
        "norm_mix_g": 1.0 + nrm(ks[2], (DEPTH, D_MODEL), 0.02),
        "w_in": nrm(ks[3], (DEPTH, D_MODEL, IN_COLS), D_MODEL ** -0.5),
        "conv_w": nrm(ks[4], (DEPTH, CONV_K, CONV_WIDTH), CONV_K ** -0.5),
        "ret_decay_fwd": base_decay + nrm(ks[5], (DEPTH, RET_HEADS), 0.1),
        "ret_decay_bwd": base_decay + nrm(ks[6], (DEPTH, RET_HEADS), 0.1),
        "ret_gn_g": 1.0 + nrm(ks[7], (DEPTH, RET_WIDTH), 0.02),
        "w_o": nrm(ks[8], (DEPTH, MIX_WIDTH, D_MODEL), MIX_WIDTH ** -0.5),
        "norm_ffn_g": 1.0 + nrm(ks[9], (DEPTH, D_MODEL), 0.02),
        "w_router": nrm(ks[10], (DEPTH, D_MODEL, N_EXPERTS), D_MODEL ** -0.5),
        "w_gate": nrm(ks[11], (DEPTH, N_EXPERTS, D_MODEL, EXPERT_FF), D_MODEL ** -0.5),
        "w_up": nrm(ks[12], (DEPTH, N_EXPERTS, D_MODEL, EXPERT_FF), D_MODEL ** -0.5),
        "w_down": nrm(ks[13], (DEPTH, N_EXPERTS, EXPERT_FF, D_MODEL), EXPERT_FF ** -0.5),
        "final_norm_g": 1.0 + nrm(ks[14], (D_MODEL,), 0.02),
    }


def reference(x_prompt, x_sample, norm_mix_g, w_in, conv_w, ret_decay_fwd, ret_decay_bwd, ret_gn_g, w_o,
              norm_ffn_g, w_router, w_gate, w_up, w_down, final_norm_g):
    y_prompt = trunk(x_prompt, norm_mix_g, w_in, conv_w, ret_decay_fwd, ret_decay_bwd, ret_gn_g, w_o,
                     norm_ffn_g, w_router, w_gate, w_up, w_down, final_norm_g)
    y_sample = trunk(x_sample, norm_mix_g, w_in, conv_w, ret_decay_fwd, ret_decay_bwd, ret_gn_g, w_o,
                     norm_ffn_g, w_router, w_gate, w_up, w_down, final_norm_g)
    return (y_prompt, y_sample)
```

```python
import functools

import jax
import jax.numpy as jnp
from jax import lax
from jax.experimental import pallas as pl
from jax.experimental.pallas import tpu as pltpu

F32 = jnp.float32
BF16 = jnp.bfloat16
I32 = jnp.int32

D_MODEL = 2048
RET_WIDTH = 1024
RET_HEADS = 8
HEAD_DIM = 128
IN_COLS = 7168
COL_BLOCK = 1024
N_EXPERTS = 16
EC_CAPACITY_FACTOR = 2
EXPERT_FF = 2048
CHUNK = 128
ROPE_BASE = 10000.0
EPS = 1e-6
LANES = 128
SUBLANES = 8
IDX_PAD = 1024
VMEM_LIMIT = 56 * 1024 * 1024


def _params(sem, vmem=VMEM_LIMIT):
    return pltpu.CompilerParams(dimension_semantics=sem, vmem_limit_bytes=vmem)


def _rmsnorm(x, g):
    return x * lax.rsqrt(jnp.mean(x * x, axis=-1, keepdims=True) + EPS) * g


def _split_bf16(x):
    hi = x.astype(BF16)
    lo = (x - hi.astype(F32)).astype(BF16)
    return hi, lo


def _dot3(a, b, dims):
    a_hi, a_lo = _split_bf16(a)
    b_hi, b_lo = _split_bf16(b)
    d = functools.partial(lax.dot_general, dimension_numbers=dims, preferred_element_type=F32)
    return d(a_hi, b_hi) + d(a_hi, b_lo) + d(a_lo, b_hi)


def _inproj_kernel(x_ref, g_ref, w_ref, cos_ref, sin_ref, o_ref, h_ref):
    j = pl.program_id(1)

    @pl.when(j == 0)
    def _():
        h_ref[...] = _rmsnorm(x_ref[...], g_ref[...]).astype(BF16)

    o_ref[...] = jnp.dot(h_ref[...], w_ref[...], preferred_element_type=F32)

    @pl.when(j < 2)
    def _():
        scale = jnp.where(j == 1, HEAD_DIM ** -0.5, 1.0).astype(F32)
        cos = cos_ref[...]
        sin = sin_ref[...]
        for h in range(COL_BLOCK // HEAD_DIM):
            sl = slice(h * HEAD_DIM, (h + 1) * HEAD_DIM)
            t = o_ref[:, sl]
            o_ref[:, sl] = (t * cos + pltpu.roll(t, HEAD_DIM // 2, axis=1) * sin) * scale


def _in_proj(x2d, g, w_bf16, cos_t, sin_t, seq):
    t = x2d.shape[0]
    tm = min(1024, seq)
    blocks_per_seq = seq // tm
    return pl.pallas_call(
        _inproj_kernel,
        out_shape=jax.ShapeDtypeStruct((t, IN_COLS), F32),
        grid=(t // tm, IN_COLS // COL_BLOCK),
        in_specs=[
            pl.BlockSpec((tm, D_MODEL), lambda i, j: (i, 0)),
            pl.BlockSpec((1, D_MODEL), lambda i, j: (0, 0)),
            pl.BlockSpec((D_MODEL, COL_BLOCK), lambda i, j: (0, j)),
            pl.BlockSpec((tm, HEAD_DIM), lambda i, j: (i % blocks_per_seq, 0)),
            pl.BlockSpec((tm, HEAD_DIM), lambda i, j: (i % blocks_per_seq, 0)),
        ],
        out_specs=pl.BlockSpec((tm, COL_BLOCK), lambda i, j: (i, j)),
        scratch_shapes=[pltpu.VMEM((tm, D_MODEL), BF16)],
        compiler_params=_params(("parallel", "arbitrary")),
        name="in_proj",
    )(x2d, g, w_bf16, cos_t, sin_t)


def _retention_kernel(dec_ref, qf_ref, kf_ref, vf_ref, qb_ref, kb_ref, vb_ref,
                      of_ref, ob_ref,
                      sf_ref, sb_ref, dtab_ref, xif_ref, xib_ref, zf_ref, zb_ref):
    first = jnp.logical_and(pl.program_id(0) == 0, pl.program_id(1) == 0)

    @pl.when(first)
    def _():
        row = lax.broadcasted_iota(I32, (CHUNK, CHUNK), 0).astype(F32)
        col = lax.broadcasted_iota(I32, (CHUNK, CHUNK), 1).astype(F32)
        diff = row - col
        for h in range(RET_HEADS):
            lgf = -jnp.exp(dec_ref[0:1, h:h + 1])
            lgb = -jnp.exp(dec_ref[1:2, h:h + 1])
            dtab_ref[h] = jnp.where(diff >= 0, jnp.exp(lgf * jnp.maximum(diff, 0.0)),
                                    jnp.exp(lgb * jnp.maximum(-diff, 0.0)))
            xif_ref[h] = jnp.exp(lgf * (row + 1.0))
            zf_ref[h] = jnp.exp(lgf * (CHUNK - 1.0 - row))
            xib_ref[h] = jnp.exp(lgb * (CHUNK - row))
            zb_ref[h] = jnp.exp(lgb * row)

    @pl.when(pl.program_id(1) == 0)
    def _():
        sf_ref[...] = jnp.zeros_like(sf_ref)
        sb_ref[...] = jnp.zeros_like(sb_ref)

    nt = (((1,), (1,)), ((), ()))
    tn = (((0,), (0,)), ((), ()))
    for h in range(RET_HEADS):
        sl = slice(h * HEAD_DIM, (h + 1) * HEAD_DIM)
        gcf = jnp.exp(-jnp.exp(dec_ref[0:1, h:h + 1]) * CHUNK)
        gcb = jnp.exp(-jnp.exp(dec_ref[1:2, h:h + 1]) * CHUNK)

        q = qf_ref[:, sl].astype(BF16)
        k32 = kf_ref[:, sl]
        k = k32.astype(BF16)
        v = vf_ref[:, sl].astype(BF16)
        s = lax.dot_general(q, k, nt, preferred_element_type=F32) * dtab_ref[h]
        sf = sf_ref[h]
        o = jnp.dot(s.astype(BF16), v, preferred_element_type=F32)
        o = o + jnp.dot(q, sf.astype(BF16), preferred_element_type=F32) * xif_ref[h]
        of_ref[:, sl] = o
        kz = (k32 * zf_ref[h]).astype(BF16)
        sf_ref[h] = sf * gcf + lax.dot_general(kz, v, tn, preferred_element_type=F32)

        q = qb_ref[:, sl].astype(BF16)
        k32 = kb_ref[:, sl]
        v = vb_ref[:, sl].astype(BF16)
        sb = sb_ref[h]
        ob_ref[:, sl] = jnp.dot(q, sb.astype(BF16), preferred_element_type=F32) * xib_ref[h]
        kz = (k32 * zb_ref[h]).astype(BF16)
        sb_ref[h] = sb * gcb + lax.dot_general(kz, v, tn, preferred_element_type=F32)


def _retention(proj, decays, batch, seq):
    t = proj.shape[0]
    n = seq // CHUNK
    fwd = lambda col: pl.BlockSpec((CHUNK, COL_BLOCK), lambda b, c: (b * n + c, col))
    bwd = lambda col: pl.BlockSpec((CHUNK, COL_BLOCK), lambda b, c: (b * n + n - 1 - c, col))
    table = pltpu.VMEM((RET_HEADS, CHUNK, CHUNK), F32)
    return pl.pallas_call(
        _retention_kernel,
        out_shape=(jax.ShapeDtypeStruct((t, RET_WIDTH), F32), jax.ShapeDtypeStruct((t, RET_WIDTH), F32)),
        grid=(batch, n),
        in_specs=[pl.BlockSpec((2, RET_HEADS), lambda b, c: (0, 0)),
                  fwd(0), fwd(1), fwd(2), bwd(0), bwd(1), bwd(2)],
        out_specs=(pl.BlockSpec((CHUNK, RET_WIDTH), lambda b, c: (b * n + c, 0)),
                   pl.BlockSpec((CHUNK, RET_WIDTH), lambda b, c: (b * n + n - 1 - c, 0))),
        scratch_shapes=[table] * 7,
        compiler_params=_params(("arbitrary", "arbitrary")),
        name="retention",
    )(decays, proj, proj, proj, proj, proj, proj)


def _mixout_kernel(of_ref, ob_ref, g_ref, cb_ref, cc_ref, ch_ref, ccp_ref, chp_ref, ccn_ref, chn_ref,
                   x_ref, gn_ref, cw_ref, wo_ref, g2_ref, wr_ref,
                   x1_ref, acc_ref, aff_ref, *, blocks_per_seq):
    i = pl.program_id(0)
    tm = x_ref.shape[0]

    o = of_ref[...] + ob_ref[...]
    gate_in = g_ref[...]
    gn = gn_ref[...]
    parts = []
    for h in range(RET_HEADS):
        sl = slice(h * HEAD_DIM, (h + 1) * HEAD_DIM)
        oh = o[:, sl]
        mu = jnp.mean(oh, axis=-1, keepdims=True)
        d = oh - mu
        var = jnp.mean(d * d, axis=-1, keepdims=True)
        gh = gate_in[:, sl]
        swish = gh / (1.0 + jnp.exp(-gh))
        parts.append((swish * (d * lax.rsqrt(var + EPS) * gn[:, sl])).astype(BF16))
    ret = jnp.concatenate(parts, axis=-1)

    z = cc_ref[...] * ch_ref[...]
    pos = i % blocks_per_seq
    keep_prev = jnp.where(pos == 0, 0.0, 1.0).astype(F32)
    keep_next = jnp.where(pos == blocks_per_seq - 1, 0.0, 1.0).astype(F32)
    halo_prev = ccp_ref[SUBLANES - 1:SUBLANES, :] * chp_ref[SUBLANES - 1:SUBLANES, :] * keep_prev
    halo_next = ccn_ref[0:1, :] * chn_ref[0:1, :] * keep_next
    row = lax.broadcasted_iota(I32, z.shape, 0)
    z_prev = jnp.where(row == 0, halo_prev, pltpu.roll(z, 1, axis=0))
    z_next = jnp.where(row == tm - 1, halo_next, pltpu.roll(z, tm - 1, axis=0))
    cw = cw_ref[...]
    conv = cb_ref[...] * (z_prev * cw[0:1, :] + z * cw[1:2, :] + z_next * cw[2:3, :])

    y = jnp.dot(ret, wo_ref[0:RET_WIDTH, :], preferred_element_type=F32)
    y = y + jnp.dot(conv.astype(BF16), wo_ref[RET_WIDTH:, :], preferred_element_type=F32)
    x1 = x_ref[...] + y
    x1_ref[...] = x1
    acc_ref[...] = x1

    h2 = _rmsnorm(x1, g2_ref[...])
    logits = _dot3(wr_ref[...], h2, (((1,), (1,)), ((), ())))
    ex = jnp.exp(logits - jnp.max(logits, axis=0, keepdims=True))
    aff_ref[...] = ex / jnp.sum(ex, axis=0, keepdims=True)


def _mix_out(o_f, o_b, proj, x2d, gn_g, conv_w, wo_bf16, g2, wr_t, seq):
    t = x2d.shape[0]
    tm = min(256, seq)
    bps = seq // tm
    r8 = tm // SUBLANES
    n8 = t // SUBLANES
    col = lambda c: pl.BlockSpec((tm, COL_BLOCK), lambda i: (i, c))
    prev = lambda c: pl.BlockSpec((SUBLANES, COL_BLOCK), lambda i: (jnp.maximum(i * r8 - 1, 0), c))
    nxt = lambda c: pl.BlockSpec((SUBLANES, COL_BLOCK), lambda i: (jnp.minimum((i + 1) * r8, n8 - 1), c))
    whole = lambda shape: pl.BlockSpec(shape, lambda i: (0,) * len(shape))
    return pl.pallas_call(
        functools.partial(_mixout_kernel, blocks_per_seq=bps),
        out_shape=(jax.ShapeDtypeStruct((t, D_MODEL), F32), jax.ShapeDtypeStruct((t, D_MODEL), F32),
                   jax.ShapeDtypeStruct((N_EXPERTS, t), F32)),
        grid=(t // tm,),
        in_specs=[
            pl.BlockSpec((tm, RET_WIDTH), lambda i: (i, 0)),
            pl.BlockSpec((tm, RET_WIDTH), lambda i: (i, 0)),
            col(3), col(4), col(5), col(6), prev(5), prev(6), nxt(5), nxt(6),
            pl.BlockSpec((tm, D_MODEL), lambda i: (i, 0)),
            whole((1, RET_WIDTH)), whole((3, COL_BLOCK)), whole((D_MODEL, D_MODEL)),
            whole((1, D_MODEL)), whole((N_EXPERTS, D_MODEL)),
        ],
        out_specs=(pl.BlockSpec((tm, D_MODEL), lambda i: (i, 0)),
                   pl.BlockSpec((tm, D_MODEL), lambda i: (i, 0)),
                   pl.BlockSpec((N_EXPERTS, tm), lambda i: (0, i))),
        compiler_params=_params(("parallel",)),
        name="mix_out",
    )(o_f, o_b, proj, proj, proj, proj, proj, proj, proj, proj, x2d, gn_g, conv_w, wo_bf16, g2, wr_t)


def _select_kernel(aff_ref, sel_ref, *, cap):
    e_n, rows, _ = aff_ref.shape
    bits = lax.bitcast_convert_type(aff_ref[...], I32)

    def count(mask):
        c = jnp.sum(jnp.where(mask, 1.0, 0.0), axis=1, keepdims=True)
        return jnp.sum(c, axis=2, keepdims=True)

    thr = jnp.zeros((e_n, 1, 1), I32)
    for b in range(30, -1, -1):
        cand = thr | (1 << b)
        thr = jnp.where(count(bits >= cand) >= cap, cand, thr)

    gt = bits > thr
    eq = bits == thr
    need = cap - count(gt)

    li = lax.broadcasted_iota(I32, (LANES, LANES), 0)
    lj = lax.broadcasted_iota(I32, (LANES, LANES), 1)
    upper = (li <= lj).astype(BF16)
    ri = lax.broadcasted_iota(I32, (rows, rows), 0)
    rj = lax.broadcasted_iota(I32, (rows, rows), 1)
    lower = (rj < ri).astype(BF16)
    eqf = jnp.where(eq, 1.0, 0.0)
    for e in range(e_n):
        incl = jnp.dot(eqf[e].astype(BF16), upper, preferred_element_type=F32)
        row_tot = jnp.broadcast_to(incl[:, LANES - 1:LANES], (rows, LANES)).astype(BF16)
        row_off = jnp.dot(lower, row_tot, preferred_element_type=F32)
        rank = row_off + incl - eqf[e]
        take = jnp.logical_or(gt[e], jnp.logical_and(eq[e], rank < need[e]))
        sel_ref[e] = jnp.where(take, 1, 0).astype(I32)


def _select(aff3, cap):
    e_n, rows, lanes = aff3.shape
    return pl.pallas_call(
        functools.partial(_select_kernel, cap=cap),
        out_shape=jax.ShapeDtypeStruct((e_n, rows, lanes), I32),
        grid=(1,),
        in_specs=[pl.BlockSpec((e_n, rows, lanes), lambda i: (0, 0, 0))],
        out_specs=pl.BlockSpec((e_n, rows, lanes), lambda i: (0, 0, 0)),
        compiler_params=_params(("arbitrary",)),
        name="select",
    )(aff3)


def _compact_kernel(sel_ref, idx_ref, cnt_ref, *, chunk, cap):
    c = pl.program_id(1)

    @pl.when(c == 0)
    def _():
        cnt_ref[0] = 0

        def clear(i, carry):
            idx_ref[cap + i] = 0
            return carry
        lax.fori_loop(0, IDX_PAD, clear, 0, unroll=8)

    base = c * chunk

    def body(i, n):
        idx_ref[n] = base + i
        return n + sel_ref[i]

    cnt_ref[0] = lax.fori_loop(0, chunk, body, cnt_ref[0], unroll=8)


def _compact(sel_flat, t, cap):
    chunk = min(4096, t)
    nchunks = t // chunk
    stride = cap + IDX_PAD
    return pl.pallas_call(
        functools.partial(_compact_kernel, chunk=chunk, cap=cap),
        out_shape=jax.ShapeDtypeStruct((N_EXPERTS * stride,), I32),
        grid=(N_EXPERTS, nchunks),
        in_specs=[pl.BlockSpec((chunk,), lambda e, c: (e * nchunks + c,), memory_space=pltpu.SMEM)],
        out_specs=pl.BlockSpec((stride,), lambda e, c: (e,), memory_space=pltpu.SMEM),
        scratch_shapes=[pltpu.SMEM((1,), I32)],
        compiler_params=_params(("arbitrary", "arbitrary")),
        name="compact",
    )(sel_flat)


def _experts_kernel(idx_ref, x1_hbm, acc_in_hbm, wg_ref, wu_ref, wd_ref, g2_ref, wr_ref,
                    acc_hbm, xbuf, xn_ref, yacc_ref, gate_ref, sem):
    del acc_in_hbm
    e = pl.program_id(0)
    f = pl.program_id(2)
    nf = pl.num_programs(2)
    tc = xbuf.shape[0]

    def row_copy(src_hbm, r):
        return pltpu.make_async_copy(src_hbm.at[pl.ds(idx_ref[r], 1), :], xbuf.at[pl.ds(r, 1), :], sem.at[0])

    def gather_rows(src_hbm):
        def issue(r, carry):
            row_copy(src_hbm, r).start()
            return carry
        lax.fori_loop(0, tc, issue, 0)

        def drain(r, carry):
            row_copy(src_hbm, r).wait()
            return carry
        lax.fori_loop(0, tc, drain, 0)

    @pl.when(f == 0)
    def _():
        gather_rows(x1_hbm)
        h2 = _rmsnorm(xbuf[...], g2_ref[...])
        xn_ref[...] = h2.astype(BF16)
        logits = _dot3(h2, wr_ref[...], (((1,), (0,)), ((), ())))
        ex = jnp.exp(logits - jnp.max(logits, axis=-1, keepdims=True))
        aff = ex / jnp.sum(ex, axis=-1, keepdims=True)
        lane = lax.broadcasted_iota(I32, aff.shape, 1)
        gate_ref[...] = jnp.sum(jnp.where(lane == e, aff, 0.0), axis=-1, keepdims=True)
        yacc_ref[...] = jnp.zeros_like(yacc_ref)

    xn = xn_ref[...]
    hg = jnp.dot(xn, wg_ref[0], preferred_element_type=F32)
    hu = jnp.dot(xn, wu_ref[0], preferred_element_type=F32)
    hid = (hg / (1.0 + jnp.exp(-hg)) * hu).astype(BF16)
    yacc_ref[...] += jnp.dot(hid, wd_ref[0], preferred_element_type=F32)

    @pl.when(f == nf - 1)
    def _():
        gather_rows(acc_hbm)
        xbuf[...] = xbuf[...] + yacc_ref[...] * gate_ref[...]

        def put(r):
            return pltpu.make_async_copy(xbuf.at[pl.ds(r, 1), :], acc_hbm.at[pl.ds(idx_ref[r], 1), :], sem.at[1])

        def issue(r, carry):
            put(r).start()
            return carry
        lax.fori_loop(0, tc, issue, 0)

        def drain(r, carry):
            put(r).wait()
            return carry
        lax.fori_loop(0, tc, drain, 0)


def _experts(idx_flat, x1, acc, wg, wu, wd, g2, wr, cap):
    t = x1.shape[0]
    tc = min(1024, cap)
    tf = 512
    stride_blocks = (cap + IDX_PAD) // tc
    return pl.pallas_call(
        _experts_kernel,
        out_shape=jax.ShapeDtypeStruct((t, D_MODEL), F32),
        grid=(N_EXPERTS, cap // tc, EXPERT_FF // tf),
        in_specs=[
            pl.BlockSpec((tc,), lambda e, c, f: (e * stride_blocks + c,), memory_space=pltpu.SMEM),
            pl.BlockSpec(memory_space=pl.ANY),
            pl.BlockSpec(memory_space=pl.ANY),
            pl.BlockSpec((1, D_MODEL, tf), lambda e, c, f: (e, 0, f)),
            pl.BlockSpec((1, D_MODEL, tf), lambda e, c, f: (e, 0, f)),
            pl.BlockSpec((1, tf, D_MODEL), lambda e, c, f: (e, f, 0)),
            pl.BlockSpec((1, D_MODEL), lambda e, c, f: (0, 0)),
            pl.BlockSpec((D_MODEL, N_EXPERTS), lambda e, c, f: (0, 0)),
        ],
        out_specs=pl.BlockSpec(memory_space=pl.ANY),
        scratch_shapes=[
            pltpu.VMEM((tc, D_MODEL), F32),
            pltpu.VMEM((tc, D_MODEL), BF16),
            pltpu.VMEM((tc, D_MODEL), F32),
            pltpu.VMEM((tc, 1), F32),
            pltpu.SemaphoreType.DMA((2,)),
        ],
        input_output_aliases={2: 0},
        compiler_params=_params(("arbitrary", "arbitrary", "arbitrary")),
        name="experts",
    )(idx_flat, x1, acc, wg, wu, wd, g2, wr)


def _final_kernel(x_ref, g_ref, o_ref):
    o_ref[...] = _rmsnorm(x_ref[...], g_ref[...])


def _final_norm(x2d, g):
    t = x2d.shape[0]
    tm = min(512, t)
    return pl.pallas_call(
        _final_kernel,
        out_shape=jax.ShapeDtypeStruct((t, D_MODEL), F32),
        grid=(t // tm,),
        in_specs=[pl.BlockSpec((tm, D_MODEL), lambda i: (i, 0)), pl.BlockSpec((1, D_MODEL), lambda i: (0, 0))],
        out_specs=pl.BlockSpec((tm, D_MODEL), lambda i: (i, 0)),
        compiler_params=_params(("parallel",)),
        name="final_norm",
    )(x2d, g)


def _rotary_tables(seq):
    half = HEAD_DIM // 2
    inv = ROPE_BASE ** (-jnp.arange(half, dtype=F32) / half)
    ang = jnp.arange(seq, dtype=F32)[:, None] * inv[None, :]
    cos, sin = jnp.cos(ang), jnp.sin(ang)
    return jnp.concatenate([cos, cos], axis=-1), jnp.concatenate([-sin, sin], axis=-1)


def _trunk(x, w):
    batch, seq, _ = x.shape
    t = batch * seq
    cap = EC_CAPACITY_FACTOR * t // N_EXPERTS
    x2d = x.reshape(t, D_MODEL)
    cos_t, sin_t = _rotary_tables(seq)

    proj = _in_proj(x2d, w["norm_mix_g"], w["w_in"], cos_t, sin_t, seq)
    o_f, o_b = _retention(proj, w["decays"], batch, seq)
    x1, acc, aff = _mix_out(o_f, o_b, proj, x2d, w["ret_gn_g"], w["conv_w"], w["w_o"],
                            w["norm_ffn_g"], w["w_router_t"], seq)
    sel = _select(aff.reshape(N_EXPERTS, t // LANES, LANES), cap)
    idx = _compact(sel.reshape(N_EXPERTS * t), t, cap)
    acc = _experts(idx, x1, acc, w["w_gate"], w["w_up"], w["w_down"], w["norm_ffn_g"], w["w_router"], cap)
    return _final_norm(acc, w["final_norm_g"]).reshape(batch, seq, D_MODEL)


def kernel(x_prompt, x_sample, norm_mix_g, w_in, conv_w, ret_decay_fwd, ret_decay_bwd, ret_gn_g, w_o,
           norm_ffn_g, w_router, w_gate, w_up, w_down, final_norm_g):
    w = {
        "norm_mix_g": norm_mix_g[0][None, :],
        "w_in": w_in[0].astype(BF16),
        "conv_w": conv_w[0],
        "decays": jnp.stack([ret_decay_fwd[0], ret_decay_bwd[0]]),
        "ret_gn_g": ret_gn_g[0][None, :],
        "w_o": w_o[0].astype(BF16),
        "norm_ffn_g": norm_ffn_g[0][None, :],
        "w_router": w_router[0],
        "w_router_t": w_router[0].T,
        "w_gate": w_gate[0].astype(BF16),
        "w_up": w_up[0].astype(BF16),
        "w_down": w_down[0].astype(BF16),
        "final_norm_g": final_norm_g[None, :],
    }
    return (_trunk(x_prompt, w), _trunk(x_sample, w))
```

```python
import functools

import jax
import jax.numpy as jnp
from jax import lax
from jax.experimental import pallas as pl
from jax.experimental.pallas import tpu as pltpu

F32 = jnp.float32
BF16 = jnp.bfloat16
I32 = jnp.int32

D_MODEL = 2048
RET_WIDTH = 1024
RET_HEADS = 8
HEAD_DIM = 128
IN_COLS = 7168
COL_BLOCK = 1024
N_EXPERTS = 16
EC_CAPACITY_FACTOR = 2
EXPERT_FF = 2048
CHUNK = 128
EXPERT_TILE = 1024
EXPERT_ROWS = 256
ROPE_BASE = 10000.0
EPS = 1e-6
LANES = 128
SUBLANES = 8
VMEM_LIMIT = 56 * 1024 * 1024


def _params(sem, vmem=VMEM_LIMIT):
    return pltpu.CompilerParams(dimension_semantics=sem, vmem_limit_bytes=vmem)


def _rmsnorm(x, g):
    return x * lax.rsqrt(jnp.mean(x * x, axis=-1, keepdims=True) + EPS) * g


def _split_bf16(x):
    hi = x.astype(BF16)
    lo = (x - hi.astype(F32)).astype(BF16)
    return hi, lo


def _dot3(a, b, dims):
    a_hi, a_lo = _split_bf16(a)
    b_hi, b_lo = _split_bf16(b)
    d = functools.partial(lax.dot_general, dimension_numbers=dims, preferred_element_type=F32)
    return d(a_hi, b_hi) + d(a_hi, b_lo) + d(a_lo, b_hi)


def _inproj_kernel(x_ref, g_ref, w_ref, cos_ref, sin_ref, o_ref, h_ref):
    j = pl.program_id(1)

    @pl.when(j == 0)
    def _():
        h_ref[...] = _rmsnorm(x_ref[...], g_ref[...]).astype(BF16)

    o_ref[...] = jnp.dot(h_ref[...], w_ref[...], preferred_element_type=F32)

    @pl.when(j < 2)
    def _():
        scale = jnp.where(j == 1, HEAD_DIM ** -0.5, 1.0).astype(F32)
        cos = cos_ref[...]
        sin = sin_ref[...]
        for h in range(COL_BLOCK // HEAD_DIM):
            sl = slice(h * HEAD_DIM, (h + 1) * HEAD_DIM)
            t = o_ref[:, sl]
            o_ref[:, sl] = (t * cos + pltpu.roll(t, HEAD_DIM // 2, axis=1) * sin) * scale


def _in_proj(x2d, g, w_bf16, cos_t, sin_t, seq):
    t = x2d.shape[0]
    tm = min(1024, seq)
    blocks_per_seq = seq // tm
    return pl.pallas_call(
        _inproj_kernel,
        out_shape=jax.ShapeDtypeStruct((t, IN_COLS), F32),
        grid=(t // tm, IN_COLS // COL_BLOCK),
        in_specs=[
            pl.BlockSpec((tm, D_MODEL), lambda i, j: (i, 0)),
            pl.BlockSpec((1, D_MODEL), lambda i, j: (0, 0)),
            pl.BlockSpec((D_MODEL, COL_BLOCK), lambda i, j: (0, j)),
            pl.BlockSpec((tm, HEAD_DIM), lambda i, j: (i % blocks_per_seq, 0)),
            pl.BlockSpec((tm, HEAD_DIM), lambda i, j: (i % blocks_per_seq, 0)),
        ],
        out_specs=pl.BlockSpec((tm, COL_BLOCK), lambda i, j: (i, j)),
        scratch_shapes=[pltpu.VMEM((tm, D_MODEL), BF16)],
        compiler_params=_params(("parallel", "arbitrary")),
        name="in_proj",
    )(x2d, g, w_bf16, cos_t, sin_t)


def _retention_kernel(dec_ref, qf_ref, kf_ref, vf_ref, qb_ref, kb_ref, vb_ref,
                      of_ref, ob_ref,
                      sf_ref, sb_ref, dtab_ref, xif_ref, xib_ref, zf_ref, zb_ref):
    first = jnp.logical_and(pl.program_id(0) == 0, pl.program_id(1) == 0)

    @pl.when(first)
    def _():
        row = lax.broadcasted_iota(I32, (CHUNK, CHUNK), 0).astype(F32)
        col = lax.broadcasted_iota(I32, (CHUNK, CHUNK), 1).astype(F32)
        diff = row - col
        for h in range(RET_HEADS):
            lgf = -jnp.exp(dec_ref[0:1, h:h + 1])
            lgb = -jnp.exp(dec_ref[1:2, h:h + 1])
            dtab_ref[h] = jnp.where(diff >= 0, jnp.exp(lgf * jnp.maximum(diff, 0.0)),
                                    jnp.exp(lgb * jnp.maximum(-diff, 0.0)))
            xif_ref[h] = jnp.exp(lgf * (row + 1.0))
            zf_ref[h] = jnp.exp(lgf * (CHUNK - 1.0 - row))
            xib_ref[h] = jnp.exp(lgb * (CHUNK - row))
            zb_ref[h] = jnp.exp(lgb * row)

    @pl.when(pl.program_id(1) == 0)
    def _():
        sf_ref[...] = jnp.zeros_like(sf_ref)
        sb_ref[...] = jnp.zeros_like(sb_ref)

    nt = (((1,), (1,)), ((), ()))
    tn = (((0,), (0,)), ((), ()))
    for h in range(RET_HEADS):
        sl = slice(h * HEAD_DIM, (h + 1) * HEAD_DIM)
        gcf = jnp.exp(-jnp.exp(dec_ref[0:1, h:h + 1]) * CHUNK)
        gcb = jnp.exp(-jnp.exp(dec_ref[1:2, h:h + 1]) * CHUNK)

        q = qf_ref[:, sl].astype(BF16)
        k32 = kf_ref[:, sl]
        k = k32.astype(BF16)
        v = vf_ref[:, sl].astype(BF16)
        s = lax.dot_general(q, k, nt, preferred_element_type=F32) * dtab_ref[h]
        sf = sf_ref[h]
        o = jnp.dot(s.astype(BF16), v, preferred_element_type=F32)
        o = o + jnp.dot(q, sf.astype(BF16), preferred_element_type=F32) * xif_ref[h]
        of_ref[:, sl] = o
        kz = (k32 * zf_ref[h]).astype(BF16)
        sf_ref[h] = sf * gcf + lax.dot_general(kz, v, tn, preferred_element_type=F32)

        q = qb_ref[:, sl].astype(BF16)
        k32 = kb_ref[:, sl]
        v = vb_ref[:, sl].astype(BF16)
        sb = sb_ref[h]
        ob_ref[:, sl] = jnp.dot(q, sb.astype(BF16), preferred_element_type=F32) * xib_ref[h]
        kz = (k32 * zb_ref[h]).astype(BF16)
        sb_ref[h] = sb * gcb + lax.dot_general(kz, v, tn, preferred_element_type=F32)


def _retention(proj, decays, batch, seq):
    t = proj.shape[0]
    n = seq // CHUNK
    fwd = lambda col: pl.BlockSpec((CHUNK, COL_BLOCK), lambda b, c: (b * n + c, col))
    bwd = lambda col: pl.BlockSpec((CHUNK, COL_BLOCK), lambda b, c: (b * n + n - 1 - c, col))
    table = pltpu.VMEM((RET_HEADS, CHUNK, CHUNK), F32)
    return pl.pallas_call(
        _retention_kernel,
        out_shape=(jax.ShapeDtypeStruct((t, RET_WIDTH), F32), jax.ShapeDtypeStruct((t, RET_WIDTH), F32)),
        grid=(batch, n),
        in_specs=[pl.BlockSpec((2, RET_HEADS), lambda b, c: (0, 0)),
                  fwd(0), fwd(1), fwd(2), bwd(0), bwd(1), bwd(2)],
        out_specs=(pl.BlockSpec((CHUNK, RET_WIDTH), lambda b, c: (b * n + c, 0)),
                   pl.BlockSpec((CHUNK, RET_WIDTH), lambda b, c: (b * n + n - 1 - c, 0))),
        scratch_shapes=[table] * 7,
        compiler_params=_params(("arbitrary", "arbitrary")),
        name="retention",
    )(decays, proj, proj, proj, proj, proj, proj)


def _mixout_kernel(of_ref, ob_ref, g_ref, cb_ref, cc_ref, ch_ref, ccp_ref, chp_ref, ccn_ref, chn_ref,
                   x_ref, gn_ref, cw_ref, wo_ref, g2_ref, wr_ref,
                   x1_ref, acc_ref, aff_ref, *, blocks_per_seq):
    i = pl.program_id(0)
    tm = x_ref.shape[0]

    o = of_ref[...] + ob_ref[...]
    gate_in = g_ref[...]
    gn = gn_ref[...]
    parts = []
    for h in range(RET_HEADS):
        sl = slice(h * HEAD_DIM, (h + 1) * HEAD_DIM)
        oh = o[:, sl]
        mu = jnp.mean(oh, axis=-1, keepdims=True)
        d = oh - mu
        var = jnp.mean(d * d, axis=-1, keepdims=True)
        gh = gate_in[:, sl]
        swish = gh / (1.0 + jnp.exp(-gh))
        parts.append((swish * (d * lax.rsqrt(var + EPS) * gn[:, sl])).astype(BF16))
    ret = jnp.concatenate(parts, axis=-1)

    z = cc_ref[...] * ch_ref[...]
    pos = i % blocks_per_seq
    keep_prev = jnp.where(pos == 0, 0.0, 1.0).astype(F32)
    keep_next = jnp.where(pos == blocks_per_seq - 1, 0.0, 1.0).astype(F32)
    halo_prev = ccp_ref[SUBLANES - 1:SUBLANES, :] * chp_ref[SUBLANES - 1:SUBLANES, :] * keep_prev
    halo_next = ccn_ref[0:1, :] * chn_ref[0:1, :] * keep_next
    row = lax.broadcasted_iota(I32, z.shape, 0)
    z_prev = jnp.where(row == 0, halo_prev, pltpu.roll(z, 1, axis=0))
    z_next = jnp.where(row == tm - 1, halo_next, pltpu.roll(z, tm - 1, axis=0))
    cw = cw_ref[...]
    conv = cb_ref[...] * (z_prev * cw[0:1, :] + z * cw[1:2, :] + z_next * cw[2:3, :])

    y = jnp.dot(ret, wo_ref[0:RET_WIDTH, :], preferred_element_type=F32)
    y = y + jnp.dot(conv.astype(BF16), wo_ref[RET_WIDTH:, :], preferred_element_type=F32)
    x1 = x_ref[...] + y
    x1_ref[...] = x1
    acc_ref[...] = x1

    h2 = _rmsnorm(x1, g2_ref[...])
    logits = _dot3(wr_ref[...], h2, (((1,), (1,)), ((), ())))
    ex = jnp.exp(logits - jnp.max(logits, axis=0, keepdims=True))
    aff_ref[...] = ex / jnp.sum(ex, axis=0, keepdims=True)


def _mix_out(o_f, o_b, proj, x2d, gn_g, conv_w, wo_bf16, g2, wr_t, seq):
    t = x2d.shape[0]
    tm = min(256, seq)
    bps = seq // tm
    r8 = tm // SUBLANES
    n8 = t // SUBLANES
    col = lambda c: pl.BlockSpec((tm, COL_BLOCK), lambda i: (i, c))
    prev = lambda c: pl.BlockSpec((SUBLANES, COL_BLOCK), lambda i: (jnp.maximum(i * r8 - 1, 0), c))
    nxt = lambda c: pl.BlockSpec((SUBLANES, COL_BLOCK), lambda i: (jnp.minimum((i + 1) * r8, n8 - 1), c))
    whole = lambda shape: pl.BlockSpec(shape, lambda i: (0,) * len(shape))
    return pl.pallas_call(
        functools.partial(_mixout_kernel, blocks_per_seq=bps),
        out_shape=(jax.ShapeDtypeStruct((t, D_MODEL), F32), jax.ShapeDtypeStruct((t, D_MODEL), F32),
                   jax.ShapeDtypeStruct((N_EXPERTS, t), F32)),
        grid=(t // tm,),
        in_specs=[
            pl.BlockSpec((tm, RET_WIDTH), lambda i: (i, 0)),
            pl.BlockSpec((tm, RET_WIDTH), lambda i: (i, 0)),
            col(3), col(4), col(5), col(6), prev(5), prev(6), nxt(5), nxt(6),
            pl.BlockSpec((tm, D_MODEL), lambda i: (i, 0)),
            whole((1, RET_WIDTH)), whole((3, COL_BLOCK)), whole((D_MODEL, D_MODEL)),
            whole((1, D_MODEL)), whole((N_EXPERTS, D_MODEL)),
        ],
        out_specs=(pl.BlockSpec((tm, D_MODEL), lambda i: (i, 0)),
                   pl.BlockSpec((tm, D_MODEL), lambda i: (i, 0)),
                   pl.BlockSpec((N_EXPERTS, tm), lambda i: (0, i))),
        compiler_params=_params(("parallel",)),
        name="mix_out",
    )(o_f, o_b, proj, proj, proj, proj, proj, proj, proj, proj, x2d, gn_g, conv_w, wo_bf16, g2, wr_t)


def _select_kernel(aff_ref, idx_ref, sel_ref, *, cap):
    e_n, rows, _ = aff_ref.shape
    bits = lax.bitcast_convert_type(aff_ref[...], I32)

    def count(mask):
        c = jnp.sum(jnp.where(mask, 1.0, 0.0), axis=1, keepdims=True)
        return jnp.sum(c, axis=2, keepdims=True)

    thr = jnp.zeros((e_n, 1, 1), I32)
    for b in range(30, -1, -1):
        cand = thr | (1 << b)
        thr = jnp.where(count(bits >= cand) >= cap, cand, thr)

    gt = bits > thr
    eq = bits == thr
    need = cap - count(gt)

    li = lax.broadcasted_iota(I32, (LANES, LANES), 0)
    lj = lax.broadcasted_iota(I32, (LANES, LANES), 1)
    upper = (li <= lj).astype(BF16)
    ri = lax.broadcasted_iota(I32, (rows, rows), 0)
    rj = lax.broadcasted_iota(I32, (rows, rows), 1)
    lower = (rj < ri).astype(BF16)
    eqf = jnp.where(eq, 1.0, 0.0)
    for e in range(e_n):
        incl = jnp.dot(eqf[e].astype(BF16), upper, preferred_element_type=F32)
        row_tot = jnp.broadcast_to(incl[:, LANES - 1:LANES], (rows, LANES)).astype(BF16)
        row_off = jnp.dot(lower, row_tot, preferred_element_type=F32)
        rank = row_off + incl - eqf[e]
        take = jnp.logical_or(gt[e], jnp.logical_and(eq[e], rank < need[e]))
        sel_ref[e] = jnp.where(take, 1.0, 0.0)

    chunk = min(512, cap)
    lower_incl = (li >= lj).astype(BF16)
    slot0 = lax.broadcasted_iota(I32, (1, chunk), 1).astype(F32)
    rowid = lax.broadcasted_iota(I32, (rows, chunk), 0).astype(F32)

    def compact(e, carry):
        s = sel_ref[e].astype(BF16)
        q_t = lax.dot_general(lower_incl, s, (((1,), (1,)), ((), ())),
                              preferred_element_type=F32)
        q = jnp.dot(s, upper, preferred_element_type=F32)
        row_tot = q[:, LANES - 1:LANES]
        row_cum = jnp.dot(lower, jnp.broadcast_to(row_tot, (rows, LANES)).astype(BF16),
                          preferred_element_type=F32)[:, 0:1] + row_tot
        q_t = q_t.astype(BF16)
        for ch in range(cap // chunk):
            c = slot0 + float(ch * chunk)
            before = row_cum <= c
            r_c = jnp.sum(jnp.where(before, 1.0, 0.0), axis=0, keepdims=True)
            k = c - jnp.sum(jnp.where(before, row_tot, 0.0), axis=0, keepdims=True)
            onehot = jnp.where(rowid == r_c, 1.0, 0.0).astype(BF16)
            q_row = jnp.dot(q_t, onehot, preferred_element_type=F32)
            lane_c = jnp.sum(jnp.where(q_row <= k, 1.0, 0.0), axis=0, keepdims=True)
            idx_ref[pl.ds(e, 1), ch * chunk:(ch + 1) * chunk] = (r_c * LANES + lane_c).astype(I32)
        return carry

    lax.fori_loop(0, e_n, compact, 0)


def _select(aff3, cap):
    e_n, rows, lanes = aff3.shape
    return pl.pallas_call(
        functools.partial(_select_kernel, cap=cap),
        out_shape=jax.ShapeDtypeStruct((e_n, cap), I32),
        grid=(1,),
        in_specs=[pl.BlockSpec((e_n, rows, lanes), lambda i: (0, 0, 0))],
        out_specs=pl.BlockSpec((e_n, cap), lambda i: (0, 0)),
        scratch_shapes=[pltpu.VMEM((e_n, rows, lanes), F32)],
        compiler_params=_params(("arbitrary",)),
        name="select",
    )(aff3)


def _experts_kernel(idx_ref, idx_next_ref, x1_hbm, acc_in_hbm, wg_ref, wu_ref, wd_ref, g2_ref, wr_ref,
                    acc_hbm, xbuf, xn_ref, yacc_ref, gate_ref, sem_x, sem_a, sem_s):
    del acc_in_hbm
    e = pl.program_id(0)
    c = pl.program_id(1)
    f = pl.program_id(2)
    ne = pl.num_programs(0)
    nt = pl.num_programs(1)
    nf = pl.num_programs(2)
    tc = xn_ref.shape[0]
    g = e * nt + c
    slot = g % 2
    other = 1 - slot

    def start_gather(src_hbm, ids_ref, base, buf, sem):
        def issue(r, carry):
            pltpu.make_async_copy(src_hbm.at[pl.ds(ids_ref[base + r], 1), :],
                                  xbuf.at[buf, pl.ds(r, 1), :], sem.at[buf]).start()
            return carry
        lax.fori_loop(0, tc, issue, 0, unroll=8)

    def wait_rows(src_hbm, buf, sem):
        pltpu.make_async_copy(src_hbm.at[pl.ds(0, tc), :], xbuf.at[buf], sem.at[buf]).wait()

    def wait_scatter(buf):
        pltpu.make_async_copy(xbuf.at[buf], acc_hbm.at[pl.ds(0, tc), :], sem_s.at[buf]).wait()

    @pl.when(jnp.logical_and(g == 0, f == 0))
    def _():
        start_gather(x1_hbm, idx_ref, 0, 0, sem_x)

    rb = min(EXPERT_ROWS, tc)

    def row_blocks(body):
        def step(m, carry):
            body(pl.ds(pl.multiple_of(m * rb, rb), rb))
            return carry
        lax.fori_loop(0, tc // rb, step, 0)

    @pl.when(f == 0)
    def _():
        wait_rows(x1_hbm, slot, sem_x)

        def prep(rows):
            h2 = _rmsnorm(xbuf[slot, rows, :], g2_ref[...])
            xn_ref[rows, :] = h2.astype(BF16)
            logits = _dot3(h2, wr_ref[...], (((1,), (0,)), ((), ())))
            ex = jnp.exp(logits - jnp.max(logits, axis=-1, keepdims=True))
            aff = ex / jnp.sum(ex, axis=-1, keepdims=True)
            lane = lax.broadcasted_iota(I32, aff.shape, 1)
            gate_ref[rows, :] = jnp.sum(jnp.where(lane == e, aff, 0.0), axis=-1, keepdims=True)
            yacc_ref[rows, :] = jnp.zeros((rb, D_MODEL), F32)
        row_blocks(prep)

        @pl.when(g > 0)
        def _():
            wait_scatter(other)

        start_gather(acc_hbm, idx_ref, c * tc, slot, sem_a)

        @pl.when(c + 1 < nt)
        def _():
            start_gather(x1_hbm, idx_ref, (c + 1) * tc, other, sem_x)

        @pl.when(jnp.logical_and(c + 1 == nt, e + 1 < ne))
        def _():
            start_gather(x1_hbm, idx_next_ref, 0, other, sem_x)

    def ffn(rows):
        xn = xn_ref[rows, :]
        hg = jnp.dot(xn, wg_ref[0], preferred_element_type=F32)
        hu = jnp.dot(xn, wu_ref[0], preferred_element_type=F32)
        hid = (hg / (1.0 + jnp.exp(-hg)) * hu).astype(BF16)
        yacc_ref[rows, :] += jnp.dot(hid, wd_ref[0], preferred_element_type=F32)
    row_blocks(ffn)

    @pl.when(f == nf - 1)
    def _():
        wait_rows(acc_hbm, slot, sem_a)

        def combine(rows):
            xbuf[slot, rows, :] = xbuf[slot, rows, :] + yacc_ref[rows, :] * gate_ref[rows, :]
        row_blocks(combine)

        def issue(r, carry):
            pltpu.make_async_copy(xbuf.at[slot, pl.ds(r, 1), :],
                                  acc_hbm.at[pl.ds(idx_ref[c * tc + r], 1), :], sem_s.at[slot]).start()
            return carry
        lax.fori_loop(0, tc, issue, 0, unroll=8)

        @pl.when(jnp.logical_and(e == ne - 1, c == nt - 1))
        def _():
            wait_scatter(slot)


def _experts(idx_flat, x1, acc, wg, wu, wd, g2, wr, cap):
    t = x1.shape[0]
    tc = min(EXPERT_TILE, cap)
    tf = 512
    return pl.pallas_call(
        _experts_kernel,
        out_shape=jax.ShapeDtypeStruct((t, D_MODEL), F32),
        grid=(N_EXPERTS, cap // tc, EXPERT_FF // tf),
        in_specs=[
            pl.BlockSpec((cap,), lambda e, c, f: (e,), memory_space=pltpu.SMEM),
            pl.BlockSpec((cap,), lambda e, c, f: (jnp.minimum(e + 1, N_EXPERTS - 1),), memory_space=pltpu.SMEM),
            pl.BlockSpec(memory_space=pl.ANY),
            pl.BlockSpec(memory_space=pl.ANY),
            pl.BlockSpec((1, D_MODEL, tf), lambda e, c, f: (e, 0, f)),
            pl.BlockSpec((1, D_MODEL, tf), lambda e, c, f: (e, 0, f)),
            pl.BlockSpec((1, tf, D_MODEL), lambda e, c, f: (e, f, 0)),
            pl.BlockSpec((1, D_MODEL), lambda e, c, f: (0, 0)),
            pl.BlockSpec((D_MODEL, N_EXPERTS), lambda e, c, f: (0, 0)),
        ],
        out_specs=pl.BlockSpec(memory_space=pl.ANY),
        scratch_shapes=[
            pltpu.VMEM((2, tc, D_MODEL), F32),
            pltpu.VMEM((tc, D_MODEL), BF16),
            pltpu.VMEM((tc, D_MODEL), F32),
            pltpu.VMEM((tc, 1), F32),
            pltpu.SemaphoreType.DMA((2,)),
            pltpu.SemaphoreType.DMA((2,)),
            pltpu.SemaphoreType.DMA((2,)),
        ],
        input_output_aliases={3: 0},
        compiler_params=_params(("arbitrary", "arbitrary", "arbitrary")),
        name="experts",
    )(idx_flat, idx_flat, x1, acc, wg, wu, wd, g2, wr)


def _final_kernel(x_ref, g_ref, o_ref):
    o_ref[...] = _rmsnorm(x_ref[...], g_ref[...])


def _final_norm(x2d, g):
    t = x2d.shape[0]
    tm = min(512, t)
    return pl.pallas_call(
        _final_kernel,
        out_shape=jax.ShapeDtypeStruct((t, D_MODEL), F32),
        grid=(t // tm,),
        in_specs=[pl.BlockSpec((tm, D_MODEL), lambda i: (i, 0)), pl.BlockSpec((1, D_MODEL), lambda i: (0, 0))],
        out_specs=pl.BlockSpec((tm, D_MODEL), lambda i: (i, 0)),
        compiler_params=_params(("parallel",)),
        name="final_norm",
    )(x2d, g)


def _rotary_tables(seq):
    half = HEAD_DIM // 2
    inv = ROPE_BASE ** (-jnp.arange(half, dtype=F32) / half)
    ang = jnp.arange(seq, dtype=F32)[:, None] * inv[None, :]
    cos, sin = jnp.cos(ang), jnp.sin(ang)
    return jnp.concatenate([cos, cos], axis=-1), jnp.concatenate([-sin, sin], axis=-1)


def _trunk(x, w):
    batch, seq, _ = x.shape
    t = batch * seq
    cap = EC_CAPACITY_FACTOR * t // N_EXPERTS
    x2d = x.reshape(t, D_MODEL)
    cos_t, sin_t = _rotary_tables(seq)

    proj = _in_proj(x2d, w["norm_mix_g"], w["w_in"], cos_t, sin_t, seq)
    o_f, o_b = _retention(proj, w["decays"], batch, seq)
    x1, acc, aff = _mix_out(o_f, o_b, proj, x2d, w["ret_gn_g"], w["conv_w"], w["w_o"],
                            w["norm_ffn_g"], w["w_router_t"], seq)
    idx = _select(aff.reshape(N_EXPERTS, t // LANES, LANES), cap)
    acc = _experts(idx.reshape(N_EXPERTS * cap), x1, acc, w["w_gate"], w["w_up"], w["w_down"], w["norm_ffn_g"], w["w_router"], cap)
    return _final_norm(acc, w["final_norm_g"]).reshape(batch, seq, D_MODEL)


def kernel(x_prompt, x_sample, norm_mix_g, w_in, conv_w, ret_decay_fwd, ret_decay_bwd, ret_gn_g, w_o,
           norm_ffn_g, w_router, w_gate, w_up, w_down, final_norm_g):
    w = {
        "norm_mix_g": norm_mix_g[0][None, :],
        "w_in": w_in[0].astype(BF16),
        "conv_w": conv_w[0],
        "decays": jnp.stack([ret_decay_fwd[0], ret_decay_bwd[0]]),
        "ret_gn_g": ret_gn_g[0][None, :],
        "w_o": w_o[0].astype(BF16),
        "norm_ffn_g": norm_ffn_g[0][None, :],
        "w_router": w_router[0],
        "w_router_t": w_router[0].T,
        "w_gate": w_gate[0].astype(BF16),
        "w_up": w_up[0].astype(BF16),
        "w_down": w_down[0].astype(BF16),
        "final_norm_g": final_norm_g[None, :],
    }
    return (_trunk(x_prompt, w), _trunk(x_sample, w))
```

```python
import functools

import jax
import jax.numpy as jnp
from jax import lax
from jax.experimental import pallas as pl
from jax.experimental.pallas import tpu as pltpu

F32 = jnp.float32
BF16 = jnp.bfloat16
I32 = jnp.int32

D_MODEL = 2048
RET_WIDTH = 1024
RET_HEADS = 8
HEAD_DIM = 128
IN_COLS = 7168
COL_BLOCK = 1024
N_EXPERTS = 16
EC_CAPACITY_FACTOR = 2
EXPERT_FF = 2048
CHUNK = 128
ROPE_BASE = 10000.0
EPS = 1e-6
LANES = 128
SUBLANES = 8
HALO_ROWS = 16
TOK_COLS = D_MODEL + LANES
EXPERT_TILE = 1024
EXPERT_ROWS = 256
VMEM_LIMIT = 56 * 1024 * 1024


def _params(sem, vmem=VMEM_LIMIT):
    return pltpu.CompilerParams(dimension_semantics=sem, vmem_limit_bytes=vmem)


def _rmsnorm(x, g):
    return x * lax.rsqrt(jnp.mean(x * x, axis=-1, keepdims=True) + EPS) * g


def _split_bf16(x):
    hi = x.astype(BF16)
    lo = (x - hi.astype(F32)).astype(BF16)
    return hi, lo


def _dot3(a, b, dims):
    a_hi, a_lo = _split_bf16(a)
    b_hi, b_lo = _split_bf16(b)
    d = functools.partial(lax.dot_general, dimension_numbers=dims, preferred_element_type=F32)
    return d(a_hi, b_hi) + d(a_hi, b_lo) + d(a_lo, b_hi)


def _inproj_kernel(x_ref, g_ref, w_ref, cos_ref, sin_ref, o_ref, h_ref, acc_ref):
    j = pl.program_id(1)

    @pl.when(j == 0)
    def _():
        h_ref[...] = _rmsnorm(x_ref[...], g_ref[...]).astype(BF16)

    acc_ref[...] = jnp.dot(h_ref[...], w_ref[...], preferred_element_type=F32)

    @pl.when(j < 2)
    def _():
        scale = jnp.where(j == 1, HEAD_DIM ** -0.5, 1.0).astype(F32)
        cos = cos_ref[...]
        sin = sin_ref[...]
        for h in range(COL_BLOCK // HEAD_DIM):
            sl = slice(h * HEAD_DIM, (h + 1) * HEAD_DIM)
            t = acc_ref[:, sl]
            o_ref[:, sl] = ((t * cos + pltpu.roll(t, HEAD_DIM // 2, axis=1) * sin) * scale).astype(BF16)

    @pl.when(j >= 2)
    def _():
        o_ref[...] = acc_ref[...].astype(BF16)


def _in_proj(x2d, g, w_bf16, cos_t, sin_t, seq):
    t = x2d.shape[0]
    tm = min(1024, seq)
    blocks_per_seq = seq // tm
    return pl.pallas_call(
        _inproj_kernel,
        out_shape=jax.ShapeDtypeStruct((t, IN_COLS), BF16),
        grid=(t // tm, IN_COLS // COL_BLOCK),
        in_specs=[
            pl.BlockSpec((tm, D_MODEL), lambda i, j: (i, 0)),
            pl.BlockSpec((1, D_MODEL), lambda i, j: (0, 0)),
            pl.BlockSpec((D_MODEL, COL_BLOCK), lambda i, j: (0, j)),
            pl.BlockSpec((tm, HEAD_DIM), lambda i, j: (i % blocks_per_seq, 0)),
            pl.BlockSpec((tm, HEAD_DIM), lambda i, j: (i % blocks_per_seq, 0)),
        ],
        out_specs=pl.BlockSpec((tm, COL_BLOCK), lambda i, j: (i, j)),
        scratch_shapes=[pltpu.VMEM((tm, D_MODEL), BF16), pltpu.VMEM((tm, COL_BLOCK), F32)],
        compiler_params=_params(("parallel", "arbitrary")),
        name="in_proj",
    )(x2d, g, w_bf16, cos_t, sin_t)


def _retention_kernel(dec_ref, qf_ref, kf_ref, vf_ref, qb_ref, kb_ref, vb_ref,
                      of_ref, ob_ref,
                      sf_ref, sb_ref, dtab_ref, xif_ref, xib_ref, zf_ref, zb_ref):
    first = jnp.logical_and(pl.program_id(0) == 0, pl.program_id(1) == 0)

    @pl.when(first)
    def _():
        row = lax.broadcasted_iota(I32, (CHUNK, CHUNK), 0).astype(F32)
        col = lax.broadcasted_iota(I32, (CHUNK, CHUNK), 1).astype(F32)
        diff = row - col
        for h in range(RET_HEADS):
            lgf = -jnp.exp(dec_ref[0:1, h:h + 1])
            lgb = -jnp.exp(dec_ref[1:2, h:h + 1])
            dtab_ref[h] = jnp.where(diff >= 0, jnp.exp(lgf * jnp.maximum(diff, 0.0)),
                                    jnp.exp(lgb * jnp.maximum(-diff, 0.0)))
            xif_ref[h] = jnp.exp(lgf * (row + 1.0))
            zf_ref[h] = jnp.exp(lgf * (CHUNK - 1.0 - row))
            xib_ref[h] = jnp.exp(lgb * (CHUNK - row))
            zb_ref[h] = jnp.exp(lgb * row)

    @pl.when(pl.program_id(1) == 0)
    def _():
        sf_ref[...] = jnp.zeros_like(sf_ref)
        sb_ref[...] = jnp.zeros_like(sb_ref)

    nt = (((1,), (1,)), ((), ()))
    tn = (((0,), (0,)), ((), ()))
    for h in range(RET_HEADS):
        sl = slice(h * HEAD_DIM, (h + 1) * HEAD_DIM)
        gcf = jnp.exp(-jnp.exp(dec_ref[0:1, h:h + 1]) * CHUNK)
        gcb = jnp.exp(-jnp.exp(dec_ref[1:2, h:h + 1]) * CHUNK)

        q = qf_ref[:, sl]
        k = kf_ref[:, sl]
        k32 = k.astype(F32)
        v = vf_ref[:, sl]
        s = lax.dot_general(q, k, nt, preferred_element_type=F32) * dtab_ref[h]
        sf = sf_ref[h]
        o = jnp.dot(s.astype(BF16), v, preferred_element_type=F32)
        o = o + jnp.dot(q, sf.astype(BF16), preferred_element_type=F32) * xif_ref[h]
        of_ref[:, sl] = o
        kz = (k32 * zf_ref[h]).astype(BF16)
        sf_ref[h] = sf * gcf + lax.dot_general(kz, v, tn, preferred_element_type=F32)

        q = qb_ref[:, sl]
        k32 = kb_ref[:, sl].astype(F32)
        v = vb_ref[:, sl]
        sb = sb_ref[h]
        ob_ref[:, sl] = jnp.dot(q, sb.astype(BF16), preferred_element_type=F32) * xib_ref[h]
        kz = (k32 * zb_ref[h]).astype(BF16)
        sb_ref[h] = sb * gcb + lax.dot_general(kz, v, tn, preferred_element_type=F32)


def _retention(proj, decays, batch, seq):
    t = proj.shape[0]
    n = seq // CHUNK
    fwd = lambda col: pl.BlockSpec((CHUNK, COL_BLOCK), lambda b, c: (b * n + c, col))
    bwd = lambda col: pl.BlockSpec((CHUNK, COL_BLOCK), lambda b, c: (b * n + n - 1 - c, col))
    table = pltpu.VMEM((RET_HEADS, CHUNK, CHUNK), F32)
    return pl.pallas_call(
        _retention_kernel,
        out_shape=(jax.ShapeDtypeStruct((t, RET_WIDTH), F32), jax.ShapeDtypeStruct((t, RET_WIDTH), F32)),
        grid=(batch, n),
        in_specs=[pl.BlockSpec((2, RET_HEADS), lambda b, c: (0, 0)),
                  fwd(0), fwd(1), fwd(2), bwd(0), bwd(1), bwd(2)],
        out_specs=(pl.BlockSpec((CHUNK, RET_WIDTH), lambda b, c: (b * n + c, 0)),
                   pl.BlockSpec((CHUNK, RET_WIDTH), lambda b, c: (b * n + n - 1 - c, 0))),
        scratch_shapes=[table] * 7,
        compiler_params=_params(("arbitrary", "arbitrary")),
        name="retention",
    )(decays, proj, proj, proj, proj, proj, proj)


def _mixout_kernel(of_ref, ob_ref, g_ref, cb_ref, cc_ref, ch_ref, ccp_ref, chp_ref, ccn_ref, chn_ref,
                   x_ref, gn_ref, cw_ref, wo_ref, g2_ref, wr_ref,
                   acc_ref, tok_ref, aff_ref, *, blocks_per_seq):
    i = pl.program_id(0)
    tm = x_ref.shape[0]

    o = of_ref[...] + ob_ref[...]
    gate_in = g_ref[...].astype(F32)
    gn = gn_ref[...]
    parts = []
    for h in range(RET_HEADS):
        sl = slice(h * HEAD_DIM, (h + 1) * HEAD_DIM)
        oh = o[:, sl]
        mu = jnp.mean(oh, axis=-1, keepdims=True)
        d = oh - mu
        var = jnp.mean(d * d, axis=-1, keepdims=True)
        gh = gate_in[:, sl]
        swish = gh / (1.0 + jnp.exp(-gh))
        parts.append((swish * (d * lax.rsqrt(var + EPS) * gn[:, sl])).astype(BF16))
    ret = jnp.concatenate(parts, axis=-1)

    z = cc_ref[...].astype(F32) * ch_ref[...].astype(F32)
    pos = i % blocks_per_seq
    keep_prev = jnp.where(pos == 0, 0.0, 1.0).astype(F32)
    keep_next = jnp.where(pos == blocks_per_seq - 1, 0.0, 1.0).astype(F32)
    last = HALO_ROWS - 1
    halo_prev = (ccp_ref[...].astype(F32) * chp_ref[...].astype(F32))[last:, :] * keep_prev
    halo_next = (ccn_ref[...].astype(F32) * chn_ref[...].astype(F32))[0:1, :] * keep_next
    row = lax.broadcasted_iota(I32, z.shape, 0)
    z_prev = jnp.where(row == 0, halo_prev, pltpu.roll(z, 1, axis=0))
    z_next = jnp.where(row == tm - 1, halo_next, pltpu.roll(z, tm - 1, axis=0))
    cw = cw_ref[...]
    conv = cb_ref[...].astype(F32) * (z_prev * cw[0:1, :] + z * cw[1:2, :] + z_next * cw[2:3, :])

    y = jnp.dot(ret, wo_ref[0:RET_WIDTH, :], preferred_element_type=F32)
    y = y + jnp.dot(conv.astype(BF16), wo_ref[RET_WIDTH:, :], preferred_element_type=F32)
    x1 = x_ref[...] + y
    acc_ref[...] = x1

    h2 = _rmsnorm(x1, g2_ref[...])
    tok_ref[:, 0:D_MODEL] = h2
    logits = _dot3(wr_ref[...], h2, (((1,), (1,)), ((), ())))
    ex = jnp.exp(logits - jnp.max(logits, axis=0, keepdims=True))
    aff = ex / jnp.sum(ex, axis=0, keepdims=True)
    aff_ref[...] = aff

    eye = (lax.broadcasted_iota(I32, (N_EXPERTS, LANES), 0)
           == lax.broadcasted_iota(I32, (N_EXPERTS, LANES), 1)).astype(BF16)
    a1 = aff.astype(BF16)
    r1 = aff - a1.astype(F32)
    a2 = r1.astype(BF16)
    a3 = (r1 - a2.astype(F32)).astype(BF16)
    tn = (((0,), (0,)), ((), ()))
    tr = lambda a: lax.dot_general(a, eye, tn, preferred_element_type=F32)
    tok_ref[:, D_MODEL:] = (tr(a1) + tr(a2)) + tr(a3)


def _mix_out(o_f, o_b, proj, x2d, gn_g, conv_w, wo_bf16, g2, wr_t, seq):
    t = x2d.shape[0]
    tm = min(256, seq)
    bps = seq // tm
    r8 = tm // HALO_ROWS
    n8 = t // HALO_ROWS
    col = lambda c: pl.BlockSpec((tm, COL_BLOCK), lambda i: (i, c))
    prev = lambda c: pl.BlockSpec((HALO_ROWS, COL_BLOCK), lambda i: (jnp.maximum(i * r8 - 1, 0), c))
    nxt = lambda c: pl.BlockSpec((HALO_ROWS, COL_BLOCK), lambda i: (jnp.minimum((i + 1) * r8, n8 - 1), c))
    whole = lambda shape: pl.BlockSpec(shape, lambda i: (0,) * len(shape))
    return pl.pallas_call(
        functools.partial(_mixout_kernel, blocks_per_seq=bps),
        out_shape=(jax.ShapeDtypeStruct((t, D_MODEL), F32), jax.ShapeDtypeStruct((t, TOK_COLS), F32),
                   jax.ShapeDtypeStruct((N_EXPERTS, t), F32)),
        grid=(t // tm,),
        in_specs=[
            pl.BlockSpec((tm, RET_WIDTH), lambda i: (i, 0)),
            pl.BlockSpec((tm, RET_WIDTH), lambda i: (i, 0)),
            col(3), col(4), col(5), col(6), prev(5), prev(6), nxt(5), nxt(6),
            pl.BlockSpec((tm, D_MODEL), lambda i: (i, 0)),
            whole((1, RET_WIDTH)), whole((3, COL_BLOCK)), whole((D_MODEL, D_MODEL)),
            whole((1, D_MODEL)), whole((N_EXPERTS, D_MODEL)),
        ],
        out_specs=(pl.BlockSpec((tm, D_MODEL), lambda i: (i, 0)),
                   pl.BlockSpec((tm, TOK_COLS), lambda i: (i, 0)),
                   pl.BlockSpec((N_EXPERTS, tm), lambda i: (0, i))),
        compiler_params=_params(("parallel",)),
        name="mix_out",
    )(o_f, o_b, proj, proj, proj, proj, proj, proj, proj, proj, x2d, gn_g, conv_w, wo_bf16, g2, wr_t)


def _select_kernel(aff_ref, idx_ref, sel_ref, *, cap):
    e_n, rows, _ = aff_ref.shape
    bits = lax.bitcast_convert_type(aff_ref[...], I32)

    def count(mask):
        c = jnp.sum(jnp.where(mask, 1.0, 0.0), axis=1, keepdims=True)
        return jnp.sum(c, axis=2, keepdims=True)

    thr = jnp.zeros((e_n, 1, 1), I32)
    for b in range(30, -1, -1):
        cand = thr | (1 << b)
        thr = jnp.where(count(bits >= cand) >= cap, cand, thr)

    gt = bits > thr
    eq = bits == thr
    need = cap - count(gt)

    li = lax.broadcasted_iota(I32, (LANES, LANES), 0)
    lj = lax.broadcasted_iota(I32, (LANES, LANES), 1)
    upper = (li <= lj).astype(BF16)
    ri = lax.broadcasted_iota(I32, (rows, rows), 0)
    rj = lax.broadcasted_iota(I32, (rows, rows), 1)
    lower = (rj < ri).astype(BF16)
    eqf = jnp.where(eq, 1.0, 0.0)
    for e in range(e_n):
        incl = jnp.dot(eqf[e].astype(BF16), upper, preferred_element_type=F32)
        row_tot = jnp.broadcast_to(incl[:, LANES - 1:LANES], (rows, LANES)).astype(BF16)
        row_off = jnp.dot(lower, row_tot, preferred_element_type=F32)
        rank = row_off + incl - eqf[e]
        take = jnp.logical_or(gt[e], jnp.logical_and(eq[e], rank < need[e]))
        sel_ref[e] = jnp.where(take, 1.0, 0.0)

    chunk = min(512, cap)
    lower_incl = (li >= lj).astype(BF16)
    slot0 = lax.broadcasted_iota(I32, (1, chunk), 1).astype(F32)
    rowid = lax.broadcasted_iota(I32, (rows, chunk), 0).astype(F32)

    def compact(e, carry):
        s = sel_ref[e].astype(BF16)
        q_t = lax.dot_general(lower_incl, s, (((1,), (1,)), ((), ())),
                              preferred_element_type=F32)
        q = jnp.dot(s, upper, preferred_element_type=F32)
        row_tot = q[:, LANES - 1:LANES]
        row_cum = jnp.dot(lower, jnp.broadcast_to(row_tot, (rows, LANES)).astype(BF16),
                          preferred_element_type=F32)[:, 0:1] + row_tot
        q_t = q_t.astype(BF16)
        for ch in range(cap // chunk):
            c = slot0 + float(ch * chunk)
            before = row_cum <= c
            r_c = jnp.sum(jnp.where(before, 1.0, 0.0), axis=0, keepdims=True)
            k = c - jnp.sum(jnp.where(before, row_tot, 0.0), axis=0, keepdims=True)
            onehot = jnp.where(rowid == r_c, 1.0, 0.0).astype(BF16)
            q_row = jnp.dot(q_t, onehot, preferred_element_type=F32)
            lane_c = jnp.sum(jnp.where(q_row <= k, 1.0, 0.0), axis=0, keepdims=True)
            idx_ref[pl.ds(e, 1), ch * chunk:(ch + 1) * chunk] = (r_c * LANES + lane_c).astype(I32)
        return carry

    lax.fori_loop(0, e_n, compact, 0)


def _select(aff3, cap):
    e_n, rows, lanes = aff3.shape
    return pl.pallas_call(
        functools.partial(_select_kernel, cap=cap),
        out_shape=jax.ShapeDtypeStruct((e_n, cap), I32),
        grid=(1,),
        in_specs=[pl.BlockSpec((e_n, rows, lanes), lambda i: (0, 0, 0))],
        out_specs=pl.BlockSpec((e_n, cap), lambda i: (0, 0)),
        scratch_shapes=[pltpu.VMEM((e_n, rows, lanes), F32)],
        compiler_params=_params(("arbitrary",)),
        name="select",
    )(aff3)


def _experts_kernel(idx_ref, idx_next_ref, tok_hbm, acc_in_hbm, wg_ref, wu_ref, wd_ref,
                    acc_hbm, xbuf, xn_ref, yacc_ref, gate_ref, sem_x, sem_a, sem_s):
    del acc_in_hbm
    e = pl.program_id(0)
    c = pl.program_id(1)
    f = pl.program_id(2)
    ne = pl.num_programs(0)
    nt = pl.num_programs(1)
    nf = pl.num_programs(2)
    tc = xn_ref.shape[0]
    ng = tc // SUBLANES
    g = e * nt + c
    slot = g % 2
    other = 1 - slot

    def hbm_row(ref, t):
        return ref.at[lax.shift_right_logical(t, 3), pl.ds(jnp.bitwise_and(t, SUBLANES - 1), 1), :]

    def buf_row(buf, grp, j, width):
        return xbuf.at[buf, grp, pl.ds(j, 1), pl.ds(0, width)]

    def start_gather(src_hbm, width, ids_ref, base, buf, sem):
        def issue(grp, carry):
            for j in range(SUBLANES):
                t = ids_ref[base + grp * SUBLANES + j]
                pltpu.make_async_copy(hbm_row(src_hbm, t), buf_row(buf, grp, j, width), sem.at[buf]).start()
            return carry
        lax.fori_loop(0, ng, issue, 0)

    def buf_rows(buf, width):
        return xbuf.at[buf, pl.ds(0, ng), pl.ds(0, SUBLANES), pl.ds(0, width)]

    def wait_gather(src_hbm, width, buf, sem):
        pltpu.make_async_copy(src_hbm.at[pl.ds(0, ng)], buf_rows(buf, width), sem.at[buf]).wait()

    def wait_scatter(buf):
        pltpu.make_async_copy(buf_rows(buf, D_MODEL), acc_hbm.at[pl.ds(0, ng)], sem_s.at[buf]).wait()

    rb = min(EXPERT_ROWS, tc)
    gb = rb // SUBLANES

    def row_blocks(body):
        def step(m, carry):
            body(pl.ds(pl.multiple_of(m * rb, rb), rb), pl.ds(pl.multiple_of(m * gb, gb), gb), m)
            return carry
        lax.fori_loop(0, tc // rb, step, 0)

    @pl.when(jnp.logical_and(g == 0, f == 0))
    def _():
        start_gather(tok_hbm, TOK_COLS, idx_ref, 0, 0, sem_x)

    @pl.when(f == 0)
    def _():
        wait_gather(tok_hbm, TOK_COLS, slot, sem_x)

        def prep(rows, grps, m):
            del m
            tok = xbuf[slot, grps, :, :].reshape(rb, TOK_COLS)
            xn_ref[rows, :] = tok[:, 0:D_MODEL].astype(BF16)
            aff = tok[:, D_MODEL:]
            lane = lax.broadcasted_iota(I32, aff.shape, 1)
            gate_ref[rows, :] = jnp.sum(jnp.where(lane == e, aff, 0.0), axis=-1, keepdims=True)
            yacc_ref[rows, :] = jnp.zeros((rb, D_MODEL), F32)
        row_blocks(prep)

        @pl.when(g > 0)
        def _():
            wait_scatter(other)

        start_gather(acc_hbm, D_MODEL, idx_ref, c * tc, slot, sem_a)

        @pl.when(c + 1 < nt)
        def _():
            start_gather(tok_hbm, TOK_COLS, idx_ref, (c + 1) * tc, other, sem_x)

        @pl.when(jnp.logical_and(c + 1 == nt, e + 1 < ne))
        def _():
            start_gather(tok_hbm, TOK_COLS, idx_next_ref, 0, other, sem_x)

    def ffn(rows, grps, m):
        del grps, m
        xn = xn_ref[rows, :]
        hg = jnp.dot(xn, wg_ref[0], preferred_element_type=F32)
        hu = jnp.dot(xn, wu_ref[0], preferred_element_type=F32)
        hid = (hg / (1.0 + jnp.exp(-hg)) * hu).astype(BF16)
        yacc_ref[rows, :] += jnp.dot(hid, wd_ref[0], preferred_element_type=F32)
    row_blocks(ffn)

    @pl.when(f == nf - 1)
    def _():
        wait_gather(acc_hbm, D_MODEL, slot, sem_a)

        def combine(rows, grps, m):
            del m
            acc = xbuf[slot, grps, :, 0:D_MODEL].reshape(rb, D_MODEL) + yacc_ref[rows, :] * gate_ref[rows, :]
            xbuf[slot, grps, :, 0:D_MODEL] = acc.reshape(gb, SUBLANES, D_MODEL)
        row_blocks(combine)

        def issue(grp, carry):
            for j in range(SUBLANES):
                t = idx_ref[c * tc + grp * SUBLANES + j]
                pltpu.make_async_copy(buf_row(slot, grp, j, D_MODEL), hbm_row(acc_hbm, t), sem_s.at[slot]).start()
            return carry
        lax.fori_loop(0, ng, issue, 0)

        @pl.when(jnp.logical_and(e == ne - 1, c == nt - 1))
        def _():
            wait_scatter(slot)


def _experts(idx_flat, tok, acc, wg, wu, wd, cap):
    t = tok.shape[0]
    tc = min(EXPERT_TILE, cap)
    tf = 512
    out = pl.pallas_call(
        _experts_kernel,
        out_shape=jax.ShapeDtypeStruct((t // SUBLANES, SUBLANES, D_MODEL), F32),
        grid=(N_EXPERTS, cap // tc, EXPERT_FF // tf),
        in_specs=[
            pl.BlockSpec((cap,), lambda e, c, f: (e,), memory_space=pltpu.SMEM),
            pl.BlockSpec((cap,), lambda e, c, f: (jnp.minimum(e + 1, N_EXPERTS - 1),), memory_space=pltpu.SMEM),
            pl.BlockSpec(memory_space=pl.ANY),
            pl.BlockSpec(memory_space=pl.ANY),
            pl.BlockSpec((1, D_MODEL, tf), lambda e, c, f: (e, 0, f)),
            pl.BlockSpec((1, D_MODEL, tf), lambda e, c, f: (e, 0, f)),
            pl.BlockSpec((1, tf, D_MODEL), lambda e, c, f: (e, f, 0)),
        ],
        out_specs=pl.BlockSpec(memory_space=pl.ANY),
        scratch_shapes=[
            pltpu.VMEM((2, tc // SUBLANES, SUBLANES, TOK_COLS), F32),
            pltpu.VMEM((tc, D_MODEL), BF16),
            pltpu.VMEM((tc, D_MODEL), F32),
            pltpu.VMEM((tc, 1), F32),
            pltpu.SemaphoreType.DMA((2,)),
            pltpu.SemaphoreType.DMA((2,)),
            pltpu.SemaphoreType.DMA((2,)),
        ],
        input_output_aliases={3: 0},
        compiler_params=_params(("arbitrary", "arbitrary", "arbitrary")),
        name="experts",
    )(idx_flat, idx_flat, tok.reshape(t // SUBLANES, SUBLANES, TOK_COLS),
      acc.reshape(t // SUBLANES, SUBLANES, D_MODEL), wg, wu, wd)
    return out.reshape(t, D_MODEL)


def _final_kernel(x_ref, g_ref, o_ref):
    o_ref[...] = _rmsnorm(x_ref[...], g_ref[...])


def _final_norm(x2d, g):
    t = x2d.shape[0]
    tm = min(512, t)
    return pl.pallas_call(
        _final_kernel,
        out_shape=jax.ShapeDtypeStruct((t, D_MODEL), F32),
        grid=(t // tm,),
        in_specs=[pl.BlockSpec((tm, D_MODEL), lambda i: (i, 0)), pl.BlockSpec((1, D_MODEL), lambda i: (0, 0))],
        out_specs=pl.BlockSpec((tm, D_MODEL), lambda i: (i, 0)),
        compiler_params=_params(("parallel",)),
        name="final_norm",
    )(x2d, g)


def _rotary_tables(seq):
    half = HEAD_DIM // 2
    inv = ROPE_BASE ** (-jnp.arange(half, dtype=F32) / half)
    ang = jnp.arange(seq, dtype=F32)[:, None] * inv[None, :]
    cos, sin = jnp.cos(ang), jnp.sin(ang)
    return jnp.concatenate([cos, cos], axis=-1), jnp.concatenate([-sin, sin], axis=-1)


def _trunk(x, w):
    batch, seq, _ = x.shape
    t = batch * seq
    cap = EC_CAPACITY_FACTOR * t // N_EXPERTS
    x2d = x.reshape(t, D_MODEL)
    cos_t, sin_t = _rotary_tables(seq)

    proj = _in_proj(x2d, w["norm_mix_g"], w["w_in"], cos_t, sin_t, seq)
    o_f, o_b = _retention(proj, w["decays"], batch, seq)
    acc, tok, aff = _mix_out(o_f, o_b, proj, x2d, w["ret_gn_g"], w["conv_w"], w["w_o"],
                             w["norm_ffn_g"], w["w_router_t"], seq)
    idx = _select(aff.reshape(N_EXPERTS, t // LANES, LANES), cap)
    acc = _experts(idx.reshape(N_EXPERTS * cap), tok, acc, w["w_gate"], w["w_up"], w["w_down"], cap)
    return _final_norm(acc, w["final_norm_g"]).reshape(batch, seq, D_MODEL)


def kernel(x_prompt, x_sample, norm_mix_g, w_in, conv_w, ret_decay_fwd, ret_decay_bwd, ret_gn_g, w_o,
           norm_ffn_g, w_router, w_gate, w_up, w_down, final_norm_g):
    w = {
        "norm_mix_g": norm_mix_g[0][None, :],
        "w_in": w_in[0].astype(BF16),
        "conv_w": conv_w[0],
        "decays": jnp.stack([ret_decay_fwd[0], ret_decay_bwd[0]]),
        "ret_gn_g": ret_gn_g[0][None, :],
        "w_o": w_o[0].astype(BF16),
        "norm_ffn_g": norm_ffn_g[0][None, :],
        "w_router_t": w_router[0].T,
        "w_gate": w_gate[0].astype(BF16),
        "w_up": w_up[0].astype(BF16),
        "w_down": w_down[0].astype(BF16),
        "final_norm_g": final_norm_g[None, :],
    }
    return (_trunk(x_prompt, w), _trunk(x_sample, w))
```

```python
import functools

import jax
import jax.numpy as jnp
from jax import lax
from jax.experimental import pallas as pl
from jax.experimental.pallas import tpu as pltpu

F32 = jnp.float32
BF16 = jnp.bfloat16
I32 = jnp.int32

D_MODEL = 2048
RET_WIDTH = 1024
RET_HEADS = 8
HEAD_DIM = 128
IN_COLS = 7168
COL_BLOCK = 1024
N_EXPERTS = 16
EC_CAPACITY_FACTOR = 2
EXPERT_FF = 2048
CHUNK = 128
ROPE_BASE = 10000.0
EPS = 1e-6
LANES = 128
HALO_ROWS = 16
ROW_TILES = D_MODEL // LANES
ROW_PITCH = ROW_TILES + 1
RET_CHUNKS = 2
MIX_SPLIT = 2
EXPERT_TILE = 1024
EXPERT_ROWS = 256
VMEM_LIMIT = 56 * 1024 * 1024


def _params(sem, vmem=VMEM_LIMIT):
    return pltpu.CompilerParams(dimension_semantics=sem, vmem_limit_bytes=vmem)


def _rmsnorm(x, g):
    return x * lax.rsqrt(jnp.mean(x * x, axis=-1, keepdims=True) + EPS) * g


def _split_bf16(x):
    hi = x.astype(BF16)
    lo = (x - hi.astype(F32)).astype(BF16)
    return hi, lo


def _dot3(a, b, dims):
    a_hi, a_lo = _split_bf16(a)
    b_hi, b_lo = _split_bf16(b)
    d = functools.partial(lax.dot_general, dimension_numbers=dims, preferred_element_type=F32)
    return d(a_hi, b_hi) + d(a_hi, b_lo) + d(a_lo, b_hi)


def _inproj_kernel(x_ref, g_ref, w_ref, cos_ref, sin_ref, o_ref, h_ref, acc_ref):
    j = pl.program_id(1)

    @pl.when(j == 0)
    def _():
        h_ref[...] = _rmsnorm(x_ref[...], g_ref[...]).astype(BF16)

    acc_ref[...] = jnp.dot(h_ref[...], w_ref[...], preferred_element_type=F32)

    @pl.when(j < 2)
    def _():
        scale = jnp.where(j == 1, HEAD_DIM ** -0.5, 1.0).astype(F32)
        cos = cos_ref[...]
        sin = sin_ref[...]
        for h in range(COL_BLOCK // HEAD_DIM):
            sl = slice(h * HEAD_DIM, (h + 1) * HEAD_DIM)
            t = acc_ref[:, sl]
            o_ref[:, sl] = ((t * cos + pltpu.roll(t, HEAD_DIM // 2, axis=1) * sin) * scale).astype(BF16)

    @pl.when(j >= 2)
    def _():
        o_ref[...] = acc_ref[...].astype(BF16)


def _in_proj(x2d, g, w_bf16, cos_t, sin_t, seq):
    t = x2d.shape[0]
    tm = min(1024, seq)
    blocks_per_seq = seq // tm
    return pl.pallas_call(
        _inproj_kernel,
        out_shape=jax.ShapeDtypeStruct((t, IN_COLS), BF16),
        grid=(t // tm, IN_COLS // COL_BLOCK),
        in_specs=[
            pl.BlockSpec((tm, D_MODEL), lambda i, j: (i, 0)),
            pl.BlockSpec((1, D_MODEL), lambda i, j: (0, 0)),
            pl.BlockSpec((D_MODEL, COL_BLOCK), lambda i, j: (0, j)),
            pl.BlockSpec((tm, HEAD_DIM), lambda i, j: (i % blocks_per_seq, 0)),
            pl.BlockSpec((tm, HEAD_DIM), lambda i, j: (i % blocks_per_seq, 0)),
        ],
        out_specs=pl.BlockSpec((tm, COL_BLOCK), lambda i, j: (i, j)),
        scratch_shapes=[pltpu.VMEM((tm, D_MODEL), BF16), pltpu.VMEM((tm, COL_BLOCK), F32)],
        compiler_params=_params(("parallel", "arbitrary")),
        name="in_proj",
    )(x2d, g, w_bf16, cos_t, sin_t)


def _retention_kernel(dec_ref, qf_ref, kf_ref, vf_ref, qb_ref, kb_ref, vb_ref,
                      of_ref, ob_ref,
                      sf_ref, sb_ref, dtab_ref, xif_ref, xib_ref, zf_ref, zb_ref):
    first = jnp.logical_and(pl.program_id(0) == 0, pl.program_id(1) == 0)

    @pl.when(first)
    def _():
        row = lax.broadcasted_iota(I32, (CHUNK, CHUNK), 0).astype(F32)
        col = lax.broadcasted_iota(I32, (CHUNK, CHUNK), 1).astype(F32)
        diff = row - col
        for h in range(RET_HEADS):
            lgf = -jnp.exp(dec_ref[0:1, h:h + 1])
            lgb = -jnp.exp(dec_ref[1:2, h:h + 1])
            dtab_ref[h] = jnp.where(diff >= 0, jnp.exp(lgf * jnp.maximum(diff, 0.0)),
                                    jnp.exp(lgb * jnp.maximum(-diff, 0.0)))
            xif_ref[h] = jnp.exp(lgf * (row + 1.0))
            zf_ref[h] = jnp.exp(lgf * (CHUNK - 1.0 - row))
            xib_ref[h] = jnp.exp(lgb * (CHUNK - row))
            zb_ref[h] = jnp.exp(lgb * row)

    @pl.when(pl.program_id(1) == 0)
    def _():
        sf_ref[...] = jnp.zeros_like(sf_ref)
        sb_ref[...] = jnp.zeros_like(sb_ref)

    nt = (((1,), (1,)), ((), ()))
    tn = (((0,), (0,)), ((), ()))
    for h in range(RET_HEADS):
        sl = slice(h * HEAD_DIM, (h + 1) * HEAD_DIM)
        gcf = jnp.exp(-jnp.exp(dec_ref[0:1, h:h + 1]) * CHUNK)
        gcb = jnp.exp(-jnp.exp(dec_ref[1:2, h:h + 1]) * CHUNK)

        for u in range(RET_CHUNKS):
            rows = slice(u * CHUNK, (u + 1) * CHUNK)
            q = qf_ref[rows, sl]
            k = kf_ref[rows, sl]
            v = vf_ref[rows, sl]
            s = lax.dot_general(q, k, nt, preferred_element_type=F32) * dtab_ref[h]
            sf = sf_ref[h]
            o = jnp.dot(s.astype(BF16), v, preferred_element_type=F32)
            o = o + jnp.dot(q, sf.astype(BF16), preferred_element_type=F32) * xif_ref[h]
            of_ref[rows, sl] = o
            kz = (k.astype(F32) * zf_ref[h]).astype(BF16)
            sf_ref[h] = sf * gcf + lax.dot_general(kz, v, tn, preferred_element_type=F32)

        for u in reversed(range(RET_CHUNKS)):
            rows = slice(u * CHUNK, (u + 1) * CHUNK)
            q = qb_ref[rows, sl]
            v = vb_ref[rows, sl]
            sb = sb_ref[h]
            ob_ref[rows, sl] = jnp.dot(q, sb.astype(BF16), preferred_element_type=F32) * xib_ref[h]
            kz = (kb_ref[rows, sl].astype(F32) * zb_ref[h]).astype(BF16)
            sb_ref[h] = sb * gcb + lax.dot_general(kz, v, tn, preferred_element_type=F32)


def _retention(proj, decays, batch, seq):
    t = proj.shape[0]
    rows = RET_CHUNKS * CHUNK
    n = seq // rows
    fwd = lambda col: pl.BlockSpec((rows, COL_BLOCK), lambda b, c: (b * n + c, col))
    bwd = lambda col: pl.BlockSpec((rows, COL_BLOCK), lambda b, c: (b * n + n - 1 - c, col))
    table = pltpu.VMEM((RET_HEADS, CHUNK, CHUNK), F32)
    return pl.pallas_call(
        _retention_kernel,
        out_shape=(jax.ShapeDtypeStruct((t, RET_WIDTH), F32), jax.ShapeDtypeStruct((t, RET_WIDTH), F32)),
        grid=(batch, n),
        in_specs=[pl.BlockSpec((2, RET_HEADS), lambda b, c: (0, 0)),
                  fwd(0), fwd(1), fwd(2), bwd(0), bwd(1), bwd(2)],
        out_specs=(pl.BlockSpec((rows, RET_WIDTH), lambda b, c: (b * n + c, 0)),
                   pl.BlockSpec((rows, RET_WIDTH), lambda b, c: (b * n + n - 1 - c, 0))),
        scratch_shapes=[table] * 7,
        compiler_params=_params(("arbitrary", "arbitrary")),
        name="retention",
    )(decays, proj, proj, proj, proj, proj, proj)


def _mixout_kernel(of_ref, ob_ref, g_ref, cb_ref, cc_ref, ch_ref, ccp_ref, chp_ref, ccn_ref, chn_ref,
                   x_ref, gn_ref, cw_ref, wo_ref, g2_ref, wr_ref,
                   acc_ref, tok_ref, aff_ref, *, blocks_per_seq):
    i = pl.program_id(0)
    tm = x_ref.shape[0]

    gn = gn_ref[...]
    z = cc_ref[...].astype(F32) * ch_ref[...].astype(F32)
    pos = i % blocks_per_seq
    keep_prev = jnp.where(pos == 0, 0.0, 1.0).astype(F32)
    keep_next = jnp.where(pos == blocks_per_seq - 1, 0.0, 1.0).astype(F32)
    last = HALO_ROWS - 1
    halo_prev = (ccp_ref[...].astype(F32) * chp_ref[...].astype(F32))[last:, :] * keep_prev
    halo_next = (ccn_ref[...].astype(F32) * chn_ref[...].astype(F32))[0:1, :] * keep_next
    row = lax.broadcasted_iota(I32, z.shape, 0)
    z_prev = jnp.where(row == 0, halo_prev, pltpu.roll(z, 1, axis=0))
    z_next = jnp.where(row == tm - 1, halo_next, pltpu.roll(z, tm - 1, axis=0))
    cw = cw_ref[...]
    zc = z_prev * cw[0:1, :] + z * cw[1:2, :] + z_next * cw[2:3, :]
    eye = (lax.broadcasted_iota(I32, (N_EXPERTS, LANES), 0)
           == lax.broadcasted_iota(I32, (N_EXPERTS, LANES), 1)).astype(BF16)
    tn = (((0,), (0,)), ((), ()))
    tr = lambda a: lax.dot_general(a, eye, tn, preferred_element_type=F32)

    rs = tm // MIX_SPLIT
    for s in range(MIX_SPLIT):
        rows = slice(s * rs, (s + 1) * rs)
        o = of_ref[rows, :] + ob_ref[rows, :]
        gate_in = g_ref[rows, :].astype(F32)
        parts = []
        for h in range(RET_HEADS):
            sl = slice(h * HEAD_DIM, (h + 1) * HEAD_DIM)
            oh = o[:, sl]
            mu = jnp.mean(oh, axis=-1, keepdims=True)
            d = oh - mu
            var = jnp.mean(d * d, axis=-1, keepdims=True)
            gh = gate_in[:, sl]
            swish = gh / (1.0 + jnp.exp(-gh))
            parts.append((swish * (d * lax.rsqrt(var + EPS) * gn[:, sl])).astype(BF16))
        ret = jnp.concatenate(parts, axis=-1)
        conv = cb_ref[rows, :].astype(F32) * zc[rows, :]

        y = jnp.dot(ret, wo_ref[0:RET_WIDTH, :], preferred_element_type=F32)
        y = y + jnp.dot(conv.astype(BF16), wo_ref[RET_WIDTH:, :], preferred_element_type=F32)
        x1 = x_ref[rows, :] + y
        h2 = _rmsnorm(x1, g2_ref[...])
        for k in range(ROW_TILES):
            cols = slice(k * LANES, (k + 1) * LANES)
            dst = pl.ds(s * rs * ROW_PITCH + k, rs, stride=ROW_PITCH)
            acc_ref[dst, :] = x1[:, cols]
            tok_ref[dst, :] = h2[:, cols]
        last_tile = pl.ds(s * rs * ROW_PITCH + ROW_TILES, rs, stride=ROW_PITCH)
        acc_ref[last_tile, :] = jnp.zeros((rs, LANES), F32)
        logits = _dot3(wr_ref[...], h2, (((1,), (1,)), ((), ())))
        ex = jnp.exp(logits - jnp.max(logits, axis=0, keepdims=True))
        aff = ex / jnp.sum(ex, axis=0, keepdims=True)
        aff_ref[:, rows] = aff

        a1 = aff.astype(BF16)
        r1 = aff - a1.astype(F32)
        a2 = r1.astype(BF16)
        a3 = (r1 - a2.astype(F32)).astype(BF16)
        tok_ref[last_tile, :] = (tr(a1) + tr(a2)) + tr(a3)


def _mix_out(o_f, o_b, proj, x2d, gn_g, conv_w, wo_bf16, g2, wr_t, seq):
    t = x2d.shape[0]
    tm = min(256, seq)
    bps = seq // tm
    r8 = tm // HALO_ROWS
    n8 = t // HALO_ROWS
    col = lambda c: pl.BlockSpec((tm, COL_BLOCK), lambda i: (i, c))
    prev = lambda c: pl.BlockSpec((HALO_ROWS, COL_BLOCK), lambda i: (jnp.maximum(i * r8 - 1, 0), c))
    nxt = lambda c: pl.BlockSpec((HALO_ROWS, COL_BLOCK), lambda i: (jnp.minimum((i + 1) * r8, n8 - 1), c))
    whole = lambda shape: pl.BlockSpec(shape, lambda i: (0,) * len(shape))
    return pl.pallas_call(
        functools.partial(_mixout_kernel, blocks_per_seq=bps),
        out_shape=(jax.ShapeDtypeStruct((t * ROW_PITCH, LANES), F32),
                   jax.ShapeDtypeStruct((t * ROW_PITCH, LANES), F32),
                   jax.ShapeDtypeStruct((N_EXPERTS, t), F32)),
        grid=(t // tm,),
        in_specs=[
            pl.BlockSpec((tm, RET_WIDTH), lambda i: (i, 0)),
            pl.BlockSpec((tm, RET_WIDTH), lambda i: (i, 0)),
            col(3), col(4), col(5), col(6), prev(5), prev(6), nxt(5), nxt(6),
            pl.BlockSpec((tm, D_MODEL), lambda i: (i, 0)),
            whole((1, RET_WIDTH)), whole((3, COL_BLOCK)), whole((D_MODEL, D_MODEL)),
            whole((1, D_MODEL)), whole((N_EXPERTS, D_MODEL)),
        ],
        out_specs=(pl.BlockSpec((tm * ROW_PITCH, LANES), lambda i: (i, 0)),
                   pl.BlockSpec((tm * ROW_PITCH, LANES), lambda i: (i, 0)),
                   pl.BlockSpec((N_EXPERTS, tm), lambda i: (0, i))),
        compiler_params=_params(("parallel",)),
        name="mix_out",
    )(o_f, o_b, proj, proj, proj, proj, proj, proj, proj, proj, x2d, gn_g, conv_w, wo_bf16, g2, wr_t)


def _select_kernel(aff_ref, idx_ref, sel_ref, *, cap):
    e_n, rows, _ = aff_ref.shape
    bits = lax.bitcast_convert_type(aff_ref[...], I32)

    def count(mask):
        c = jnp.sum(jnp.where(mask, 1.0, 0.0), axis=1, keepdims=True)
        return jnp.sum(c, axis=2, keepdims=True)

    thr = jnp.zeros((e_n, 1, 1), I32)
    for b in range(30, -1, -1):
        cand = thr | (1 << b)
        thr = jnp.where(count(bits >= cand) >= cap, cand, thr)

    gt = bits > thr
    eq = bits == thr
    need = cap - count(gt)

    li = lax.broadcasted_iota(I32, (LANES, LANES), 0)
    lj = lax.broadcasted_iota(I32, (LANES, LANES), 1)
    upper = (li <= lj).astype(BF16)
    ri = lax.broadcasted_iota(I32, (rows, rows), 0)
    rj = lax.broadcasted_iota(I32, (rows, rows), 1)
    lower = (rj < ri).astype(BF16)
    eqf = jnp.where(eq, 1.0, 0.0)
    for e in range(e_n):
        incl = jnp.dot(eqf[e].astype(BF16), upper, preferred_element_type=F32)
        row_tot = jnp.broadcast_to(incl[:, LANES - 1:LANES], (rows, LANES)).astype(BF16)
        row_off = jnp.dot(lower, row_tot, preferred_element_type=F32)
        rank = row_off + incl - eqf[e]
        take = jnp.logical_or(gt[e], jnp.logical_and(eq[e], rank < need[e]))
        sel_ref[e] = jnp.where(take, 1.0, 0.0)

    chunk = min(512, cap)
    lower_incl = (li >= lj).astype(BF16)
    slot0 = lax.broadcasted_iota(I32, (1, chunk), 1).astype(F32)
    rowid = lax.broadcasted_iota(I32, (rows, chunk), 0).astype(F32)

    def compact(e, carry):
        s = sel_ref[e].astype(BF16)
        q_t = lax.dot_general(lower_incl, s, (((1,), (1,)), ((), ())),
                              preferred_element_type=F32)
        q = jnp.dot(s, upper, preferred_element_type=F32)
        row_tot = q[:, LANES - 1:LANES]
        row_cum = jnp.dot(lower, jnp.broadcast_to(row_tot, (rows, LANES)).astype(BF16),
                          preferred_element_type=F32)[:, 0:1] + row_tot
        q_t = q_t.astype(BF16)
        for ch in range(cap // chunk):
            c = slot0 + float(ch * chunk)
            before = row_cum <= c
            r_c = jnp.sum(jnp.where(before, 1.0, 0.0), axis=0, keepdims=True)
            k = c - jnp.sum(jnp.where(before, row_tot, 0.0), axis=0, keepdims=True)
            onehot = jnp.where(rowid == r_c, 1.0, 0.0).astype(BF16)
            q_row = jnp.dot(q_t, onehot, preferred_element_type=F32)
            lane_c = jnp.sum(jnp.where(q_row <= k, 1.0, 0.0), axis=0, keepdims=True)
            token = r_c * LANES + lane_c
            idx_ref[pl.ds(e, 1), ch * chunk:(ch + 1) * chunk] = (token * ROW_PITCH).astype(I32)
        return carry

    lax.fori_loop(0, e_n, compact, 0)


def _select(aff3, cap):
    e_n, rows, lanes = aff3.shape
    return pl.pallas_call(
        functools.partial(_select_kernel, cap=cap),
        out_shape=jax.ShapeDtypeStruct((e_n, cap), I32),
        grid=(1,),
        in_specs=[pl.BlockSpec((e_n, rows, lanes), lambda i: (0, 0, 0))],
        out_specs=pl.BlockSpec((e_n, cap), lambda i: (0, 0)),
        scratch_shapes=[pltpu.VMEM((e_n, rows, lanes), F32)],
        compiler_params=_params(("arbitrary",)),
        name="select",
    )(aff3)


def _experts_kernel(idx_ref, idx_next_ref, tok_hbm, acc_in_hbm, wg_ref, wu_ref, wd_ref,
                    acc_hbm, xbuf, xn_ref, yacc_ref, gate_ref, sem_x, sem_a, sem_s):
    del acc_in_hbm
    e = pl.program_id(0)
    c = pl.program_id(1)
    f = pl.program_id(2)
    ne = pl.num_programs(0)
    nt = pl.num_programs(1)
    nf = pl.num_programs(2)
    tc = xn_ref.shape[0]
    g = e * nt + c
    slot = g % 2
    other = 1 - slot

    def start_gather(src_hbm, tiles, ids_ref, base, buf, sem):
        def issue(r, carry):
            pltpu.make_async_copy(src_hbm.at[pl.ds(ids_ref[base + r], tiles), :],
                                  xbuf.at[buf, pl.ds(r * ROW_PITCH, tiles), :], sem.at[buf]).start()
            return carry
        lax.fori_loop(0, tc, issue, 0, unroll=8)

    def wait_gather(src_hbm, tiles, buf, sem):
        n = tc * tiles
        pltpu.make_async_copy(src_hbm.at[pl.ds(0, n), :], xbuf.at[buf, pl.ds(0, n), :], sem.at[buf]).wait()

    def wait_scatter(buf):
        n = tc * ROW_TILES
        pltpu.make_async_copy(xbuf.at[buf, pl.ds(0, n), :], acc_hbm.at[pl.ds(0, n), :], sem_s.at[buf]).wait()

    def tile_rows(r0, n, k):
        return pl.ds(r0 * ROW_PITCH + k, n, stride=ROW_PITCH)

    rb = min(EXPERT_ROWS, tc)

    @pl.when(jnp.logical_and(g == 0, f == 0))
    def _():
        start_gather(tok_hbm, ROW_PITCH, idx_ref, 0, 0, sem_x)

    @pl.when(f == 0)
    def _():
        wait_gather(tok_hbm, ROW_PITCH, slot, sem_x)

        for r0 in range(0, tc, rb):
            for k in range(ROW_TILES):
                xn_ref[r0:r0 + rb, k * LANES:(k + 1) * LANES] = xbuf[slot, tile_rows(r0, rb, k), :].astype(BF16)
            aff = xbuf[slot, tile_rows(r0, rb, ROW_TILES), :]
            lane = lax.broadcasted_iota(I32, aff.shape, 1)
            gate_ref[r0:r0 + rb, :] = jnp.sum(jnp.where(lane == e, aff, 0.0), axis=-1, keepdims=True)
            yacc_ref[r0:r0 + rb, :] = jnp.zeros((rb, D_MODEL), F32)

        @pl.when(g > 0)
        def _():
            wait_scatter(other)

        start_gather(acc_hbm, ROW_TILES, idx_ref, c * tc, slot, sem_a)

        @pl.when(c + 1 < nt)
        def _():
            start_gather(tok_hbm, ROW_PITCH, idx_ref, (c + 1) * tc, other, sem_x)

        @pl.when(jnp.logical_and(c + 1 == nt, e + 1 < ne))
        def _():
            start_gather(tok_hbm, ROW_PITCH, idx_next_ref, 0, other, sem_x)

    def ffn(m, carry):
        rows = pl.ds(pl.multiple_of(m * rb, rb), rb)
        xn = xn_ref[rows, :]
        hg = jnp.dot(xn, wg_ref[0], preferred_element_type=F32)
        hu = jnp.dot(xn, wu_ref[0], preferred_element_type=F32)
        hid = (hg / (1.0 + jnp.exp(-hg)) * hu).astype(BF16)
        yacc_ref[rows, :] += jnp.dot(hid, wd_ref[0], preferred_element_type=F32)
        return carry
    lax.fori_loop(0, tc // rb, ffn, 0)

    @pl.when(f == nf - 1)
    def _():
        wait_gather(acc_hbm, ROW_TILES, slot, sem_a)

        for r0 in range(0, tc, rb):
            gate = gate_ref[r0:r0 + rb, :]
            for k in range(ROW_TILES):
                dst = tile_rows(r0, rb, k)
                xbuf[slot, dst, :] = xbuf[slot, dst, :] + yacc_ref[r0:r0 + rb, k * LANES:(k + 1) * LANES] * gate

        def issue(r, carry):
            pltpu.make_async_copy(xbuf.at[slot, pl.ds(r * ROW_PITCH, ROW_TILES), :],
                                  acc_hbm.at[pl.ds(idx_ref[c * tc + r], ROW_TILES), :], sem_s.at[slot]).start()
            return carry
        lax.fori_loop(0, tc, issue, 0, unroll=8)

        @pl.when(jnp.logical_and(e == ne - 1, c == nt - 1))
        def _():
            wait_scatter(slot)


def _experts(idx_flat, tok, acc, wg, wu, wd, cap):
    tc = min(EXPERT_TILE, cap)
    tf = 512
    return pl.pallas_call(
        _experts_kernel,
        out_shape=jax.ShapeDtypeStruct(acc.shape, F32),
        grid=(N_EXPERTS, cap // tc, EXPERT_FF // tf),
        in_specs=[
            pl.BlockSpec((cap,), lambda e, c, f: (e,), memory_space=pltpu.SMEM),
            pl.BlockSpec((cap,), lambda e, c, f: (jnp.minimum(e + 1, N_EXPERTS - 1),), memory_space=pltpu.SMEM),
            pl.BlockSpec(memory_space=pl.ANY),
            pl.BlockSpec(memory_space=pl.ANY),
            pl.BlockSpec((1, D_MODEL, tf), lambda e, c, f: (e, 0, f)),
            pl.BlockSpec((1, D_MODEL, tf), lambda e, c, f: (e, 0, f)),
            pl.BlockSpec((1, tf, D_MODEL), lambda e, c, f: (e, f, 0)),
        ],
        out_specs=pl.BlockSpec(memory_space=pl.ANY),
        scratch_shapes=[
            pltpu.VMEM((2, tc * ROW_PITCH, LANES), F32),
            pltpu.VMEM((tc, D_MODEL), BF16),
            pltpu.VMEM((tc, D_MODEL), F32),
            pltpu.VMEM((tc, 1), F32),
            pltpu.SemaphoreType.DMA((2,)),
            pltpu.SemaphoreType.DMA((2,)),
            pltpu.SemaphoreType.DMA((2,)),
        ],
        input_output_aliases={3: 0},
        compiler_params=_params(("arbitrary", "arbitrary", "arbitrary")),
        name="experts",
    )(idx_flat, idx_flat, tok, acc, wg, wu, wd)


def _final_kernel(x_ref, g_ref, o_ref):
    tm = o_ref.shape[0]
    x = jnp.concatenate([x_ref[pl.ds(k, tm, stride=ROW_PITCH), :] for k in range(ROW_TILES)], axis=-1)
    o_ref[...] = _rmsnorm(x, g_ref[...])


def _final_norm(rows, g):
    t = rows.shape[0] // ROW_PITCH
    tm = min(512, t)
    return pl.pallas_call(
        _final_kernel,
        out_shape=jax.ShapeDtypeStruct((t, D_MODEL), F32),
        grid=(t // tm,),
        in_specs=[pl.BlockSpec((tm * ROW_PITCH, LANES), lambda i: (i, 0)),
                  pl.BlockSpec((1, D_MODEL), lambda i: (0, 0))],
        out_specs=pl.BlockSpec((tm, D_MODEL), lambda i: (i, 0)),
        compiler_params=_params(("parallel",)),
        name="final_norm",
    )(rows, g)


def _rotary_tables(seq):
    half = HEAD_DIM // 2
    inv = ROPE_BASE ** (-jnp.arange(half, dtype=F32) / half)
    ang = jnp.arange(seq, dtype=F32)[:, None] * inv[None, :]
    cos, sin = jnp.cos(ang), jnp.sin(ang)
    return jnp.concatenate([cos, cos], axis=-1), jnp.concatenate([-sin, sin], axis=-1)


def _trunk(x, w):
    batch, seq, _ = x.shape
    t = batch * seq
    cap = EC_CAPACITY_FACTOR * t // N_EXPERTS
    x2d = x.reshape(t, D_MODEL)
    cos_t, sin_t = _rotary_tables(seq)

    proj = _in_proj(x2d, w["norm_mix_g"], w["w_in"], cos_t, sin_t, seq)
    o_f, o_b = _retention(proj, w["decays"], batch, seq)
    acc, tok, aff = _mix_out(o_f, o_b, proj, x2d, w["ret_gn_g"], w["conv_w"], w["w_o"],
                             w["norm_ffn_g"], w["w_router_t"], seq)
    idx = _select(aff.reshape(N_EXPERTS, t // LANES, LANES), cap)
    acc = _experts(idx.reshape(N_EXPERTS * cap), tok, acc, w["w_gate"], w["w_up"], w["w_down"], cap)
    return _final_norm(acc, w["final_norm_g"]).reshape(batch, seq, D_MODEL)


def kernel(x_prompt, x_sample, norm_mix_g, w_in, conv_w, ret_decay_fwd, ret_decay_bwd, ret_gn_g, w_o,
           norm_ffn_g, w_router, w_gate, w_up, w_down, final_norm_g):
    w = {
        "norm_mix_g": norm_mix_g[0][None, :],
        "w_in": w_in[0].astype(BF16),
        "conv_w": conv_w[0],
        "decays": jnp.stack([ret_decay_fwd[0], ret_decay_bwd[0]]),
        "ret_gn_g": ret_gn_g[0][None, :],
        "w_o": w_o[0].astype(BF16),
        "norm_ffn_g": norm_ffn_g[0][None, :],
        "w_router_t": w_router[0].T,
        "w_gate": w_gate[0].astype(BF16),
        "w_up": w_up[0].astype(BF16),
        "w_down": w_down[0].astype(BF16),
        "final_norm_g": final_norm_g[None, :],
    }
    return (_trunk(x_prompt, w), _trunk(x_sample, w))
```

```python
import functools

import jax
import jax.numpy as jnp
from jax import lax
from jax.experimental import pallas as pl
from jax.experimental.pallas import tpu as pltpu

F32 = jnp.float32
BF16 = jnp.bfloat16
I32 = jnp.int32

D_MODEL = 2048
RET_WIDTH = 1024
RET_HEADS = 8
HEAD_DIM = 128
IN_COLS = 7168
COL_BLOCK = 1024
N_EXPERTS = 16
EC_CAPACITY_FACTOR = 2
EXPERT_FF = 2048
CHUNK = 128
ROPE_BASE = 10000.0
EPS = 1e-6
LANES = 128
HALO_ROWS = 16
ROW_TILES = D_MODEL // LANES
ROW_PITCH = ROW_TILES + 1
RET_CHUNKS = 2
MIX_SPLIT = 2
EXPERT_TILE = 1024
EXPERT_ROWS = 256
VMEM_LIMIT = 56 * 1024 * 1024


def _params(sem, vmem=VMEM_LIMIT):
    return pltpu.CompilerParams(dimension_semantics=sem, vmem_limit_bytes=vmem)


def _rmsnorm(x, g):
    return x * lax.rsqrt(jnp.mean(x * x, axis=-1, keepdims=True) + EPS) * g


def _split_bf16(x):
    hi = x.astype(BF16)
    lo = (x - hi.astype(F32)).astype(BF16)
    return hi, lo


def _dot3(a, b, dims):
    a_hi, a_lo = _split_bf16(a)
    b_hi, b_lo = _split_bf16(b)
    d = functools.partial(lax.dot_general, dimension_numbers=dims, preferred_element_type=F32)
    return d(a_hi, b_hi) + d(a_hi, b_lo) + d(a_lo, b_hi)


def _inproj_kernel(x_ref, g_ref, w_ref, cos_ref, sin_ref, o_ref, h_ref, acc_ref):
    j = pl.program_id(1)

    @pl.when(j == 0)
    def _():
        h_ref[...] = _rmsnorm(x_ref[...], g_ref[...]).astype(BF16)

    @pl.when(j < 2)
    def _():
        acc_ref[...] = jnp.dot(h_ref[...], w_ref[...], preferred_element_type=F32)
        scale = jnp.where(j == 1, HEAD_DIM ** -0.5, 1.0).astype(F32)
        cos = cos_ref[...]
        sin = sin_ref[...]
        for h in range(COL_BLOCK // HEAD_DIM):
            sl = slice(h * HEAD_DIM, (h + 1) * HEAD_DIM)
            t = acc_ref[:, sl]
            o_ref[:, sl] = ((t * cos + pltpu.roll(t, HEAD_DIM // 2, axis=1) * sin) * scale).astype(BF16)

    @pl.when(j >= 2)
    def _():
        o_ref[...] = jnp.dot(h_ref[...], w_ref[...], preferred_element_type=F32).astype(BF16)


def _in_proj(x2d, g, w_bf16, cos_t, sin_t, seq):
    t = x2d.shape[0]
    tm = min(1024, seq)
    blocks_per_seq = seq // tm
    return pl.pallas_call(
        _inproj_kernel,
        out_shape=jax.ShapeDtypeStruct((t, IN_COLS), BF16),
        grid=(t // tm, IN_COLS // COL_BLOCK),
        in_specs=[
            pl.BlockSpec((tm, D_MODEL), lambda i, j: (i, 0)),
            pl.BlockSpec((1, D_MODEL), lambda i, j: (0, 0)),
            pl.BlockSpec((D_MODEL, COL_BLOCK), lambda i, j: (0, j)),
            pl.BlockSpec((tm, HEAD_DIM), lambda i, j: (i % blocks_per_seq, 0)),
            pl.BlockSpec((tm, HEAD_DIM), lambda i, j: (i % blocks_per_seq, 0)),
        ],
        out_specs=pl.BlockSpec((tm, COL_BLOCK), lambda i, j: (i, j)),
        scratch_shapes=[pltpu.VMEM((tm, D_MODEL), BF16), pltpu.VMEM((tm, COL_BLOCK), F32)],
        compiler_params=_params(("parallel", "arbitrary")),
        name="in_proj",
    )(x2d, g, w_bf16, cos_t, sin_t)


def _retention_kernel(dec_ref, qf_ref, kf_ref, vf_ref, qb_ref, kb_ref, vb_ref,
                      of_ref, ob_ref,
                      sf_ref, sb_ref, dtab_ref, xif_ref, xib_ref, zf_ref, zb_ref):
    first = jnp.logical_and(pl.program_id(0) == 0, pl.program_id(1) == 0)

    @pl.when(first)
    def _():
        row = lax.broadcasted_iota(I32, (CHUNK, CHUNK), 0).astype(F32)
        col = lax.broadcasted_iota(I32, (CHUNK, CHUNK), 1).astype(F32)
        diff = row - col
        for h in range(RET_HEADS):
            lgf = -jnp.exp(dec_ref[0:1, h:h + 1])
            lgb = -jnp.exp(dec_ref[1:2, h:h + 1])
            dtab_ref[h] = jnp.where(diff >= 0, jnp.exp(lgf * jnp.maximum(diff, 0.0)),
                                    jnp.exp(lgb * jnp.maximum(-diff, 0.0)))
            xif_ref[h] = jnp.exp(lgf * (row + 1.0))
            zf_ref[h] = jnp.exp(lgf * (CHUNK - 1.0 - row))
            xib_ref[h] = jnp.exp(lgb * (CHUNK - row))
            zb_ref[h] = jnp.exp(lgb * row)

    @pl.when(pl.program_id(1) == 0)
    def _():
        sf_ref[...] = jnp.zeros_like(sf_ref)
        sb_ref[...] = jnp.zeros_like(sb_ref)

    nt = (((1,), (1,)), ((), ()))
    tn = (((0,), (0,)), ((), ()))
    for h in range(RET_HEADS):
        sl = slice(h * HEAD_DIM, (h + 1) * HEAD_DIM)
        gcf = jnp.exp(-jnp.exp(dec_ref[0:1, h:h + 1]) * CHUNK)
        gcb = jnp.exp(-jnp.exp(dec_ref[1:2, h:h + 1]) * CHUNK)

        for u in range(RET_CHUNKS):
            rows = slice(u * CHUNK, (u + 1) * CHUNK)
            q = qf_ref[rows, sl]
            k = kf_ref[rows, sl]
            v = vf_ref[rows, sl]
            s = lax.dot_general(q, k, nt, preferred_element_type=F32) * dtab_ref[h]
            sf = sf_ref[h]
            o = jnp.dot(s.astype(BF16), v, preferred_element_type=F32)
            o = o + jnp.dot(q, sf.astype(BF16), preferred_element_type=F32) * xif_ref[h]
            of_ref[rows, sl] = o
            kz = (k.astype(F32) * zf_ref[h]).astype(BF16)
            sf_ref[h] = sf * gcf + lax.dot_general(kz, v, tn, preferred_element_type=F32)

        for u in reversed(range(RET_CHUNKS)):
            rows = slice(u * CHUNK, (u + 1) * CHUNK)
            q = qb_ref[rows, sl]
            v = vb_ref[rows, sl]
            sb = sb_ref[h]
            ob_ref[rows, sl] = jnp.dot(q, sb.astype(BF16), preferred_element_type=F32) * xib_ref[h]
            kz = (kb_ref[rows, sl].astype(F32) * zb_ref[h]).astype(BF16)
            sb_ref[h] = sb * gcb + lax.dot_general(kz, v, tn, preferred_element_type=F32)


def _retention(proj, decays, batch, seq):
    t = proj.shape[0]
    rows = RET_CHUNKS * CHUNK
    n = seq // rows
    fwd = lambda col: pl.BlockSpec((rows, COL_BLOCK), lambda b, c: (b * n + c, col))
    bwd = lambda col: pl.BlockSpec((rows, COL_BLOCK), lambda b, c: (b * n + n - 1 - c, col))
    table = pltpu.VMEM((RET_HEADS, CHUNK, CHUNK), F32)
    return pl.pallas_call(
        _retention_kernel,
        out_shape=(jax.ShapeDtypeStruct((t, RET_WIDTH), F32), jax.ShapeDtypeStruct((t, RET_WIDTH), F32)),
        grid=(batch, n),
        in_specs=[pl.BlockSpec((2, RET_HEADS), lambda b, c: (0, 0)),
                  fwd(0), fwd(1), fwd(2), bwd(0), bwd(1), bwd(2)],
        out_specs=(pl.BlockSpec((rows, RET_WIDTH), lambda b, c: (b * n + c, 0)),
                   pl.BlockSpec((rows, RET_WIDTH), lambda b, c: (b * n + n - 1 - c, 0))),
        scratch_shapes=[table] * 7,
        compiler_params=_params(("arbitrary", "arbitrary")),
        name="retention",
    )(decays, proj, proj, proj, proj, proj, proj)


def _mixout_kernel(of_ref, ob_ref, g_ref, cb_ref, cc_ref, ch_ref, ccp_ref, chp_ref, ccn_ref, chn_ref,
                   x_ref, gn_ref, cw_ref, wo_ref, g2_ref, wr_ref,
                   acc_ref, tok_ref, aff_ref, *, blocks_per_seq):
    i = pl.program_id(0)
    tm = x_ref.shape[0]

    gn = gn_ref[...]
    z = cc_ref[...].astype(F32) * ch_ref[...].astype(F32)
    pos = i % blocks_per_seq
    keep_prev = jnp.where(pos == 0, 0.0, 1.0).astype(F32)
    keep_next = jnp.where(pos == blocks_per_seq - 1, 0.0, 1.0).astype(F32)
    last = HALO_ROWS - 1
    halo_prev = (ccp_ref[...].astype(F32) * chp_ref[...].astype(F32))[last:, :] * keep_prev
    halo_next = (ccn_ref[...].astype(F32) * chn_ref[...].astype(F32))[0:1, :] * keep_next
    row = lax.broadcasted_iota(I32, z.shape, 0)
    z_prev = jnp.where(row == 0, halo_prev, pltpu.roll(z, 1, axis=0))
    z_next = jnp.where(row == tm - 1, halo_next, pltpu.roll(z, tm - 1, axis=0))
    cw = cw_ref[...]
    zc = z_prev * cw[0:1, :] + z * cw[1:2, :] + z_next * cw[2:3, :]
    eye = (lax.broadcasted_iota(I32, (N_EXPERTS, LANES), 0)
           == lax.broadcasted_iota(I32, (N_EXPERTS, LANES), 1)).astype(BF16)
    tn = (((0,), (0,)), ((), ()))
    tr = lambda a: lax.dot_general(a, eye, tn, preferred_element_type=F32)

    rs = tm // MIX_SPLIT
    for s in range(MIX_SPLIT):
        rows = slice(s * rs, (s + 1) * rs)
        o = of_ref[rows, :] + ob_ref[rows, :]
        gate_in = g_ref[rows, :].astype(F32)
        parts = []
        for h in range(RET_HEADS):
            sl = slice(h * HEAD_DIM, (h + 1) * HEAD_DIM)
            oh = o[:, sl]
            mu = jnp.mean(oh, axis=-1, keepdims=True)
            d = oh - mu
            var = jnp.mean(d * d, axis=-1, keepdims=True)
            gh = gate_in[:, sl]
            swish = gh / (1.0 + jnp.exp(-gh))
            parts.append((swish * (d * lax.rsqrt(var + EPS) * gn[:, sl])).astype(BF16))
        ret = jnp.concatenate(parts, axis=-1)
        conv = cb_ref[rows, :].astype(F32) * zc[rows, :]

        y = jnp.dot(ret, wo_ref[0:RET_WIDTH, :], preferred_element_type=F32)
        y = y + jnp.dot(conv.astype(BF16), wo_ref[RET_WIDTH:, :], preferred_element_type=F32)
        x1 = x_ref[rows, :] + y
        h2 = _rmsnorm(x1, g2_ref[...])
        for k in range(ROW_TILES):
            cols = slice(k * LANES, (k + 1) * LANES)
            dst = pl.ds(s * rs * ROW_PITCH + k, rs, stride=ROW_PITCH)
            acc_ref[dst, :] = x1[:, cols]
            tok_ref[dst, :] = h2[:, cols]
        last_tile = pl.ds(s * rs * ROW_PITCH + ROW_TILES, rs, stride=ROW_PITCH)
        acc_ref[last_tile, :] = jnp.zeros((rs, LANES), F32)
        logits = _dot3(wr_ref[...], h2, (((1,), (1,)), ((), ())))
        ex = jnp.exp(logits - jnp.max(logits, axis=0, keepdims=True))
        aff = ex / jnp.sum(ex, axis=0, keepdims=True)
        aff_ref[:, rows] = aff

        a1 = aff.astype(BF16)
        r1 = aff - a1.astype(F32)
        a2 = r1.astype(BF16)
        a3 = (r1 - a2.astype(F32)).astype(BF16)
        tok_ref[last_tile, :] = (tr(a1) + tr(a2)) + tr(a3)


def _mix_out(o_f, o_b, proj, x2d, gn_g, conv_w, wo_bf16, g2, wr_t, seq):
    t = x2d.shape[0]
    tm = min(256, seq)
    bps = seq // tm
    r8 = tm // HALO_ROWS
    n8 = t // HALO_ROWS
    col = lambda c: pl.BlockSpec((tm, COL_BLOCK), lambda i: (i, c))
    prev = lambda c: pl.BlockSpec((HALO_ROWS, COL_BLOCK), lambda i: (jnp.maximum(i * r8 - 1, 0), c))
    nxt = lambda c: pl.BlockSpec((HALO_ROWS, COL_BLOCK), lambda i: (jnp.minimum((i + 1) * r8, n8 - 1), c))
    whole = lambda shape: pl.BlockSpec(shape, lambda i: (0,) * len(shape))
    return pl.pallas_call(
        functools.partial(_mixout_kernel, blocks_per_seq=bps),
        out_shape=(jax.ShapeDtypeStruct((t * ROW_PITCH, LANES), F32),
                   jax.ShapeDtypeStruct((t * ROW_PITCH, LANES), F32),
                   jax.ShapeDtypeStruct((N_EXPERTS, t), F32)),
        grid=(t // tm,),
        in_specs=[
            pl.BlockSpec((tm, RET_WIDTH), lambda i: (i, 0)),
            pl.BlockSpec((tm, RET_WIDTH), lambda i: (i, 0)),
            col(3), col(4), col(5), col(6), prev(5), prev(6), nxt(5), nxt(6),
            pl.BlockSpec((tm, D_MODEL), lambda i: (i, 0)),
            whole((1, RET_WIDTH)), whole((3, COL_BLOCK)), whole((D_MODEL, D_MODEL)),
            whole((1, D_MODEL)), whole((N_EXPERTS, D_MODEL)),
        ],
        out_specs=(pl.BlockSpec((tm * ROW_PITCH, LANES), lambda i: (i, 0)),
                   pl.BlockSpec((tm * ROW_PITCH, LANES), lambda i: (i, 0)),
                   pl.BlockSpec((N_EXPERTS, tm), lambda i: (0, i))),
        compiler_params=_params(("parallel",)),
        name="mix_out",
    )(o_f, o_b, proj, proj, proj, proj, proj, proj, proj, proj, x2d, gn_g, conv_w, wo_bf16, g2, wr_t)


def _select_kernel(aff_ref, idx_ref, sel_ref, *, cap):
    e_n, rows, _ = aff_ref.shape
    bits = lax.bitcast_convert_type(aff_ref[...], I32)

    def count(mask):
        c = jnp.sum(jnp.where(mask, 1.0, 0.0), axis=1, keepdims=True)
        return jnp.sum(c, axis=2, keepdims=True)

    thr = jnp.zeros((e_n, 1, 1), I32)
    for b in range(30, -1, -1):
        cand = thr | (1 << b)
        thr = jnp.where(count(bits >= cand) >= cap, cand, thr)

    gt = bits > thr
    eq = bits == thr
    need = cap - count(gt)

    li = lax.broadcasted_iota(I32, (LANES, LANES), 0)
    lj = lax.broadcasted_iota(I32, (LANES, LANES), 1)
    upper = (li <= lj).astype(BF16)
    ri = lax.broadcasted_iota(I32, (rows, rows), 0)
    rj = lax.broadcasted_iota(I32, (rows, rows), 1)
    lower = (rj < ri).astype(BF16)
    eqf = jnp.where(eq, 1.0, 0.0)
    for e in range(e_n):
        incl = jnp.dot(eqf[e].astype(BF16), upper, preferred_element_type=F32)
        row_tot = jnp.broadcast_to(incl[:, LANES - 1:LANES], (rows, LANES)).astype(BF16)
        row_off = jnp.dot(lower, row_tot, preferred_element_type=F32)
        rank = row_off + incl - eqf[e]
        take = jnp.logical_or(gt[e], jnp.logical_and(eq[e], rank < need[e]))
        sel_ref[e] = jnp.where(take, 1.0, 0.0)

    chunk = min(512, cap)
    lower_incl = (li >= lj).astype(BF16)
    slot0 = lax.broadcasted_iota(I32, (1, chunk), 1).astype(F32)
    rowid = lax.broadcasted_iota(I32, (rows, chunk), 0).astype(F32)

    def compact(e, carry):
        s = sel_ref[e].astype(BF16)
        q_t = lax.dot_general(lower_incl, s, (((1,), (1,)), ((), ())),
                              preferred_element_type=F32)
        q = jnp.dot(s, upper, preferred_element_type=F32)
        row_tot = q[:, LANES - 1:LANES]
        row_cum = jnp.dot(lower, jnp.broadcast_to(row_tot, (rows, LANES)).astype(BF16),
                          preferred_element_type=F32)[:, 0:1] + row_tot
        q_t = q_t.astype(BF16)
        for ch in range(cap // chunk):
            c = slot0 + float(ch * chunk)
            before = row_cum <= c
            r_c = jnp.sum(jnp.where(before, 1.0, 0.0), axis=0, keepdims=True)
            k = c - jnp.sum(jnp.where(before, row_tot, 0.0), axis=0, keepdims=True)
            onehot = jnp.where(rowid == r_c, 1.0, 0.0).astype(BF16)
            q_row = jnp.dot(q_t, onehot, preferred_element_type=F32)
            lane_c = jnp.sum(jnp.where(q_row <= k, 1.0, 0.0), axis=0, keepdims=True)
            token = r_c * LANES + lane_c
            idx_ref[pl.ds(e, 1), ch * chunk:(ch + 1) * chunk] = (token * ROW_PITCH).astype(I32)
        return carry

    lax.fori_loop(0, e_n, compact, 0)


def _select(aff3, cap):
    e_n, rows, lanes = aff3.shape
    return pl.pallas_call(
        functools.partial(_select_kernel, cap=cap),
        out_shape=jax.ShapeDtypeStruct((e_n, cap), I32),
        grid=(1,),
        in_specs=[pl.BlockSpec((e_n, rows, lanes), lambda i: (0, 0, 0))],
        out_specs=pl.BlockSpec((e_n, cap), lambda i: (0, 0)),
        scratch_shapes=[pltpu.VMEM((e_n, rows, lanes), F32)],
        compiler_params=_params(("arbitrary",)),
        name="select",
    )(aff3)


def _experts_kernel(idx_ref, idx_next_ref, tok_hbm, acc_in_hbm, wg_in, wu_in, wd_in, acc_hbm, *rest,
                    cast_weights):
    del acc_in_hbm
    if cast_weights:
        wg_ref, wu_ref, wd_ref, xbuf, xn_ref, yacc_ref, gate_ref, sem_x, sem_a, sem_s = rest
        wg_ref[...] = wg_in[...].astype(BF16)
        wu_ref[...] = wu_in[...].astype(BF16)
        wd_ref[...] = wd_in[...].astype(BF16)
    else:
        wg_ref, wu_ref, wd_ref = wg_in, wu_in, wd_in
        xbuf, xn_ref, yacc_ref, gate_ref, sem_x, sem_a, sem_s = rest
    e = pl.program_id(0)
    c = pl.program_id(1)
    f = pl.program_id(2)
    ne = pl.num_programs(0)
    nt = pl.num_programs(1)
    nf = pl.num_programs(2)
    tc = xn_ref.shape[0]
    g = e * nt + c
    slot = g % 2
    other = 1 - slot

    def start_gather(src_hbm, tiles, ids_ref, base, buf, sem):
        def issue(r, carry):
            pltpu.make_async_copy(src_hbm.at[pl.ds(ids_ref[base + r], tiles), :],
                                  xbuf.at[buf, pl.ds(r * ROW_PITCH, tiles), :], sem.at[buf]).start()
            return carry
        lax.fori_loop(0, tc, issue, 0, unroll=8)

    def wait_gather(src_hbm, tiles, buf, sem):
        n = tc * tiles
        pltpu.make_async_copy(src_hbm.at[pl.ds(0, n), :], xbuf.at[buf, pl.ds(0, n), :], sem.at[buf]).wait()

    def wait_scatter(buf):
        n = tc * ROW_TILES
        pltpu.make_async_copy(xbuf.at[buf, pl.ds(0, n), :], acc_hbm.at[pl.ds(0, n), :], sem_s.at[buf]).wait()

    def tile_rows(r0, n, k):
        return pl.ds(r0 * ROW_PITCH + k, n, stride=ROW_PITCH)

    rb = min(EXPERT_ROWS, tc)

    @pl.when(jnp.logical_and(g == 0, f == 0))
    def _():
        start_gather(tok_hbm, ROW_PITCH, idx_ref, 0, 0, sem_x)

    @pl.when(f == 0)
    def _():
        wait_gather(tok_hbm, ROW_PITCH, slot, sem_x)

        for r0 in range(0, tc, rb):
            for k in range(ROW_TILES):
                xn_ref[r0:r0 + rb, k * LANES:(k + 1) * LANES] = xbuf[slot, tile_rows(r0, rb, k), :].astype(BF16)
            aff = xbuf[slot, tile_rows(r0, rb, ROW_TILES), :]
            lane = lax.broadcasted_iota(I32, aff.shape, 1)
            gate_ref[r0:r0 + rb, :] = jnp.sum(jnp.where(lane == e, aff, 0.0), axis=-1, keepdims=True)
            yacc_ref[r0:r0 + rb, :] = jnp.zeros((rb, D_MODEL), F32)

    @pl.when(f == 1)
    def _():
        @pl.when(g > 0)
        def _():
            wait_scatter(other)

        start_gather(acc_hbm, ROW_TILES, idx_ref, c * tc, slot, sem_a)

        @pl.when(c + 1 < nt)
        def _():
            start_gather(tok_hbm, ROW_PITCH, idx_ref, (c + 1) * tc, other, sem_x)

        @pl.when(jnp.logical_and(c + 1 == nt, e + 1 < ne))
        def _():
            start_gather(tok_hbm, ROW_PITCH, idx_next_ref, 0, other, sem_x)

    def ffn(m, carry):
        rows = pl.ds(pl.multiple_of(m * rb, rb), rb)
        xn = xn_ref[rows, :]
        hg = jnp.dot(xn, wg_ref[0], preferred_element_type=F32)
        hu = jnp.dot(xn, wu_ref[0], preferred_element_type=F32)
        hid = (hg / (1.0 + jnp.exp(-hg)) * hu).astype(BF16)
        yacc_ref[rows, :] += jnp.dot(hid, wd_ref[0], preferred_element_type=F32)
        return carry
    lax.fori_loop(0, tc // rb, ffn, 0)

    @pl.when(f == nf - 1)
    def _():
        wait_gather(acc_hbm, ROW_TILES, slot, sem_a)

        for r0 in range(0, tc, rb):
            gate = gate_ref[r0:r0 + rb, :]
            for k in range(ROW_TILES):
                dst = tile_rows(r0, rb, k)
                xbuf[slot, dst, :] = xbuf[slot, dst, :] + yacc_ref[r0:r0 + rb, k * LANES:(k + 1) * LANES] * gate

        def issue(r, carry):
            pltpu.make_async_copy(xbuf.at[slot, pl.ds(r * ROW_PITCH, ROW_TILES), :],
                                  acc_hbm.at[pl.ds(idx_ref[c * tc + r], ROW_TILES), :], sem_s.at[slot]).start()
            return carry
        lax.fori_loop(0, tc, issue, 0, unroll=8)

        @pl.when(jnp.logical_and(e == ne - 1, c == nt - 1))
        def _():
            wait_scatter(slot)


def _experts(idx_flat, tok, acc, wg, wu, wd, cap):
    tc = min(EXPERT_TILE, cap)
    cast_weights = wg.dtype != BF16
    tf = 256 if cast_weights else 512
    assert not cast_weights or cap == tc, "each bf16 weight block must be written by exactly one grid step"
    w_specs = [
        pl.BlockSpec((1, D_MODEL, tf), lambda e, c, f: (e, 0, f)),
        pl.BlockSpec((1, D_MODEL, tf), lambda e, c, f: (e, 0, f)),
        pl.BlockSpec((1, tf, D_MODEL), lambda e, c, f: (e, f, 0)),
    ]
    acc_shape = jax.ShapeDtypeStruct(acc.shape, F32)
    acc_spec = pl.BlockSpec(memory_space=pl.ANY)
    w_shape = jax.ShapeDtypeStruct(wg.shape, BF16)
    out = pl.pallas_call(
        functools.partial(_experts_kernel, cast_weights=cast_weights),
        out_shape=(acc_shape, w_shape, w_shape, w_shape) if cast_weights else acc_shape,
        grid=(N_EXPERTS, cap // tc, EXPERT_FF // tf),
        in_specs=[
            pl.BlockSpec((cap,), lambda e, c, f: (e,), memory_space=pltpu.SMEM),
            pl.BlockSpec((cap,), lambda e, c, f: (jnp.minimum(e + 1, N_EXPERTS - 1),), memory_space=pltpu.SMEM),
            pl.BlockSpec(memory_space=pl.ANY),
            pl.BlockSpec(memory_space=pl.ANY),
            *w_specs,
        ],
        out_specs=(acc_spec, *w_specs) if cast_weights else acc_spec,
        scratch_shapes=[
            pltpu.VMEM((2, tc * ROW_PITCH, LANES), F32),
            pltpu.VMEM((tc, D_MODEL), BF16),
            pltpu.VMEM((tc, D_MODEL), F32),
            pltpu.VMEM((tc, 1), F32),
            pltpu.SemaphoreType.DMA((2,)),
            pltpu.SemaphoreType.DMA((2,)),
            pltpu.SemaphoreType.DMA((2,)),
        ],
        input_output_aliases={3: 0},
        compiler_params=_params(("arbitrary", "arbitrary", "arbitrary")),
        name="experts",
    )(idx_flat, idx_flat, tok, acc, wg, wu, wd)
    return out if cast_weights else (out, wg, wu, wd)


def _final_kernel(x_ref, g_ref, o_ref):
    tm = o_ref.shape[0]
    x = jnp.concatenate([x_ref[pl.ds(k, tm, stride=ROW_PITCH), :] for k in range(ROW_TILES)], axis=-1)
    o_ref[...] = _rmsnorm(x, g_ref[...])


def _final_norm(rows, g):
    t = rows.shape[0] // ROW_PITCH
    tm = min(512, t)
    return pl.pallas_call(
        _final_kernel,
        out_shape=jax.ShapeDtypeStruct((t, D_MODEL), F32),
        grid=(t // tm,),
        in_specs=[pl.BlockSpec((tm * ROW_PITCH, LANES), lambda i: (i, 0)),
                  pl.BlockSpec((1, D_MODEL), lambda i: (0, 0))],
        out_specs=pl.BlockSpec((tm, D_MODEL), lambda i: (i, 0)),
        compiler_params=_params(("parallel",)),
        name="final_norm",
    )(rows, g)


def _rotary_tables(seq):
    half = HEAD_DIM // 2
    inv = ROPE_BASE ** (-jnp.arange(half, dtype=F32) / half)
    ang = jnp.arange(seq, dtype=F32)[:, None] * inv[None, :]
    cos, sin = jnp.cos(ang), jnp.sin(ang)
    return jnp.concatenate([cos, cos], axis=-1), jnp.concatenate([-sin, sin], axis=-1)


def _trunk(x, w, expert_w):
    batch, seq, _ = x.shape
    t = batch * seq
    cap = EC_CAPACITY_FACTOR * t // N_EXPERTS
    x2d = x.reshape(t, D_MODEL)
    cos_t, sin_t = _rotary_tables(seq)

    proj = _in_proj(x2d, w["norm_mix_g"], w["w_in"], cos_t, sin_t, seq)
    o_f, o_b = _retention(proj, w["decays"], batch, seq)
    acc, tok, aff = _mix_out(o_f, o_b, proj, x2d, w["ret_gn_g"], w["conv_w"], w["w_o"],
                             w["norm_ffn_g"], w["w_router_t"], seq)
    idx = _select(aff.reshape(N_EXPERTS, t // LANES, LANES), cap)
    acc, *expert_w = _experts(idx.reshape(N_EXPERTS * cap), tok, acc, *expert_w, cap)
    return _final_norm(acc, w["final_norm_g"]).reshape(batch, seq, D_MODEL), expert_w


def kernel(x_prompt, x_sample, norm_mix_g, w_in, conv_w, ret_decay_fwd, ret_decay_bwd, ret_gn_g, w_o,
           norm_ffn_g, w_router, w_gate, w_up, w_down, final_norm_g):
    w = {
        "norm_mix_g": norm_mix_g[0][None, :],
        "w_in": w_in[0].astype(BF16),
        "conv_w": conv_w[0],
        "decays": jnp.stack([ret_decay_fwd[0], ret_decay_bwd[0]]),
        "ret_gn_g": ret_gn_g[0][None, :],
        "w_o": w_o[0].astype(BF16),
        "norm_ffn_g": norm_ffn_g[0][None, :],
        "w_router_t": w_router[0].T,
        "final_norm_g": final_norm_g[None, :],
    }
    y_prompt, expert_w = _trunk(x_prompt, w, (w_gate[0], w_up[0], w_down[0]))
    y_sample, _ = _trunk(x_sample, w, expert_w)
    return (y_prompt, y_sample)
```

```python
import functools

import jax
import jax.numpy as jnp
from jax import lax
from jax.experimental import pallas as pl
from jax.experimental.pallas import tpu as pltpu

F32 = jnp.float32
BF16 = jnp.bfloat16
I32 = jnp.int32

D_MODEL = 2048
RET_WIDTH = 1024
RET_HEADS = 8
HEAD_DIM = 128
IN_COLS = 7168
COL_BLOCK = 1024
N_EXPERTS = 16
EC_CAPACITY_FACTOR = 2
EXPERT_FF = 2048
CHUNK = 128
ROPE_BASE = 10000.0
EPS = 1e-6
LANES = 128
HALO_ROWS = 16
ROW_TILES = D_MODEL // LANES
ROW_PITCH = ROW_TILES + 1
RET_CHUNKS = 4
MIX_SPLIT = 2
EXPERT_TILE = 1024
EXPERT_ROWS = 256
TOK_PRIORITY = 0
ACC_PRIORITY = 1
VMEM_LIMIT = 56 * 1024 * 1024


def _params(sem, vmem=VMEM_LIMIT):
    return pltpu.CompilerParams(dimension_semantics=sem, vmem_limit_bytes=vmem)


def _rmsnorm(x, g):
    return x * lax.rsqrt(jnp.mean(x * x, axis=-1, keepdims=True) + EPS) * g


def _split_bf16(x):
    hi = x.astype(BF16)
    lo = (x - hi.astype(F32)).astype(BF16)
    return hi, lo


def _dot3(a, b, dims):
    a_hi, a_lo = _split_bf16(a)
    b_hi, b_lo = _split_bf16(b)
    d = functools.partial(lax.dot_general, dimension_numbers=dims, preferred_element_type=F32)
    return d(a_hi, b_hi) + d(a_hi, b_lo) + d(a_lo, b_hi)


def _inproj_kernel(x_ref, g_ref, w_ref, cos_ref, sin_ref, o_ref, h_ref, acc_ref):
    j = pl.program_id(1)

    @pl.when(j == 0)
    def _():
        h_ref[...] = _rmsnorm(x_ref[...], g_ref[...]).astype(BF16)

    @pl.when(j < 2)
    def _():
        acc_ref[...] = jnp.dot(h_ref[...], w_ref[...], preferred_element_type=F32)
        scale = jnp.where(j == 1, HEAD_DIM ** -0.5, 1.0).astype(F32)
        cos = cos_ref[...]
        sin = sin_ref[...]
        for h in range(COL_BLOCK // HEAD_DIM):
            sl = slice(h * HEAD_DIM, (h + 1) * HEAD_DIM)
            t = acc_ref[:, sl]
            o_ref[:, sl] = ((t * cos + pltpu.roll(t, HEAD_DIM // 2, axis=1) * sin) * scale).astype(BF16)

    @pl.when(j >= 2)
    def _():
        o_ref[...] = jnp.dot(h_ref[...], w_ref[...], preferred_element_type=F32).astype(BF16)


def _in_proj(x2d, g, w_bf16, cos_t, sin_t, seq):
    t = x2d.shape[0]
    tm = min(1024, seq)
    blocks_per_seq = seq // tm
    return pl.pallas_call(
        _inproj_kernel,
        out_shape=jax.ShapeDtypeStruct((t, IN_COLS), BF16),
        grid=(t // tm, IN_COLS // COL_BLOCK),
        in_specs=[
            pl.BlockSpec((tm, D_MODEL), lambda i, j: (i, 0)),
            pl.BlockSpec((1, D_MODEL), lambda i, j: (0, 0)),
            pl.BlockSpec((D_MODEL, COL_BLOCK), lambda i, j: (0, j)),
            pl.BlockSpec((tm, HEAD_DIM), lambda i, j: (i % blocks_per_seq, 0)),
            pl.BlockSpec((tm, HEAD_DIM), lambda i, j: (i % blocks_per_seq, 0)),
        ],
        out_specs=pl.BlockSpec((tm, COL_BLOCK), lambda i, j: (i, j)),
        scratch_shapes=[pltpu.VMEM((tm, D_MODEL), BF16), pltpu.VMEM((tm, COL_BLOCK), F32)],
        compiler_params=_params(("parallel", "arbitrary")),
        name="in_proj",
    )(x2d, g, w_bf16, cos_t, sin_t)


def _retention_kernel(dec_ref, qf_ref, kf_ref, vf_ref, qb_ref, kb_ref, vb_ref,
                      of_ref, ob_ref,
                      sf_ref, sb_ref, dtab_ref, xif_ref, xib_ref, zf_ref, zb_ref):
    first = jnp.logical_and(pl.program_id(0) == 0, pl.program_id(1) == 0)

    @pl.when(first)
    def _():
        row = lax.broadcasted_iota(I32, (CHUNK, CHUNK), 0).astype(F32)
        col = lax.broadcasted_iota(I32, (CHUNK, CHUNK), 1).astype(F32)
        diff = row - col
        for h in range(RET_HEADS):
            lgf = -jnp.exp(dec_ref[0:1, h:h + 1])
            lgb = -jnp.exp(dec_ref[1:2, h:h + 1])
            dtab_ref[h] = jnp.where(diff >= 0, jnp.exp(lgf * jnp.maximum(diff, 0.0)),
                                    jnp.exp(lgb * jnp.maximum(-diff, 0.0)))
            xif_ref[h] = jnp.exp(lgf * (row + 1.0))
            zf_ref[h] = jnp.exp(lgf * (CHUNK - 1.0 - row))
            xib_ref[h] = jnp.exp(lgb * (CHUNK - row))
            zb_ref[h] = jnp.exp(lgb * row)

    @pl.when(pl.program_id(1) == 0)
    def _():
        sf_ref[...] = jnp.zeros_like(sf_ref)
        sb_ref[...] = jnp.zeros_like(sb_ref)

    nt = (((1,), (1,)), ((), ()))
    tn = (((0,), (0,)), ((), ()))
    for h in range(RET_HEADS):
        sl = slice(h * HEAD_DIM, (h + 1) * HEAD_DIM)
        gcf = jnp.exp(-jnp.exp(dec_ref[0:1, h:h + 1]) * CHUNK)
        gcb = jnp.exp(-jnp.exp(dec_ref[1:2, h:h + 1]) * CHUNK)

        for u in range(RET_CHUNKS):
            rows = slice(u * CHUNK, (u + 1) * CHUNK)
            q = qf_ref[rows, sl]
            k = kf_ref[rows, sl]
            v = vf_ref[rows, sl]
            s = lax.dot_general(q, k, nt, preferred_element_type=F32) * dtab_ref[h]
            sf = sf_ref[h]
            o = jnp.dot(s.astype(BF16), v, preferred_element_type=F32)
            o = o + jnp.dot(q, sf.astype(BF16), preferred_element_type=F32) * xif_ref[h]
            of_ref[rows, sl] = o
            kz = (k.astype(F32) * zf_ref[h]).astype(BF16)
            sf_ref[h] = sf * gcf + lax.dot_general(kz, v, tn, preferred_element_type=F32)

        for u in reversed(range(RET_CHUNKS)):
            rows = slice(u * CHUNK, (u + 1) * CHUNK)
            q = qb_ref[rows, sl]
            v = vb_ref[rows, sl]
            sb = sb_ref[h]
            ob_ref[rows, sl] = jnp.dot(q, sb.astype(BF16), preferred_element_type=F32) * xib_ref[h]
            kz = (kb_ref[rows, sl].astype(F32) * zb_ref[h]).astype(BF16)
            sb_ref[h] = sb * gcb + lax.dot_general(kz, v, tn, preferred_element_type=F32)


def _retention(proj, decays, batch, seq):
    t = proj.shape[0]
    rows = RET_CHUNKS * CHUNK
    n = seq // rows
    fwd = lambda col: pl.BlockSpec((rows, COL_BLOCK), lambda b, c: (b * n + c, col))
    bwd = lambda col: pl.BlockSpec((rows, COL_BLOCK), lambda b, c: (b * n + n - 1 - c, col))
    table = pltpu.VMEM((RET_HEADS, CHUNK, CHUNK), F32)
    return pl.pallas_call(
        _retention_kernel,
        out_shape=(jax.ShapeDtypeStruct((t, RET_WIDTH), F32), jax.ShapeDtypeStruct((t, RET_WIDTH), F32)),
        grid=(batch, n),
        in_specs=[pl.BlockSpec((2, RET_HEADS), lambda b, c: (0, 0)),
                  fwd(0), fwd(1), fwd(2), bwd(0), bwd(1), bwd(2)],
        out_specs=(pl.BlockSpec((rows, RET_WIDTH), lambda b, c: (b * n + c, 0)),
                   pl.BlockSpec((rows, RET_WIDTH), lambda b, c: (b * n + n - 1 - c, 0))),
        scratch_shapes=[table] * 7,
        compiler_params=_params(("arbitrary", "arbitrary")),
        name="retention",
    )(decays, proj, proj, proj, proj, proj, proj)


def _mixout_kernel(of_ref, ob_ref, g_ref, cb_ref, cc_ref, ch_ref, ccp_ref, chp_ref, ccn_ref, chn_ref,
                   x_ref, gn_ref, cw_ref, wo_ref, g2_ref, wr_ref,
                   acc_ref, tok_ref, aff_ref, *, blocks_per_seq):
    i = pl.program_id(0)
    tm = x_ref.shape[0]

    gn = gn_ref[...]
    z = cc_ref[...].astype(F32) * ch_ref[...].astype(F32)
    pos = i % blocks_per_seq
    keep_prev = jnp.where(pos == 0, 0.0, 1.0).astype(F32)
    keep_next = jnp.where(pos == blocks_per_seq - 1, 0.0, 1.0).astype(F32)
    last = HALO_ROWS - 1
    halo_prev = (ccp_ref[...].astype(F32) * chp_ref[...].astype(F32))[last:, :] * keep_prev
    halo_next = (ccn_ref[...].astype(F32) * chn_ref[...].astype(F32))[0:1, :] * keep_next
    row = lax.broadcasted_iota(I32, z.shape, 0)
    z_prev = jnp.where(row == 0, halo_prev, pltpu.roll(z, 1, axis=0))
    z_next = jnp.where(row == tm - 1, halo_next, pltpu.roll(z, tm - 1, axis=0))
    cw = cw_ref[...]
    zc = z_prev * cw[0:1, :] + z * cw[1:2, :] + z_next * cw[2:3, :]
    eye = (lax.broadcasted_iota(I32, (N_EXPERTS, LANES), 0)
           == lax.broadcasted_iota(I32, (N_EXPERTS, LANES), 1)).astype(BF16)
    tn = (((0,), (0,)), ((), ()))
    tr = lambda a: lax.dot_general(a, eye, tn, preferred_element_type=F32)

    rs = tm // MIX_SPLIT
    for s in range(MIX_SPLIT):
        rows = slice(s * rs, (s + 1) * rs)
        o = of_ref[rows, :] + ob_ref[rows, :]
        gate_in = g_ref[rows, :].astype(F32)
        parts = []
        for h in range(RET_HEADS):
            sl = slice(h * HEAD_DIM, (h + 1) * HEAD_DIM)
            oh = o[:, sl]
            mu = jnp.mean(oh, axis=-1, keepdims=True)
            d = oh - mu
            var = jnp.mean(d * d, axis=-1, keepdims=True)
            gh = gate_in[:, sl]
            swish = gh / (1.0 + jnp.exp(-gh))
            parts.append((swish * (d * lax.rsqrt(var + EPS) * gn[:, sl])).astype(BF16))
        ret = jnp.concatenate(parts, axis=-1)
        conv = cb_ref[rows, :].astype(F32) * zc[rows, :]

        y = jnp.dot(ret, wo_ref[0:RET_WIDTH, :], preferred_element_type=F32)
        y = y + jnp.dot(conv.astype(BF16), wo_ref[RET_WIDTH:, :], preferred_element_type=F32)
        x1 = x_ref[rows, :] + y
        h2 = _rmsnorm(x1, g2_ref[...])
        for k in range(ROW_TILES):
            cols = slice(k * LANES, (k + 1) * LANES)
            dst = pl.ds(s * rs * ROW_PITCH + k, rs, stride=ROW_PITCH)
            acc_ref[dst, :] = x1[:, cols]
            tok_ref[dst, :] = h2[:, cols]
        last_tile = pl.ds(s * rs * ROW_PITCH + ROW_TILES, rs, stride=ROW_PITCH)
        acc_ref[last_tile, :] = jnp.zeros((rs, LANES), F32)
        logits = _dot3(wr_ref[...], h2, (((1,), (1,)), ((), ())))
        ex = jnp.exp(logits - jnp.max(logits, axis=0, keepdims=True))
        aff = ex / jnp.sum(ex, axis=0, keepdims=True)
        aff_ref[:, rows] = aff

        a1 = aff.astype(BF16)
        r1 = aff - a1.astype(F32)
        a2 = r1.astype(BF16)
        a3 = (r1 - a2.astype(F32)).astype(BF16)
        tok_ref[last_tile, :] = (tr(a1) + tr(a2)) + tr(a3)


def _mix_out(o_f, o_b, proj, x2d, gn_g, conv_w, wo_bf16, g2, wr_t, seq):
    t = x2d.shape[0]
    tm = min(256, seq)
    bps = seq // tm
    r8 = tm // HALO_ROWS
    n8 = t // HALO_ROWS
    col = lambda c: pl.BlockSpec((tm, COL_BLOCK), lambda i: (i, c))
    prev = lambda c: pl.BlockSpec((HALO_ROWS, COL_BLOCK), lambda i: (jnp.maximum(i * r8 - 1, 0), c))
    nxt = lambda c: pl.BlockSpec((HALO_ROWS, COL_BLOCK), lambda i: (jnp.minimum((i + 1) * r8, n8 - 1), c))
    whole = lambda shape: pl.BlockSpec(shape, lambda i: (0,) * len(shape))
    return pl.pallas_call(
        functools.partial(_mixout_kernel, blocks_per_seq=bps),
        out_shape=(jax.ShapeDtypeStruct((t * ROW_PITCH, LANES), F32),
                   jax.ShapeDtypeStruct((t * ROW_PITCH, LANES), F32),
                   jax.ShapeDtypeStruct((N_EXPERTS, t), F32)),
        grid=(t // tm,),
        in_specs=[
            pl.BlockSpec((tm, RET_WIDTH), lambda i: (i, 0)),
            pl.BlockSpec((tm, RET_WIDTH), lambda i: (i, 0)),
            col(3), col(4), col(5), col(6), prev(5), prev(6), nxt(5), nxt(6),
            pl.BlockSpec((tm, D_MODEL), lambda i: (i, 0)),
            whole((1, RET_WIDTH)), whole((3, COL_BLOCK)), whole((D_MODEL, D_MODEL)),
            whole((1, D_MODEL)), whole((N_EXPERTS, D_MODEL)),
        ],
        out_specs=(pl.BlockSpec((tm * ROW_PITCH, LANES), lambda i: (i, 0)),
                   pl.BlockSpec((tm * ROW_PITCH, LANES), lambda i: (i, 0)),
                   pl.BlockSpec((N_EXPERTS, tm), lambda i: (0, i))),
        compiler_params=_params(("parallel",)),
        name="mix_out",
    )(o_f, o_b, proj, proj, proj, proj, proj, proj, proj, proj, x2d, gn_g, conv_w, wo_bf16, g2, wr_t)


def _select_kernel(aff_ref, idx_ref, sel_ref, *, cap):
    e_n, rows, _ = aff_ref.shape
    bits = lax.bitcast_convert_type(aff_ref[...], I32)

    def count(mask):
        c = jnp.sum(jnp.where(mask, 1.0, 0.0), axis=1, keepdims=True)
        return jnp.sum(c, axis=2, keepdims=True)

    thr = jnp.zeros((e_n, 1, 1), I32)
    for b in range(30, -1, -1):
        cand = thr | (1 << b)
        thr = jnp.where(count(bits >= cand) >= cap, cand, thr)

    gt = bits > thr
    eq = bits == thr
    need = cap - count(gt)

    li = lax.broadcasted_iota(I32, (LANES, LANES), 0)
    lj = lax.broadcasted_iota(I32, (LANES, LANES), 1)
    upper = (li <= lj).astype(BF16)
    ri = lax.broadcasted_iota(I32, (rows, rows), 0)
    rj = lax.broadcasted_iota(I32, (rows, rows), 1)
    lower = (rj < ri).astype(BF16)
    eqf = jnp.where(eq, 1.0, 0.0)
    for e in range(e_n):
        incl = jnp.dot(eqf[e].astype(BF16), upper, preferred_element_type=F32)
        row_tot = jnp.broadcast_to(incl[:, LANES - 1:LANES], (rows, LANES)).astype(BF16)
        row_off = jnp.dot(lower, row_tot, preferred_element_type=F32)
        rank = row_off + incl - eqf[e]
        take = jnp.logical_or(gt[e], jnp.logical_and(eq[e], rank < need[e]))
        sel_ref[e] = jnp.where(take, 1.0, 0.0)

    chunk = min(512, cap)
    lower_incl = (li >= lj).astype(BF16)
    slot0 = lax.broadcasted_iota(I32, (1, chunk), 1).astype(F32)
    rowid = lax.broadcasted_iota(I32, (rows, chunk), 0).astype(F32)

    def compact(e, carry):
        s = sel_ref[e].astype(BF16)
        q_t = lax.dot_general(lower_incl, s, (((1,), (1,)), ((), ())),
                              preferred_element_type=F32)
        q = jnp.dot(s, upper, preferred_element_type=F32)
        row_tot = q[:, LANES - 1:LANES]
        row_cum = jnp.dot(lower, jnp.broadcast_to(row_tot, (rows, LANES)).astype(BF16),
                          preferred_element_type=F32)[:, 0:1] + row_tot
        q_t = q_t.astype(BF16)
        for ch in range(cap // chunk):
            c = slot0 + float(ch * chunk)
            before = row_cum <= c
            r_c = jnp.sum(jnp.where(before, 1.0, 0.0), axis=0, keepdims=True)
            k = c - jnp.sum(jnp.where(before, row_tot, 0.0), axis=0, keepdims=True)
            onehot = jnp.where(rowid == r_c, 1.0, 0.0).astype(BF16)
            q_row = jnp.dot(q_t, onehot, preferred_element_type=F32)
            lane_c = jnp.sum(jnp.where(q_row <= k, 1.0, 0.0), axis=0, keepdims=True)
            token = r_c * LANES + lane_c
            idx_ref[pl.ds(e, 1), ch * chunk:(ch + 1) * chunk] = (token * ROW_PITCH).astype(I32)
        return carry

    lax.fori_loop(0, e_n, compact, 0)


def _select(aff3, cap):
    e_n, rows, lanes = aff3.shape
    return pl.pallas_call(
        functools.partial(_select_kernel, cap=cap),
        out_shape=jax.ShapeDtypeStruct((e_n, cap), I32),
        grid=(1,),
        in_specs=[pl.BlockSpec((e_n, rows, lanes), lambda i: (0, 0, 0))],
        out_specs=pl.BlockSpec((e_n, cap), lambda i: (0, 0)),
        scratch_shapes=[pltpu.VMEM((e_n, rows, lanes), F32)],
        compiler_params=_params(("arbitrary",)),
        name="select",
    )(aff3)


def _experts_kernel(idx_ref, idx_next_ref, tok_hbm, acc_in_hbm, wg_in, wu_in, wd_in, acc_hbm, *rest,
                    cast_weights):
    del acc_in_hbm
    if cast_weights:
        wg_ref, wu_ref, wd_ref, xbuf, xn_ref, yacc_ref, gate_ref, sem_x, sem_a, sem_s = rest
        wg_ref[...] = wg_in[...].astype(BF16)
        wu_ref[...] = wu_in[...].astype(BF16)
        wd_ref[...] = wd_in[...].astype(BF16)
    else:
        wg_ref, wu_ref, wd_ref = wg_in, wu_in, wd_in
        xbuf, xn_ref, yacc_ref, gate_ref, sem_x, sem_a, sem_s = rest
    e = pl.program_id(0)
    c = pl.program_id(1)
    f = pl.program_id(2)
    ne = pl.num_programs(0)
    nt = pl.num_programs(1)
    nf = pl.num_programs(2)
    tc = xn_ref.shape[0]
    g = e * nt + c
    slot = g % 2
    other = 1 - slot

    def start_gather(src_hbm, tiles, ids_ref, base, buf, sem, priority):
        def issue(r, carry):
            pltpu.make_async_copy(src_hbm.at[pl.ds(ids_ref[base + r], tiles), :],
                                  xbuf.at[buf, pl.ds(r * ROW_PITCH, tiles), :], sem.at[buf]).start(priority)
            return carry
        lax.fori_loop(0, tc, issue, 0, unroll=8)

    def wait_gather(src_hbm, tiles, buf, sem):
        n = tc * tiles
        pltpu.make_async_copy(src_hbm.at[pl.ds(0, n), :], xbuf.at[buf, pl.ds(0, n), :], sem.at[buf]).wait()

    def wait_scatter(buf):
        n = tc * ROW_TILES
        pltpu.make_async_copy(xbuf.at[buf, pl.ds(0, n), :], acc_hbm.at[pl.ds(0, n), :], sem_s.at[buf]).wait()

    def tile_rows(r0, n, k):
        return pl.ds(r0 * ROW_PITCH + k, n, stride=ROW_PITCH)

    rb = min(EXPERT_ROWS, tc)

    @pl.when(jnp.logical_and(g == 0, f == 0))
    def _():
        start_gather(tok_hbm, ROW_PITCH, idx_ref, 0, 0, sem_x, TOK_PRIORITY)

    @pl.when(f == 0)
    def _():
        wait_gather(tok_hbm, ROW_PITCH, slot, sem_x)

        for r0 in range(0, tc, rb):
            for k in range(ROW_TILES):
                xn_ref[r0:r0 + rb, k * LANES:(k + 1) * LANES] = xbuf[slot, tile_rows(r0, rb, k), :].astype(BF16)
            aff = xbuf[slot, tile_rows(r0, rb, ROW_TILES), :]
            lane = lax.broadcasted_iota(I32, aff.shape, 1)
            gate_ref[r0:r0 + rb, :] = jnp.sum(jnp.where(lane == e, aff, 0.0), axis=-1, keepdims=True)
            yacc_ref[r0:r0 + rb, :] = jnp.zeros((rb, D_MODEL), F32)

    @pl.when(f == 1)
    def _():
        @pl.when(g > 0)
        def _():
            wait_scatter(other)

        start_gather(acc_hbm, ROW_TILES, idx_ref, c * tc, slot, sem_a, ACC_PRIORITY)

        @pl.when(c + 1 < nt)
        def _():
            start_gather(tok_hbm, ROW_PITCH, idx_ref, (c + 1) * tc, other, sem_x, TOK_PRIORITY)

        @pl.when(jnp.logical_and(c + 1 == nt, e + 1 < ne))
        def _():
            start_gather(tok_hbm, ROW_PITCH, idx_next_ref, 0, other, sem_x, TOK_PRIORITY)

    def ffn(m, carry):
        rows = pl.ds(pl.multiple_of(m * rb, rb), rb)
        xn = xn_ref[rows, :]
        hg = jnp.dot(xn, wg_ref[0], preferred_element_type=F32)
        hu = jnp.dot(xn, wu_ref[0], preferred_element_type=F32)
        hid = (hg / (1.0 + jnp.exp(-hg)) * hu).astype(BF16)
        yacc_ref[rows, :] += jnp.dot(hid, wd_ref[0], preferred_element_type=F32)
        return carry
    lax.fori_loop(0, tc // rb, ffn, 0)

    @pl.when(f == nf - 1)
    def _():
        wait_gather(acc_hbm, ROW_TILES, slot, sem_a)

        for r0 in range(0, tc, rb):
            gate = gate_ref[r0:r0 + rb, :]
            for k in range(ROW_TILES):
                dst = tile_rows(r0, rb, k)
                xbuf[slot, dst, :] = xbuf[slot, dst, :] + yacc_ref[r0:r0 + rb, k * LANES:(k + 1) * LANES] * gate

        def issue(pair, carry):
            for p in range(2):
                r = 2 * pair + p
                pltpu.make_async_copy(xbuf.at[slot, pl.ds(r * ROW_PITCH, ROW_TILES), :],
                                      acc_hbm.at[pl.ds(idx_ref[c * tc + r], ROW_TILES), :],
                                      sem_s.at[slot]).start(p)
            return carry
        lax.fori_loop(0, tc // 2, issue, 0, unroll=4)

        @pl.when(jnp.logical_and(e == ne - 1, c == nt - 1))
        def _():
            wait_scatter(slot)


def _experts(idx_flat, tok, acc, wg, wu, wd, cap):
    tc = min(EXPERT_TILE, cap)
    cast_weights = wg.dtype != BF16
    tf = 256 if cast_weights else 512
    assert not cast_weights or cap == tc, "each bf16 weight block must be written by exactly one grid step"
    w_specs = [
        pl.BlockSpec((1, D_MODEL, tf), lambda e, c, f: (e, 0, f)),
        pl.BlockSpec((1, D_MODEL, tf), lambda e, c, f: (e, 0, f)),
        pl.BlockSpec((1, tf, D_MODEL), lambda e, c, f: (e, f, 0)),
    ]
    acc_shape = jax.ShapeDtypeStruct(acc.shape, F32)
    acc_spec = pl.BlockSpec(memory_space=pl.ANY)
    w_shape = jax.ShapeDtypeStruct(wg.shape, BF16)
    out = pl.pallas_call(
        functools.partial(_experts_kernel, cast_weights=cast_weights),
        out_shape=(acc_shape, w_shape, w_shape, w_shape) if cast_weights else acc_shape,
        grid=(N_EXPERTS, cap // tc, EXPERT_FF // tf),
        in_specs=[
            pl.BlockSpec((cap,), lambda e, c, f: (e,), memory_space=pltpu.SMEM),
            pl.BlockSpec((cap,), lambda e, c, f: (jnp.minimum(e + 1, N_EXPERTS - 1),), memory_space=pltpu.SMEM),
            pl.BlockSpec(memory_space=pl.ANY),
            pl.BlockSpec(memory_space=pl.ANY),
            *w_specs,
        ],
        out_specs=(acc_spec, *w_specs) if cast_weights else acc_spec,
        scratch_shapes=[
            pltpu.VMEM((2, tc * ROW_PITCH, LANES), F32),
            pltpu.VMEM((tc, D_MODEL), BF16),
            pltpu.VMEM((tc, D_MODEL), F32),
            pltpu.VMEM((tc, 1), F32),
            pltpu.SemaphoreType.DMA((2,)),
            pltpu.SemaphoreType.DMA((2,)),
            pltpu.SemaphoreType.DMA((2,)),
        ],
        input_output_aliases={3: 0},
        compiler_params=_params(("arbitrary", "arbitrary", "arbitrary")),
        name="experts",
    )(idx_flat, idx_flat, tok, acc, wg, wu, wd)
    return out if cast_weights else (out, wg, wu, wd)


def _final_kernel(x_ref, g_ref, o_ref):
    tm = o_ref.shape[0]
    x = jnp.concatenate([x_ref[pl.ds(k, tm, stride=ROW_PITCH), :] for k in range(ROW_TILES)], axis=-1)
    o_ref[...] = _rmsnorm(x, g_ref[...])


def _final_norm(rows, g):
    t = rows.shape[0] // ROW_PITCH
    tm = min(512, t)
    return pl.pallas_call(
        _final_kernel,
        out_shape=jax.ShapeDtypeStruct((t, D_MODEL), F32),
        grid=(t // tm,),
        in_specs=[pl.BlockSpec((tm * ROW_PITCH, LANES), lambda i: (i, 0)),
                  pl.BlockSpec((1, D_MODEL), lambda i: (0, 0))],
        out_specs=pl.BlockSpec((tm, D_MODEL), lambda i: (i, 0)),
        compiler_params=_params(("parallel",)),
        name="final_norm",
    )(rows, g)


def _rotary_tables(seq):
    half = HEAD_DIM // 2
    inv = ROPE_BASE ** (-jnp.arange(half, dtype=F32) / half)
    ang = jnp.arange(seq, dtype=F32)[:, None] * inv[None, :]
    cos, sin = jnp.cos(ang), jnp.sin(ang)
    return jnp.concatenate([cos, cos], axis=-1), jnp.concatenate([-sin, sin], axis=-1)


def _trunk(x, w, expert_w):
    batch, seq, _ = x.shape
    t = batch * seq
    cap = EC_CAPACITY_FACTOR * t // N_EXPERTS
    x2d = x.reshape(t, D_MODEL)
    cos_t, sin_t = _rotary_tables(seq)

    proj = _in_proj(x2d, w["norm_mix_g"], w["w_in"], cos_t, sin_t, seq)
    o_f, o_b = _retention(proj, w["decays"], batch, seq)
    acc, tok, aff = _mix_out(o_f, o_b, proj, x2d, w["ret_gn_g"], w["conv_w"], w["w_o"],
                             w["norm_ffn_g"], w["w_router_t"], seq)
    idx = _select(aff.reshape(N_EXPERTS, t // LANES, LANES), cap)
    acc, *expert_w = _experts(idx.reshape(N_EXPERTS * cap), tok, acc, *expert_w, cap)
    return _final_norm(acc, w["final_norm_g"]).reshape(batch, seq, D_MODEL), expert_w


def kernel(x_prompt, x_sample, norm_mix_g, w_in, conv_w, ret_decay_fwd, ret_decay_bwd, ret_gn_g, w_o,
           norm_ffn_g, w_router, w_gate, w_up, w_down, final_norm_g):
    w = {
        "norm_mix_g": norm_mix_g[0][None, :],
        "w_in": w_in[0].astype(BF16),
        "conv_w": conv_w[0],
        "decays": jnp.stack([ret_decay_fwd[0], ret_decay_bwd[0]]),
        "ret_gn_g": ret_gn_g[0][None, :],
        "w_o": w_o[0].astype(BF16),
        "norm_ffn_g": norm_ffn_g[0][None, :],
        "w_router_t": w_router[0].T,
        "final_norm_g": final_norm_g[None, :],
    }
    y_prompt, expert_w = _trunk(x_prompt, w, (w_gate[0], w_up[0], w_down[0]))
    y_sample, _ = _trunk(x_sample, w, expert_w)
    return (y_prompt, y_sample)
```

```python
import functools

import jax
import jax.numpy as jnp
from jax import lax
from jax.experimental import pallas as pl
from jax.experimental.pallas import tpu as pltpu

F32 = jnp.float32
BF16 = jnp.bfloat16
I32 = jnp.int32

D_MODEL = 2048
RET_WIDTH = 1024
RET_HEADS = 8
HEAD_DIM = 128
IN_COLS = 7168
COL_BLOCK = 1024
N_EXPERTS = 16
EC_CAPACITY_FACTOR = 2
EXPERT_FF = 2048
CHUNK = 128
ROPE_BASE = 10000.0
EPS = 1e-6
LANES = 128
HALO_ROWS = 16
ROW_TILES = D_MODEL // LANES
ROW_PITCH = ROW_TILES + 1
RET_CHUNKS = 4
MIX_SPLIT = 2
EXPERT_TILE = 1024
EXPERT_ROWS = 512
ISSUE_UNROLL = 64
TOK_PRIORITY = 0
ACC_PRIORITY = 1
VMEM_LIMIT = 56 * 1024 * 1024


def _params(sem, vmem=VMEM_LIMIT):
    return pltpu.CompilerParams(dimension_semantics=sem, vmem_limit_bytes=vmem)


def _rmsnorm(x, g):
    return x * lax.rsqrt(jnp.mean(x * x, axis=-1, keepdims=True) + EPS) * g


def _split_bf16(x):
    hi = x.astype(BF16)
    lo = (x - hi.astype(F32)).astype(BF16)
    return hi, lo


def _dot3(a, b, dims):
    a_hi, a_lo = _split_bf16(a)
    b_hi, b_lo = _split_bf16(b)
    d = functools.partial(lax.dot_general, dimension_numbers=dims, preferred_element_type=F32)
    return d(a_hi, b_hi) + d(a_hi, b_lo) + d(a_lo, b_hi)


def _inproj_kernel(x_ref, g_ref, w_ref, cos_ref, sin_ref, o_ref, h_ref, acc_ref):
    j = pl.program_id(1)

    @pl.when(j == 0)
    def _():
        h_ref[...] = _rmsnorm(x_ref[...], g_ref[...]).astype(BF16)

    @pl.when(j < 2)
    def _():
        acc_ref[...] = jnp.dot(h_ref[...], w_ref[...], preferred_element_type=F32)
        scale = jnp.where(j == 1, HEAD_DIM ** -0.5, 1.0).astype(F32)
        cos = cos_ref[...]
        sin = sin_ref[...]
        for h in range(COL_BLOCK // HEAD_DIM):
            sl = slice(h * HEAD_DIM, (h + 1) * HEAD_DIM)
            t = acc_ref[:, sl]
            o_ref[:, sl] = ((t * cos + pltpu.roll(t, HEAD_DIM // 2, axis=1) * sin) * scale).astype(BF16)

    @pl.when(j >= 2)
    def _():
        o_ref[...] = jnp.dot(h_ref[...], w_ref[...], preferred_element_type=F32).astype(BF16)


def _in_proj(x2d, g, w_bf16, cos_t, sin_t, seq):
    t = x2d.shape[0]
    tm = min(1024, seq)
    blocks_per_seq = seq // tm
    return pl.pallas_call(
        _inproj_kernel,
        out_shape=jax.ShapeDtypeStruct((t, IN_COLS), BF16),
        grid=(t // tm, IN_COLS // COL_BLOCK),
        in_specs=[
            pl.BlockSpec((tm, D_MODEL), lambda i, j: (i, 0)),
            pl.BlockSpec((1, D_MODEL), lambda i, j: (0, 0)),
            pl.BlockSpec((D_MODEL, COL_BLOCK), lambda i, j: (0, j)),
            pl.BlockSpec((tm, HEAD_DIM), lambda i, j: (i % blocks_per_seq, 0)),
            pl.BlockSpec((tm, HEAD_DIM), lambda i, j: (i % blocks_per_seq, 0)),
        ],
        out_specs=pl.BlockSpec((tm, COL_BLOCK), lambda i, j: (i, j)),
        scratch_shapes=[pltpu.VMEM((tm, D_MODEL), BF16), pltpu.VMEM((tm, COL_BLOCK), F32)],
        compiler_params=_params(("parallel", "arbitrary")),
        name="in_proj",
    )(x2d, g, w_bf16, cos_t, sin_t)


def _retention_kernel(dec_ref, qf_ref, kf_ref, vf_ref, qb_ref, kb_ref, vb_ref,
                      of_ref, ob_ref,
                      sf_ref, sb_ref, dtab_ref, xif_ref, xib_ref, zf_ref, zb_ref):
    first = jnp.logical_and(pl.program_id(0) == 0, pl.program_id(1) == 0)

    @pl.when(first)
    def _():
        row = lax.broadcasted_iota(I32, (CHUNK, CHUNK), 0).astype(F32)
        col = lax.broadcasted_iota(I32, (CHUNK, CHUNK), 1).astype(F32)
        diff = row - col
        for h in range(RET_HEADS):
            lgf = -jnp.exp(dec_ref[0:1, h:h + 1])
            lgb = -jnp.exp(dec_ref[1:2, h:h + 1])
            dtab_ref[h] = jnp.where(diff >= 0, jnp.exp(lgf * jnp.maximum(diff, 0.0)),
                                    jnp.exp(lgb * jnp.maximum(-diff, 0.0)))
            xif_ref[h] = jnp.exp(lgf * (row + 1.0))
            zf_ref[h] = jnp.exp(lgf * (CHUNK - 1.0 - row))
            xib_ref[h] = jnp.exp(lgb * (CHUNK - row))
            zb_ref[h] = jnp.exp(lgb * row)

    @pl.when(pl.program_id(1) == 0)
    def _():
        sf_ref[...] = jnp.zeros_like(sf_ref)
        sb_ref[...] = jnp.zeros_like(sb_ref)

    nt = (((1,), (1,)), ((), ()))
    tn = (((0,), (0,)), ((), ()))
    for h in range(RET_HEADS):
        sl = slice(h * HEAD_DIM, (h + 1) * HEAD_DIM)
        gcf = jnp.exp(-jnp.exp(dec_ref[0:1, h:h + 1]) * CHUNK)
        gcb = jnp.exp(-jnp.exp(dec_ref[1:2, h:h + 1]) * CHUNK)

        for u in range(RET_CHUNKS):
            rows = slice(u * CHUNK, (u + 1) * CHUNK)
            q = qf_ref[rows, sl]
            k = kf_ref[rows, sl]
            v = vf_ref[rows, sl]
            s = lax.dot_general(q, k, nt, preferred_element_type=F32) * dtab_ref[h]
            sf = sf_ref[h]
            o = jnp.dot(s.astype(BF16), v, preferred_element_type=F32)
            o = o + jnp.dot(q, sf.astype(BF16), preferred_element_type=F32) * xif_ref[h]
            of_ref[rows, sl] = o
            kz = (k.astype(F32) * zf_ref[h]).astype(BF16)
            sf_ref[h] = sf * gcf + lax.dot_general(kz, v, tn, preferred_element_type=F32)

        for u in reversed(range(RET_CHUNKS)):
            rows = slice(u * CHUNK, (u + 1) * CHUNK)
            q = qb_ref[rows, sl]
            v = vb_ref[rows, sl]
            sb = sb_ref[h]
            ob_ref[rows, sl] = jnp.dot(q, sb.astype(BF16), preferred_element_type=F32) * xib_ref[h]
            kz = (kb_ref[rows, sl].astype(F32) * zb_ref[h]).astype(BF16)
            sb_ref[h] = sb * gcb + lax.dot_general(kz, v, tn, preferred_element_type=F32)


def _retention(proj, decays, batch, seq):
    t = proj.shape[0]
    rows = RET_CHUNKS * CHUNK
    n = seq // rows
    fwd = lambda col: pl.BlockSpec((rows, COL_BLOCK), lambda b, c: (b * n + c, col))
    bwd = lambda col: pl.BlockSpec((rows, COL_BLOCK), lambda b, c: (b * n + n - 1 - c, col))
    table = pltpu.VMEM((RET_HEADS, CHUNK, CHUNK), F32)
    return pl.pallas_call(
        _retention_kernel,
        out_shape=(jax.ShapeDtypeStruct((t, RET_WIDTH), F32), jax.ShapeDtypeStruct((t, RET_WIDTH), F32)),
        grid=(batch, n),
        in_specs=[pl.BlockSpec((2, RET_HEADS), lambda b, c: (0, 0)),
                  fwd(0), fwd(1), fwd(2), bwd(0), bwd(1), bwd(2)],
        out_specs=(pl.BlockSpec((rows, RET_WIDTH), lambda b, c: (b * n + c, 0)),
                   pl.BlockSpec((rows, RET_WIDTH), lambda b, c: (b * n + n - 1 - c, 0))),
        scratch_shapes=[table] * 7,
        compiler_params=_params(("arbitrary", "arbitrary")),
        name="retention",
    )(decays, proj, proj, proj, proj, proj, proj)


def _mixout_kernel(of_ref, ob_ref, g_ref, cb_ref, cc_ref, ch_ref, ccp_ref, chp_ref, ccn_ref, chn_ref,
                   x_ref, gn_ref, cw_ref, wo_ref, g2_ref, wr_ref,
                   acc_ref, tok_ref, aff_ref, *, blocks_per_seq):
    i = pl.program_id(0)
    tm = x_ref.shape[0]

    gn = gn_ref[...]
    z = cc_ref[...].astype(F32) * ch_ref[...].astype(F32)
    pos = i % blocks_per_seq
    keep_prev = jnp.where(pos == 0, 0.0, 1.0).astype(F32)
    keep_next = jnp.where(pos == blocks_per_seq - 1, 0.0, 1.0).astype(F32)
    last = HALO_ROWS - 1
    halo_prev = (ccp_ref[...].astype(F32) * chp_ref[...].astype(F32))[last:, :] * keep_prev
    halo_next = (ccn_ref[...].astype(F32) * chn_ref[...].astype(F32))[0:1, :] * keep_next
    row = lax.broadcasted_iota(I32, z.shape, 0)
    z_prev = jnp.where(row == 0, halo_prev, pltpu.roll(z, 1, axis=0))
    z_next = jnp.where(row == tm - 1, halo_next, pltpu.roll(z, tm - 1, axis=0))
    cw = cw_ref[...]
    zc = z_prev * cw[0:1, :] + z * cw[1:2, :] + z_next * cw[2:3, :]
    eye = (lax.broadcasted_iota(I32, (N_EXPERTS, LANES), 0)
           == lax.broadcasted_iota(I32, (N_EXPERTS, LANES), 1)).astype(BF16)
    tn = (((0,), (0,)), ((), ()))
    tr = lambda a: lax.dot_general(a, eye, tn, preferred_element_type=F32)

    rs = tm // MIX_SPLIT
    for s in range(MIX_SPLIT):
        rows = slice(s * rs, (s + 1) * rs)
        o = of_ref[rows, :] + ob_ref[rows, :]
        gate_in = g_ref[rows, :].astype(F32)
        parts = []
        for h in range(RET_HEADS):
            sl = slice(h * HEAD_DIM, (h + 1) * HEAD_DIM)
            oh = o[:, sl]
            mu = jnp.mean(oh, axis=-1, keepdims=True)
            d = oh - mu
            var = jnp.mean(d * d, axis=-1, keepdims=True)
            gh = gate_in[:, sl]
            swish = gh / (1.0 + jnp.exp(-gh))
            parts.append((swish * (d * lax.rsqrt(var + EPS) * gn[:, sl])).astype(BF16))
        ret = jnp.concatenate(parts, axis=-1)
        conv = cb_ref[rows, :].astype(F32) * zc[rows, :]

        y = jnp.dot(ret, wo_ref[0:RET_WIDTH, :], preferred_element_type=F32)
        y = y + jnp.dot(conv.astype(BF16), wo_ref[RET_WIDTH:, :], preferred_element_type=F32)
        x1 = x_ref[rows, :] + y
        h2 = _rmsnorm(x1, g2_ref[...])
        for k in range(ROW_TILES):
            cols = slice(k * LANES, (k + 1) * LANES)
            dst = pl.ds(s * rs * ROW_PITCH + k, rs, stride=ROW_PITCH)
            acc_ref[dst, :] = x1[:, cols]
            tok_ref[dst, :] = h2[:, cols]
        last_tile = pl.ds(s * rs * ROW_PITCH + ROW_TILES, rs, stride=ROW_PITCH)
        acc_ref[last_tile, :] = jnp.zeros((rs, LANES), F32)
        logits = _dot3(wr_ref[...], h2, (((1,), (1,)), ((), ())))
        ex = jnp.exp(logits - jnp.max(logits, axis=0, keepdims=True))
        aff = ex / jnp.sum(ex, axis=0, keepdims=True)
        aff_ref[:, rows] = aff

        a1 = aff.astype(BF16)
        r1 = aff - a1.astype(F32)
        a2 = r1.astype(BF16)
        a3 = (r1 - a2.astype(F32)).astype(BF16)
        tok_ref[last_tile, :] = (tr(a1) + tr(a2)) + tr(a3)


def _mix_out(o_f, o_b, proj, x2d, gn_g, conv_w, wo_bf16, g2, wr_t, seq):
    t = x2d.shape[0]
    tm = min(256, seq)
    bps = seq // tm
    r8 = tm // HALO_ROWS
    n8 = t // HALO_ROWS
    col = lambda c: pl.BlockSpec((tm, COL_BLOCK), lambda i: (i, c))
    prev = lambda c: pl.BlockSpec((HALO_ROWS, COL_BLOCK), lambda i: (jnp.maximum(i * r8 - 1, 0), c))
    nxt = lambda c: pl.BlockSpec((HALO_ROWS, COL_BLOCK), lambda i: (jnp.minimum((i + 1) * r8, n8 - 1), c))
    whole = lambda shape: pl.BlockSpec(shape, lambda i: (0,) * len(shape))
    return pl.pallas_call(
        functools.partial(_mixout_kernel, blocks_per_seq=bps),
        out_shape=(jax.ShapeDtypeStruct((t * ROW_PITCH, LANES), F32),
                   jax.ShapeDtypeStruct((t * ROW_PITCH, LANES), F32),
                   jax.ShapeDtypeStruct((N_EXPERTS, t), F32)),
        grid=(t // tm,),
        in_specs=[
            pl.BlockSpec((tm, RET_WIDTH), lambda i: (i, 0)),
            pl.BlockSpec((tm, RET_WIDTH), lambda i: (i, 0)),
            col(3), col(4), col(5), col(6), prev(5), prev(6), nxt(5), nxt(6),
            pl.BlockSpec((tm, D_MODEL), lambda i: (i, 0)),
            whole((1, RET_WIDTH)), whole((3, COL_BLOCK)), whole((D_MODEL, D_MODEL)),
            whole((1, D_MODEL)), whole((N_EXPERTS, D_MODEL)),
        ],
        out_specs=(pl.BlockSpec((tm * ROW_PITCH, LANES), lambda i: (i, 0)),
                   pl.BlockSpec((tm * ROW_PITCH, LANES), lambda i: (i, 0)),
                   pl.BlockSpec((N_EXPERTS, tm), lambda i: (0, i))),
        compiler_params=_params(("parallel",)),
        name="mix_out",
    )(o_f, o_b, proj, proj, proj, proj, proj, proj, proj, proj, x2d, gn_g, conv_w, wo_bf16, g2, wr_t)


def _select_kernel(aff_ref, idx_ref, sel_ref, *, cap):
    e_n, rows, _ = aff_ref.shape
    bits = lax.bitcast_convert_type(aff_ref[...], I32)

    def count(mask):
        c = jnp.sum(jnp.where(mask, 1.0, 0.0), axis=1, keepdims=True)
        return jnp.sum(c, axis=2, keepdims=True)

    thr = jnp.zeros((e_n, 1, 1), I32)
    for b in range(30, -1, -1):
        cand = thr | (1 << b)
        thr = jnp.where(count(bits >= cand) >= cap, cand, thr)

    gt = bits > thr
    eq = bits == thr
    need = cap - count(gt)

    li = lax.broadcasted_iota(I32, (LANES, LANES), 0)
    lj = lax.broadcasted_iota(I32, (LANES, LANES), 1)
    upper = (li <= lj).astype(BF16)
    ri = lax.broadcasted_iota(I32, (rows, rows), 0)
    rj = lax.broadcasted_iota(I32, (rows, rows), 1)
    lower = (rj < ri).astype(BF16)
    eqf = jnp.where(eq, 1.0, 0.0)
    for e in range(e_n):
        incl = jnp.dot(eqf[e].astype(BF16), upper, preferred_element_type=F32)
        row_tot = jnp.broadcast_to(incl[:, LANES - 1:LANES], (rows, LANES)).astype(BF16)
        row_off = jnp.dot(lower, row_tot, preferred_element_type=F32)
        rank = row_off + incl - eqf[e]
        take = jnp.logical_or(gt[e], jnp.logical_and(eq[e], rank < need[e]))
        sel_ref[e] = jnp.where(take, 1.0, 0.0)

    chunk = min(512, cap)
    lower_incl = (li >= lj).astype(BF16)
    slot0 = lax.broadcasted_iota(I32, (1, chunk), 1).astype(F32)
    rowid = lax.broadcasted_iota(I32, (rows, chunk), 0).astype(F32)

    def compact(e, carry):
        s = sel_ref[e].astype(BF16)
        q_t = lax.dot_general(lower_incl, s, (((1,), (1,)), ((), ())),
                              preferred_element_type=F32)
        q = jnp.dot(s, upper, preferred_element_type=F32)
        row_tot = q[:, LANES - 1:LANES]
        row_cum = jnp.dot(lower, jnp.broadcast_to(row_tot, (rows, LANES)).astype(BF16),
                          preferred_element_type=F32)[:, 0:1] + row_tot
        q_t = q_t.astype(BF16)
        for ch in range(cap // chunk):
            c = slot0 + float(ch * chunk)
            before = row_cum <= c
            r_c = jnp.sum(jnp.where(before, 1.0, 0.0), axis=0, keepdims=True)
            k = c - jnp.sum(jnp.where(before, row_tot, 0.0), axis=0, keepdims=True)
            onehot = jnp.where(rowid == r_c, 1.0, 0.0).astype(BF16)
            q_row = jnp.dot(q_t, onehot, preferred_element_type=F32)
            lane_c = jnp.sum(jnp.where(q_row <= k, 1.0, 0.0), axis=0, keepdims=True)
            token = r_c * LANES + lane_c
            idx_ref[pl.ds(e, 1), ch * chunk:(ch + 1) * chunk] = (token * ROW_PITCH).astype(I32)
        return carry

    lax.fori_loop(0, e_n, compact, 0)


def _select(aff3, cap):
    e_n, rows, lanes = aff3.shape
    return pl.pallas_call(
        functools.partial(_select_kernel, cap=cap),
        out_shape=jax.ShapeDtypeStruct((e_n, cap), I32),
        grid=(1,),
        in_specs=[pl.BlockSpec((e_n, rows, lanes), lambda i: (0, 0, 0))],
        out_specs=pl.BlockSpec((e_n, cap), lambda i: (0, 0)),
        scratch_shapes=[pltpu.VMEM((e_n, rows, lanes), F32)],
        compiler_params=_params(("arbitrary",)),
        name="select",
    )(aff3)


def _experts_kernel(idx_ref, idx_next_ref, tok_hbm, acc_in_hbm, wg_in, wu_in, wd_in, acc_hbm, *rest,
                    cast_weights):
    del acc_in_hbm
    if cast_weights:
        wg_ref, wu_ref, wd_ref, xbuf, xn_ref, yacc_ref, gate_ref, sem_x, sem_a, sem_s = rest
        wg_ref[...] = wg_in[...].astype(BF16)
        wu_ref[...] = wu_in[...].astype(BF16)
        wd_ref[...] = wd_in[...].astype(BF16)
    else:
        wg_ref, wu_ref, wd_ref = wg_in, wu_in, wd_in
        xbuf, xn_ref, yacc_ref, gate_ref, sem_x, sem_a, sem_s = rest
    e = pl.program_id(0)
    c = pl.program_id(1)
    f = pl.program_id(2)
    ne = pl.num_programs(0)
    nt = pl.num_programs(1)
    nf = pl.num_programs(2)
    tc = xn_ref.shape[0]
    g = e * nt + c
    slot = g % 2
    other = 1 - slot

    def start_gather(src_hbm, tiles, ids_ref, base, buf, sem, priority):
        def issue(r, carry):
            pltpu.make_async_copy(src_hbm.at[pl.ds(ids_ref[base + r], tiles), :],
                                  xbuf.at[buf, pl.ds(r * ROW_PITCH, tiles), :], sem.at[buf]).start(priority)
            return carry
        lax.fori_loop(0, tc, issue, 0, unroll=min(ISSUE_UNROLL, tc))

    def wait_gather(src_hbm, tiles, buf, sem):
        n = tc * tiles
        pltpu.make_async_copy(src_hbm.at[pl.ds(0, n), :], xbuf.at[buf, pl.ds(0, n), :], sem.at[buf]).wait()

    def wait_scatter(buf):
        n = tc * ROW_TILES
        pltpu.make_async_copy(xbuf.at[buf, pl.ds(0, n), :], acc_hbm.at[pl.ds(0, n), :], sem_s.at[buf]).wait()

    def tile_rows(r0, n, k):
        return pl.ds(r0 * ROW_PITCH + k, n, stride=ROW_PITCH)

    rb = min(EXPERT_ROWS, tc)

    @pl.when(jnp.logical_and(g == 0, f == 0))
    def _():
        start_gather(tok_hbm, ROW_PITCH, idx_ref, 0, 0, sem_x, TOK_PRIORITY)

    @pl.when(f == 0)
    def _():
        wait_gather(tok_hbm, ROW_PITCH, slot, sem_x)

        for r0 in range(0, tc, rb):
            for k in range(ROW_TILES):
                xn_ref[r0:r0 + rb, k * LANES:(k + 1) * LANES] = xbuf[slot, tile_rows(r0, rb, k), :].astype(BF16)
            aff = xbuf[slot, tile_rows(r0, rb, ROW_TILES), :]
            lane = lax.broadcasted_iota(I32, aff.shape, 1)
            gate_ref[r0:r0 + rb, :] = jnp.sum(jnp.where(lane == e, aff, 0.0), axis=-1, keepdims=True)
            yacc_ref[r0:r0 + rb, :] = jnp.zeros((rb, D_MODEL), F32)

    @pl.when(f == 1)
    def _():
        @pl.when(g > 0)
        def _():
            wait_scatter(other)

        start_gather(acc_hbm, ROW_TILES, idx_ref, c * tc, slot, sem_a, ACC_PRIORITY)

        @pl.when(c + 1 < nt)
        def _():
            start_gather(tok_hbm, ROW_PITCH, idx_ref, (c + 1) * tc, other, sem_x, TOK_PRIORITY)

        @pl.when(jnp.logical_and(c + 1 == nt, e + 1 < ne))
        def _():
            start_gather(tok_hbm, ROW_PITCH, idx_next_ref, 0, other, sem_x, TOK_PRIORITY)

    def ffn(m, carry):
        rows = pl.ds(pl.multiple_of(m * rb, rb), rb)
        xn = xn_ref[rows, :]
        hg = jnp.dot(xn, wg_ref[0], preferred_element_type=F32)
        hu = jnp.dot(xn, wu_ref[0], preferred_element_type=F32)
        hid = (hg / (1.0 + jnp.exp(-hg)) * hu).astype(BF16)
        yacc_ref[rows, :] += jnp.dot(hid, wd_ref[0], preferred_element_type=F32)
        return carry
    lax.fori_loop(0, tc // rb, ffn, 0)

    @pl.when(f == nf - 1)
    def _():
        wait_gather(acc_hbm, ROW_TILES, slot, sem_a)

        for r0 in range(0, tc, rb):
            gate = gate_ref[r0:r0 + rb, :]
            for k in range(ROW_TILES):
                dst = tile_rows(r0, rb, k)
                xbuf[slot, dst, :] = xbuf[slot, dst, :] + yacc_ref[r0:r0 + rb, k * LANES:(k + 1) * LANES] * gate

        def issue(pair, carry):
            for p in range(2):
                r = 2 * pair + p
                pltpu.make_async_copy(xbuf.at[slot, pl.ds(r * ROW_PITCH, ROW_TILES), :],
                                      acc_hbm.at[pl.ds(idx_ref[c * tc + r], ROW_TILES), :],
                                      sem_s.at[slot]).start(p)
            return carry
        lax.fori_loop(0, tc // 2, issue, 0, unroll=min(ISSUE_UNROLL, tc) // 2)

        @pl.when(jnp.logical_and(e == ne - 1, c == nt - 1))
        def _():
            wait_scatter(slot)


def _experts(idx_flat, tok, acc, wg, wu, wd, cap):
    tc = min(EXPERT_TILE, cap)
    cast_weights = wg.dtype != BF16
    tf = 256 if cast_weights else 512
    assert not cast_weights or cap == tc, "each bf16 weight block must be written by exactly one grid step"
    w_specs = [
        pl.BlockSpec((1, D_MODEL, tf), lambda e, c, f: (e, 0, f)),
        pl.BlockSpec((1, D_MODEL, tf), lambda e, c, f: (e, 0, f)),
        pl.BlockSpec((1, tf, D_MODEL), lambda e, c, f: (e, f, 0)),
    ]
    acc_shape = jax.ShapeDtypeStruct(acc.shape, F32)
    acc_spec = pl.BlockSpec(memory_space=pl.ANY)
    w_shape = jax.ShapeDtypeStruct(wg.shape, BF16)
    out = pl.pallas_call(
        functools.partial(_experts_kernel, cast_weights=cast_weights),
        out_shape=(acc_shape, w_shape, w_shape, w_shape) if cast_weights else acc_shape,
        grid=(N_EXPERTS, cap // tc, EXPERT_FF // tf),
        in_specs=[
            pl.BlockSpec((cap,), lambda e, c, f: (e,), memory_space=pltpu.SMEM),
            pl.BlockSpec((cap,), lambda e, c, f: (jnp.minimum(e + 1, N_EXPERTS - 1),), memory_space=pltpu.SMEM),
            pl.BlockSpec(memory_space=pl.ANY),
            pl.BlockSpec(memory_space=pl.ANY),
            *w_specs,
        ],
        out_specs=(acc_spec, *w_specs) if cast_weights else acc_spec,
        scratch_shapes=[
            pltpu.VMEM((2, tc * ROW_PITCH, LANES), F32),
            pltpu.VMEM((tc, D_MODEL), BF16),
            pltpu.VMEM((tc, D_MODEL), F32),
            pltpu.VMEM((tc, 1), F32),
            pltpu.SemaphoreType.DMA((2,)),
            pltpu.SemaphoreType.DMA((2,)),
            pltpu.SemaphoreType.DMA((2,)),
        ],
        input_output_aliases={3: 0},
        compiler_params=_params(("arbitrary", "arbitrary", "arbitrary")),
        name="experts",
    )(idx_flat, idx_flat, tok, acc, wg, wu, wd)
    return out if cast_weights else (out, wg, wu, wd)


def _final_kernel(x_ref, g_ref, o_ref):
    tm = o_ref.shape[0]
    x = jnp.concatenate([x_ref[pl.ds(k, tm, stride=ROW_PITCH), :] for k in range(ROW_TILES)], axis=-1)
    o_ref[...] = _rmsnorm(x, g_ref[...])


def _final_norm(rows, g):
    t = rows.shape[0] // ROW_PITCH
    tm = min(512, t)
    return pl.pallas_call(
        _final_kernel,
        out_shape=jax.ShapeDtypeStruct((t, D_MODEL), F32),
        grid=(t // tm,),
        in_specs=[pl.BlockSpec((tm * ROW_PITCH, LANES), lambda i: (i, 0)),
                  pl.BlockSpec((1, D_MODEL), lambda i: (0, 0))],
        out_specs=pl.BlockSpec((tm, D_MODEL), lambda i: (i, 0)),
        compiler_params=_params(("parallel",)),
        name="final_norm",
    )(rows, g)


def _rotary_tables(seq):
    half = HEAD_DIM // 2
    inv = ROPE_BASE ** (-jnp.arange(half, dtype=F32) / half)
    ang = jnp.arange(seq, dtype=F32)[:, None] * inv[None, :]
    cos, sin = jnp.cos(ang), jnp.sin(ang)
    return jnp.concatenate([cos, cos], axis=-1), jnp.concatenate([-sin, sin], axis=-1)


def _trunk(x, w, expert_w):
    batch, seq, _ = x.shape
    t = batch * seq
    cap = EC_CAPACITY_FACTOR * t // N_EXPERTS
    x2d = x.reshape(t, D_MODEL)
    cos_t, sin_t = _rotary_tables(seq)

    proj = _in_proj(x2d, w["norm_mix_g"], w["w_in"], cos_t, sin_t, seq)
    o_f, o_b = _retention(proj, w["decays"], batch, seq)
    acc, tok, aff = _mix_out(o_f, o_b, proj, x2d, w["ret_gn_g"], w["conv_w"], w["w_o"],
                             w["norm_ffn_g"], w["w_router_t"], seq)
    idx = _select(aff.reshape(N_EXPERTS, t // LANES, LANES), cap)
    acc, *expert_w = _experts(idx.reshape(N_EXPERTS * cap), tok, acc, *expert_w, cap)
    return _final_norm(acc, w["final_norm_g"]).reshape(batch, seq, D_MODEL), expert_w


def kernel(x_prompt, x_sample, norm_mix_g, w_in, conv_w, ret_decay_fwd, ret_decay_bwd, ret_gn_g, w_o,
           norm_ffn_g, w_router, w_gate, w_up, w_down, final_norm_g):
    w = {
        "norm_mix_g": norm_mix_g[0][None, :],
        "w_in": w_in[0].astype(BF16),
        "conv_w": conv_w[0],
        "decays": jnp.stack([ret_decay_fwd[0], ret_decay_bwd[0]]),
        "ret_gn_g": ret_gn_g[0][None, :],
        "w_o": w_o[0].astype(BF16),
        "norm_ffn_g": norm_ffn_g[0][None, :],
        "w_router_t": w_router[0].T,
        "final_norm_g": final_norm_g[None, :],
    }
    y_prompt, expert_w = _trunk(x_prompt, w, (w_gate[0], w_up[0], w_down[0]))
    y_sample, _ = _trunk(x_sample, w, expert_w)
    return (y_prompt, y_sample)
```

```python
import functools

import jax
import jax.numpy as jnp
from jax import lax
from jax.experimental import pallas as pl
from jax.experimental.pallas import tpu as pltpu

F32 = jnp.float32
BF16 = jnp.bfloat16
I32 = jnp.int32

D_MODEL = 2048
RET_WIDTH = 1024
RET_HEADS = 8
HEAD_DIM = 128
IN_COLS = 7168
COL_BLOCK = 1024
N_EXPERTS = 16
EC_CAPACITY_FACTOR = 2
EXPERT_FF = 2048
CHUNK = 256
ROPE_BASE = 10000.0
EPS = 1e-6
LANES = 128
HALO_ROWS = 16
ROW_TILES = D_MODEL // LANES
ROW_PITCH = ROW_TILES + 1
RET_CHUNKS = 4
MIX_SPLIT = 2
EXPERT_TILE = 1024
EXPERT_ROWS = 1024
DOWN_COLS = 512
ISSUE_UNROLL = 64
TOK_PRIORITY = 0
ACC_PRIORITY = 1
VMEM_LIMIT = 56 * 1024 * 1024


def _params(sem, vmem=VMEM_LIMIT):
    return pltpu.CompilerParams(dimension_semantics=sem, vmem_limit_bytes=vmem)


def _rmsnorm(x, g):
    return x * lax.rsqrt(jnp.mean(x * x, axis=-1, keepdims=True) + EPS) * g


def _split_bf16(x):
    hi = x.astype(BF16)
    lo = (x - hi.astype(F32)).astype(BF16)
    return hi, lo


def _dot3(a, b, dims):
    a_hi, a_lo = _split_bf16(a)
    b_hi, b_lo = _split_bf16(b)
    d = functools.partial(lax.dot_general, dimension_numbers=dims, preferred_element_type=F32)
    return d(a_hi, b_hi) + d(a_hi, b_lo) + d(a_lo, b_hi)


def _inproj_kernel(x_ref, g_ref, w_ref, cos_ref, sin_ref, o_ref, h_ref, acc_ref):
    j = pl.program_id(1)

    @pl.when(j == 0)
    def _():
        h_ref[...] = _rmsnorm(x_ref[...], g_ref[...]).astype(BF16)

    @pl.when(j < 2)
    def _():
        acc_ref[...] = jnp.dot(h_ref[...], w_ref[...], preferred_element_type=F32)
        scale = jnp.where(j == 1, HEAD_DIM ** -0.5, 1.0).astype(F32)
        cos = cos_ref[...]
        sin = sin_ref[...]
        for h in range(COL_BLOCK // HEAD_DIM):
            sl = slice(h * HEAD_DIM, (h + 1) * HEAD_DIM)
            t = acc_ref[:, sl]
            o_ref[:, sl] = ((t * cos + pltpu.roll(t, HEAD_DIM // 2, axis=1) * sin) * scale).astype(BF16)

    @pl.when(j >= 2)
    def _():
        o_ref[...] = jnp.dot(h_ref[...], w_ref[...], preferred_element_type=F32).astype(BF16)


def _in_proj(x2d, g, w_bf16, cos_t, sin_t, seq):
    t = x2d.shape[0]
    tm = min(1024, seq)
    blocks_per_seq = seq // tm
    return pl.pallas_call(
        _inproj_kernel,
        out_shape=jax.ShapeDtypeStruct((t, IN_COLS), BF16),
        grid=(t // tm, IN_COLS // COL_BLOCK),
        in_specs=[
            pl.BlockSpec((tm, D_MODEL), lambda i, j: (i, 0)),
            pl.BlockSpec((1, D_MODEL), lambda i, j: (0, 0)),
            pl.BlockSpec((D_MODEL, COL_BLOCK), lambda i, j: (0, j)),
            pl.BlockSpec((tm, HEAD_DIM), lambda i, j: (i % blocks_per_seq, 0)),
            pl.BlockSpec((tm, HEAD_DIM), lambda i, j: (i % blocks_per_seq, 0)),
        ],
        out_specs=pl.BlockSpec((tm, COL_BLOCK), lambda i, j: (i, j)),
        scratch_shapes=[pltpu.VMEM((tm, D_MODEL), BF16), pltpu.VMEM((tm, COL_BLOCK), F32)],
        compiler_params=_params(("parallel", "arbitrary")),
        name="in_proj",
    )(x2d, g, w_bf16, cos_t, sin_t)


def _retention_kernel(dec_ref, qf_ref, kf_ref, vf_ref, qb_ref, kb_ref, vb_ref,
                      of_ref, ob_ref,
                      sf_ref, sb_ref, dtab_ref, xif_ref, xib_ref, zf_ref, zb_ref):
    first = jnp.logical_and(pl.program_id(0) == 0, pl.program_id(1) == 0)

    @pl.when(first)
    def _():
        diff = (lax.broadcasted_iota(I32, (CHUNK, CHUNK), 0)
                - lax.broadcasted_iota(I32, (CHUNK, CHUNK), 1)).astype(F32)
        row = lax.broadcasted_iota(I32, (CHUNK, HEAD_DIM), 0).astype(F32)
        for h in range(RET_HEADS):
            lgf = -jnp.exp(dec_ref[0:1, h:h + 1])
            lgb = -jnp.exp(dec_ref[1:2, h:h + 1])
            dtab_ref[h] = jnp.where(diff >= 0, jnp.exp(lgf * jnp.maximum(diff, 0.0)),
                                    jnp.exp(lgb * jnp.maximum(-diff, 0.0)))
            xif_ref[h] = jnp.exp(lgf * (row + 1.0))
            zf_ref[h] = jnp.exp(lgf * (CHUNK - 1.0 - row))
            xib_ref[h] = jnp.exp(lgb * (CHUNK - row))
            zb_ref[h] = jnp.exp(lgb * row)

    @pl.when(pl.program_id(1) == 0)
    def _():
        sf_ref[...] = jnp.zeros_like(sf_ref)
        sb_ref[...] = jnp.zeros_like(sb_ref)

    chunks = qf_ref.shape[0] // CHUNK
    nt = (((1,), (1,)), ((), ()))
    tn = (((0,), (0,)), ((), ()))
    for h in range(RET_HEADS):
        sl = slice(h * HEAD_DIM, (h + 1) * HEAD_DIM)
        gcf = jnp.exp(-jnp.exp(dec_ref[0:1, h:h + 1]) * CHUNK)
        gcb = jnp.exp(-jnp.exp(dec_ref[1:2, h:h + 1]) * CHUNK)

        for u in range(chunks):
            rows = slice(u * CHUNK, (u + 1) * CHUNK)
            q = qf_ref[rows, sl]
            k = kf_ref[rows, sl]
            v = vf_ref[rows, sl]
            s = lax.dot_general(q, k, nt, preferred_element_type=F32) * dtab_ref[h]
            sf = sf_ref[h]
            o = jnp.dot(s.astype(BF16), v, preferred_element_type=F32)
            o = o + jnp.dot(q, sf.astype(BF16), preferred_element_type=F32) * xif_ref[h]
            of_ref[rows, sl] = o
            kz = (k.astype(F32) * zf_ref[h]).astype(BF16)
            sf_ref[h] = sf * gcf + lax.dot_general(kz, v, tn, preferred_element_type=F32)

        for u in reversed(range(chunks)):
            rows = slice(u * CHUNK, (u + 1) * CHUNK)
            q = qb_ref[rows, sl]
            v = vb_ref[rows, sl]
            sb = sb_ref[h]
            ob_ref[rows, sl] = jnp.dot(q, sb.astype(BF16), preferred_element_type=F32) * xib_ref[h]
            kz = (kb_ref[rows, sl].astype(F32) * zb_ref[h]).astype(BF16)
            sb_ref[h] = sb * gcb + lax.dot_general(kz, v, tn, preferred_element_type=F32)


def _retention(proj, decays, batch, seq):
    t = proj.shape[0]
    rows = min(RET_CHUNKS * CHUNK, seq)
    n = seq // rows
    fwd = lambda col: pl.BlockSpec((rows, COL_BLOCK), lambda b, c: (b * n + c, col))
    bwd = lambda col: pl.BlockSpec((rows, COL_BLOCK), lambda b, c: (b * n + n - 1 - c, col))
    state = pltpu.VMEM((RET_HEADS, HEAD_DIM, HEAD_DIM), F32)
    decay = pltpu.VMEM((RET_HEADS, CHUNK, CHUNK), F32)
    per_row = pltpu.VMEM((RET_HEADS, CHUNK, HEAD_DIM), F32)
    return pl.pallas_call(
        _retention_kernel,
        out_shape=(jax.ShapeDtypeStruct((t, RET_WIDTH), F32), jax.ShapeDtypeStruct((t, RET_WIDTH), F32)),
        grid=(batch, n),
        in_specs=[pl.BlockSpec((2, RET_HEADS), lambda b, c: (0, 0)),
                  fwd(0), fwd(1), fwd(2), bwd(0), bwd(1), bwd(2)],
        out_specs=(pl.BlockSpec((rows, RET_WIDTH), lambda b, c: (b * n + c, 0)),
                   pl.BlockSpec((rows, RET_WIDTH), lambda b, c: (b * n + n - 1 - c, 0))),
        scratch_shapes=[state, state, decay, per_row, per_row, per_row, per_row],
        compiler_params=_params(("arbitrary", "arbitrary")),
        name="retention",
    )(decays, proj, proj, proj, proj, proj, proj)


def _mixout_kernel(of_ref, ob_ref, g_ref, cb_ref, cc_ref, ch_ref, ccp_ref, chp_ref, ccn_ref, chn_ref,
                   x_ref, gn_ref, cw_ref, wo_ref, g2_ref, wr_ref,
                   acc_ref, tok_ref, aff_ref, *, blocks_per_seq):
    i = pl.program_id(0)
    tm = x_ref.shape[0]

    gn = gn_ref[...]
    z = cc_ref[...].astype(F32) * ch_ref[...].astype(F32)
    pos = i % blocks_per_seq
    keep_prev = jnp.where(pos == 0, 0.0, 1.0).astype(F32)
    keep_next = jnp.where(pos == blocks_per_seq - 1, 0.0, 1.0).astype(F32)
    last = HALO_ROWS - 1
    halo_prev = (ccp_ref[...].astype(F32) * chp_ref[...].astype(F32))[last:, :] * keep_prev
    halo_next = (ccn_ref[...].astype(F32) * chn_ref[...].astype(F32))[0:1, :] * keep_next
    row = lax.broadcasted_iota(I32, z.shape, 0)
    z_prev = jnp.where(row == 0, halo_prev, pltpu.roll(z, 1, axis=0))
    z_next = jnp.where(row == tm - 1, halo_next, pltpu.roll(z, tm - 1, axis=0))
    cw = cw_ref[...]
    zc = z_prev * cw[0:1, :] + z * cw[1:2, :] + z_next * cw[2:3, :]
    eye = (lax.broadcasted_iota(I32, (N_EXPERTS, LANES), 0)
           == lax.broadcasted_iota(I32, (N_EXPERTS, LANES), 1)).astype(BF16)
    tn = (((0,), (0,)), ((), ()))
    tr = lambda a: lax.dot_general(a, eye, tn, preferred_element_type=F32)

    rs = tm // MIX_SPLIT
    for s in range(MIX_SPLIT):
        rows = slice(s * rs, (s + 1) * rs)
        o = of_ref[rows, :] + ob_ref[rows, :]
        gate_in = g_ref[rows, :].astype(F32)
        parts = []
        for h in range(RET_HEADS):
            sl = slice(h * HEAD_DIM, (h + 1) * HEAD_DIM)
            oh = o[:, sl]
            mu = jnp.mean(oh, axis=-1, keepdims=True)
            d = oh - mu
            var = jnp.mean(d * d, axis=-1, keepdims=True)
            gh = gate_in[:, sl]
            swish = gh / (1.0 + jnp.exp(-gh))
            parts.append((swish * (d * lax.rsqrt(var + EPS) * gn[:, sl])).astype(BF16))
        ret = jnp.concatenate(parts, axis=-1)
        conv = cb_ref[rows, :].astype(F32) * zc[rows, :]

        y = jnp.dot(ret, wo_ref[0:RET_WIDTH, :], preferred_element_type=F32)
        y = y + jnp.dot(conv.astype(BF16), wo_ref[RET_WIDTH:, :], preferred_element_type=F32)
        x1 = x_ref[rows, :] + y
        h2 = _rmsnorm(x1, g2_ref[...])
        for k in range(ROW_TILES):
            cols = slice(k * LANES, (k + 1) * LANES)
            dst = pl.ds(s * rs * ROW_PITCH + k, rs, stride=ROW_PITCH)
            acc_ref[dst, :] = x1[:, cols]
            tok_ref[dst, :] = h2[:, cols]
        last_tile = pl.ds(s * rs * ROW_PITCH + ROW_TILES, rs, stride=ROW_PITCH)
        acc_ref[last_tile, :] = jnp.zeros((rs, LANES), F32)
        logits = _dot3(wr_ref[...], h2, (((1,), (1,)), ((), ())))
        ex = jnp.exp(logits - jnp.max(logits, axis=0, keepdims=True))
        aff = ex / jnp.sum(ex, axis=0, keepdims=True)
        aff_ref[:, rows] = aff

        a1 = aff.astype(BF16)
        r1 = aff - a1.astype(F32)
        a2 = r1.astype(BF16)
        a3 = (r1 - a2.astype(F32)).astype(BF16)
        tok_ref[last_tile, :] = (tr(a1) + tr(a2)) + tr(a3)


def _mix_out(o_f, o_b, proj, x2d, gn_g, conv_w, wo_bf16, g2, wr_t, seq):
    t = x2d.shape[0]
    tm = min(256, seq)
    bps = seq // tm
    r8 = tm // HALO_ROWS
    n8 = t // HALO_ROWS
    col = lambda c: pl.BlockSpec((tm, COL_BLOCK), lambda i: (i, c))
    prev = lambda c: pl.BlockSpec((HALO_ROWS, COL_BLOCK), lambda i: (jnp.maximum(i * r8 - 1, 0), c))
    nxt = lambda c: pl.BlockSpec((HALO_ROWS, COL_BLOCK), lambda i: (jnp.minimum((i + 1) * r8, n8 - 1), c))
    whole = lambda shape: pl.BlockSpec(shape, lambda i: (0,) * len(shape))
    return pl.pallas_call(
        functools.partial(_mixout_kernel, blocks_per_seq=bps),
        out_shape=(jax.ShapeDtypeStruct((t * ROW_PITCH, LANES), F32),
                   jax.ShapeDtypeStruct((t * ROW_PITCH, LANES), F32),
                   jax.ShapeDtypeStruct((N_EXPERTS, t), F32)),
        grid=(t // tm,),
        in_specs=[
            pl.BlockSpec((tm, RET_WIDTH), lambda i: (i, 0)),
            pl.BlockSpec((tm, RET_WIDTH), lambda i: (i, 0)),
            col(3), col(4), col(5), col(6), prev(5), prev(6), nxt(5), nxt(6),
            pl.BlockSpec((tm, D_MODEL), lambda i: (i, 0)),
            whole((1, RET_WIDTH)), whole((3, COL_BLOCK)), whole((D_MODEL, D_MODEL)),
            whole((1, D_MODEL)), whole((N_EXPERTS, D_MODEL)),
        ],
        out_specs=(pl.BlockSpec((tm * ROW_PITCH, LANES), lambda i: (i, 0)),
                   pl.BlockSpec((tm * ROW_PITCH, LANES), lambda i: (i, 0)),
                   pl.BlockSpec((N_EXPERTS, tm), lambda i: (0, i))),
        compiler_params=_params(("parallel",)),
        name="mix_out",
    )(o_f, o_b, proj, proj, proj, proj, proj, proj, proj, proj, x2d, gn_g, conv_w, wo_bf16, g2, wr_t)


def _select_kernel(aff_ref, idx_ref, sel_ref, *, cap):
    e_n, rows, _ = aff_ref.shape
    bits = lax.bitcast_convert_type(aff_ref[...], I32)

    def count(mask):
        c = jnp.sum(jnp.where(mask, 1.0, 0.0), axis=1, keepdims=True)
        return jnp.sum(c, axis=2, keepdims=True)

    thr = jnp.zeros((e_n, 1, 1), I32)
    for b in range(30, -1, -1):
        cand = thr | (1 << b)
        thr = jnp.where(count(bits >= cand) >= cap, cand, thr)

    gt = bits > thr
    eq = bits == thr
    need = cap - count(gt)

    li = lax.broadcasted_iota(I32, (LANES, LANES), 0)
    lj = lax.broadcasted_iota(I32, (LANES, LANES), 1)
    upper = (li <= lj).astype(BF16)
    ri = lax.broadcasted_iota(I32, (rows, rows), 0)
    rj = lax.broadcasted_iota(I32, (rows, rows), 1)
    lower = (rj < ri).astype(BF16)
    eqf = jnp.where(eq, 1.0, 0.0)
    for e in range(e_n):
        incl = jnp.dot(eqf[e].astype(BF16), upper, preferred_element_type=F32)
        row_tot = jnp.broadcast_to(incl[:, LANES - 1:LANES], (rows, LANES)).astype(BF16)
        row_off = jnp.dot(lower, row_tot, preferred_element_type=F32)
        rank = row_off + incl - eqf[e]
        take = jnp.logical_or(gt[e], jnp.logical_and(eq[e], rank < need[e]))
        sel_ref[e] = jnp.where(take, 1.0, 0.0)

    chunk = min(512, cap)
    lower_incl = (li >= lj).astype(BF16)
    slot0 = lax.broadcasted_iota(I32, (1, chunk), 1).astype(F32)
    rowid = lax.broadcasted_iota(I32, (rows, chunk), 0).astype(F32)

    def compact(e, carry):
        s = sel_ref[e].astype(BF16)
        q_t = lax.dot_general(lower_incl, s, (((1,), (1,)), ((), ())),
                              preferred_element_type=F32)
        q = jnp.dot(s, upper, preferred_element_type=F32)
        row_tot = q[:, LANES - 1:LANES]
        row_cum = jnp.dot(lower, jnp.broadcast_to(row_tot, (rows, LANES)).astype(BF16),
                          preferred_element_type=F32)[:, 0:1] + row_tot
        q_t = q_t.astype(BF16)
        for ch in range(cap // chunk):
            c = slot0 + float(ch * chunk)
            before = row_cum <= c
            r_c = jnp.sum(jnp.where(before, 1.0, 0.0), axis=0, keepdims=True)
            k = c - jnp.sum(jnp.where(before, row_tot, 0.0), axis=0, keepdims=True)
            onehot = jnp.where(rowid == r_c, 1.0, 0.0).astype(BF16)
            q_row = jnp.dot(q_t, onehot, preferred_element_type=F32)
            lane_c = jnp.sum(jnp.where(q_row <= k, 1.0, 0.0), axis=0, keepdims=True)
            token = r_c * LANES + lane_c
            idx_ref[pl.ds(e, 1), ch * chunk:(ch + 1) * chunk] = (token * ROW_PITCH).astype(I32)
        return carry

    lax.fori_loop(0, e_n, compact, 0)


def _select(aff3, cap):
    e_n, rows, lanes = aff3.shape
    return pl.pallas_call(
        functools.partial(_select_kernel, cap=cap),
        out_shape=jax.ShapeDtypeStruct((e_n, cap), I32),
        grid=(1,),
        in_specs=[pl.BlockSpec((e_n, rows, lanes), lambda i: (0, 0, 0))],
        out_specs=pl.BlockSpec((e_n, cap), lambda i: (0, 0)),
        scratch_shapes=[pltpu.VMEM((e_n, rows, lanes), F32)],
        compiler_params=_params(("arbitrary",)),
        name="select",
    )(aff3)


def _experts_kernel(idx_ref, idx_next_ref, tok_hbm, acc_in_hbm, wg_in, wu_in, wd_in, acc_hbm, *rest,
                    cast_weights):
    del acc_in_hbm
    if cast_weights:
        wg_ref, wu_ref, wd_ref, xbuf, xn_ref, yacc_ref, gate_ref, sem_x, sem_a, sem_s = rest
        wg_ref[...] = wg_in[...].astype(BF16)
        wu_ref[...] = wu_in[...].astype(BF16)
        wd_ref[...] = wd_in[...].astype(BF16)
    else:
        wg_ref, wu_ref, wd_ref = wg_in, wu_in, wd_in
        xbuf, xn_ref, yacc_ref, gate_ref, sem_x, sem_a, sem_s = rest
    e = pl.program_id(0)
    c = pl.program_id(1)
    f = pl.program_id(2)
    ne = pl.num_programs(0)
    nt = pl.num_programs(1)
    nf = pl.num_programs(2)
    tc = xn_ref.shape[0]
    g = e * nt + c
    slot = g % 2
    other = 1 - slot

    def start_gather(src_hbm, tiles, ids_ref, base, buf, sem, priority):
        def issue(r, carry):
            pltpu.make_async_copy(src_hbm.at[pl.ds(ids_ref[base + r], tiles), :],
                                  xbuf.at[buf, pl.ds(r * ROW_PITCH, tiles), :], sem.at[buf]).start(priority)
            return carry
        lax.fori_loop(0, tc, issue, 0, unroll=min(ISSUE_UNROLL, tc))

    def wait_gather(src_hbm, tiles, buf, sem):
        n = tc * tiles
        pltpu.make_async_copy(src_hbm.at[pl.ds(0, n), :], xbuf.at[buf, pl.ds(0, n), :], sem.at[buf]).wait()

    def wait_scatter(buf):
        n = tc * ROW_TILES
        pltpu.make_async_copy(xbuf.at[buf, pl.ds(0, n), :], acc_hbm.at[pl.ds(0, n), :], sem_s.at[buf]).wait()

    def tile_rows(r0, n, k):
        return pl.ds(r0 * ROW_PITCH + k, n, stride=ROW_PITCH)

    rb = min(EXPERT_ROWS, tc)

    @pl.when(jnp.logical_and(g == 0, f == 0))
    def _():
        start_gather(tok_hbm, ROW_PITCH, idx_ref, 0, 0, sem_x, TOK_PRIORITY)

    @pl.when(f == 0)
    def _():
        wait_gather(tok_hbm, ROW_PITCH, slot, sem_x)

        for r0 in range(0, tc, rb):
            for k in range(ROW_TILES):
                xn_ref[r0:r0 + rb, k * LANES:(k + 1) * LANES] = xbuf[slot, tile_rows(r0, rb, k), :].astype(BF16)
            aff = xbuf[slot, tile_rows(r0, rb, ROW_TILES), :]
            lane = lax.broadcasted_iota(I32, aff.shape, 1)
            gate_ref[r0:r0 + rb, :] = jnp.sum(jnp.where(lane == e, aff, 0.0), axis=-1, keepdims=True)
            yacc_ref[r0:r0 + rb, :] = jnp.zeros((rb, D_MODEL), F32)

    @pl.when(f == 1)
    def _():
        @pl.when(g > 0)
        def _():
            wait_scatter(other)

        start_gather(acc_hbm, ROW_TILES, idx_ref, c * tc, slot, sem_a, ACC_PRIORITY)

        @pl.when(c + 1 < nt)
        def _():
            start_gather(tok_hbm, ROW_PITCH, idx_ref, (c + 1) * tc, other, sem_x, TOK_PRIORITY)

        @pl.when(jnp.logical_and(c + 1 == nt, e + 1 < ne))
        def _():
            start_gather(tok_hbm, ROW_PITCH, idx_next_ref, 0, other, sem_x, TOK_PRIORITY)

    def ffn(m, carry):
        rows = pl.ds(pl.multiple_of(m * rb, rb), rb)
        xn = xn_ref[rows, :]
        hg = jnp.dot(xn, wg_ref[0], preferred_element_type=F32)
        hu = jnp.dot(xn, wu_ref[0], preferred_element_type=F32)
        hid = (hg / (1.0 + jnp.exp(-hg)) * hu).astype(BF16)
        for n0 in range(0, D_MODEL, DOWN_COLS):
            cols = slice(n0, n0 + DOWN_COLS)
            yacc_ref[rows, cols] += jnp.dot(hid, wd_ref[0, :, cols], preferred_element_type=F32)
        return carry
    lax.fori_loop(0, tc // rb, ffn, 0)

    @pl.when(f == nf - 1)
    def _():
        wait_gather(acc_hbm, ROW_TILES, slot, sem_a)

        for r0 in range(0, tc, rb):
            gate = gate_ref[r0:r0 + rb, :]
            for k in range(ROW_TILES):
                dst = tile_rows(r0, rb, k)
                xbuf[slot, dst, :] = xbuf[slot, dst, :] + yacc_ref[r0:r0 + rb, k * LANES:(k + 1) * LANES] * gate

        def issue(pair, carry):
            for p in range(2):
                r = 2 * pair + p
                pltpu.make_async_copy(xbuf.at[slot, pl.ds(r * ROW_PITCH, ROW_TILES), :],
                                      acc_hbm.at[pl.ds(idx_ref[c * tc + r], ROW_TILES), :],
                                      sem_s.at[slot]).start(p)
            return carry
        lax.fori_loop(0, tc // 2, issue, 0, unroll=min(ISSUE_UNROLL, tc) // 2)

        @pl.when(jnp.logical_and(e == ne - 1, c == nt - 1))
        def _():
            wait_scatter(slot)


def _experts(idx_flat, tok, acc, wg, wu, wd, cap):
    tc = min(EXPERT_TILE, cap)
    cast_weights = wg.dtype != BF16
    tf = 256 if cast_weights else 512
    assert not cast_weights or cap == tc, "each bf16 weight block must be written by exactly one grid step"
    w_specs = [
        pl.BlockSpec((1, D_MODEL, tf), lambda e, c, f: (e, 0, f)),
        pl.BlockSpec((1, D_MODEL, tf), lambda e, c, f: (e, 0, f)),
        pl.BlockSpec((1, tf, D_MODEL), lambda e, c, f: (e, f, 0)),
    ]
    acc_shape = jax.ShapeDtypeStruct(acc.shape, F32)
    acc_spec = pl.BlockSpec(memory_space=pl.ANY)
    w_shape = jax.ShapeDtypeStruct(wg.shape, BF16)
    out = pl.pallas_call(
        functools.partial(_experts_kernel, cast_weights=cast_weights),
        out_shape=(acc_shape, w_shape, w_shape, w_shape) if cast_weights else acc_shape,
        grid=(N_EXPERTS, cap // tc, EXPERT_FF // tf),
        in_specs=[
            pl.BlockSpec((cap,), lambda e, c, f: (e,), memory_space=pltpu.SMEM),
            pl.BlockSpec((cap,), lambda e, c, f: (jnp.minimum(e + 1, N_EXPERTS - 1),), memory_space=pltpu.SMEM),
            pl.BlockSpec(memory_space=pl.ANY),
            pl.BlockSpec(memory_space=pl.ANY),
            *w_specs,
        ],
        out_specs=(acc_spec, *w_specs) if cast_weights else acc_spec,
        scratch_shapes=[
            pltpu.VMEM((2, tc * ROW_PITCH, LANES), F32),
            pltpu.VMEM((tc, D_MODEL), BF16),
            pltpu.VMEM((tc, D_MODEL), F32),
            pltpu.VMEM((tc, 1), F32),
            pltpu.SemaphoreType.DMA((2,)),
            pltpu.SemaphoreType.DMA((2,)),
            pltpu.SemaphoreType.DMA((2,)),
        ],
        input_output_aliases={3: 0},
        compiler_params=_params(("arbitrary", "arbitrary", "arbitrary")),
        name="experts",
    )(idx_flat, idx_flat, tok, acc, wg, wu, wd)
    return out if cast_weights else (out, wg, wu, wd)


def _final_kernel(x_ref, g_ref, o_ref):
    tm = o_ref.shape[0]
    x = jnp.concatenate([x_ref[pl.ds(k, tm, stride=ROW_PITCH), :] for k in range(ROW_TILES)], axis=-1)
    o_ref[...] = _rmsnorm(x, g_ref[...])


def _final_norm(rows, g):
    t = rows.shape[0] // ROW_PITCH
    tm = min(512, t)
    return pl.pallas_call(
        _final_kernel,
        out_shape=jax.ShapeDtypeStruct((t, D_MODEL), F32),
        grid=(t // tm,),
        in_specs=[pl.BlockSpec((tm * ROW_PITCH, LANES), lambda i: (i, 0)),
                  pl.BlockSpec((1, D_MODEL), lambda i: (0, 0))],
        out_specs=pl.BlockSpec((tm, D_MODEL), lambda i: (i, 0)),
        compiler_params=_params(("parallel",)),
        name="final_norm",
    )(rows, g)


def _rotary_tables(seq):
    half = HEAD_DIM // 2
    inv = ROPE_BASE ** (-jnp.arange(half, dtype=F32) / half)
    ang = jnp.arange(seq, dtype=F32)[:, None] * inv[None, :]
    cos, sin = jnp.cos(ang), jnp.sin(ang)
    return jnp.concatenate([cos, cos], axis=-1), jnp.concatenate([-sin, sin], axis=-1)


def _trunk(x, w, expert_w):
    batch, seq, _ = x.shape
    t = batch * seq
    cap = EC_CAPACITY_FACTOR * t // N_EXPERTS
    x2d = x.reshape(t, D_MODEL)
    cos_t, sin_t = _rotary_tables(seq)

    proj = _in_proj(x2d, w["norm_mix_g"], w["w_in"], cos_t, sin_t, seq)
    o_f, o_b = _retention(proj, w["decays"], batch, seq)
    acc, tok, aff = _mix_out(o_f, o_b, proj, x2d, w["ret_gn_g"], w["conv_w"], w["w_o"],
                             w["norm_ffn_g"], w["w_router_t"], seq)
    idx = _select(aff.reshape(N_EXPERTS, t // LANES, LANES), cap)
    acc, *expert_w = _experts(idx.reshape(N_EXPERTS * cap), tok, acc, *expert_w, cap)
    return _final_norm(acc, w["final_norm_g"]).reshape(batch, seq, D_MODEL), expert_w


def kernel(x_prompt, x_sample, norm_mix_g, w_in, conv_w, ret_decay_fwd, ret_decay_bwd, ret_gn_g, w_o,
           norm_ffn_g, w_router, w_gate, w_up, w_down, final_norm_g):
    w = {
        "norm_mix_g": norm_mix_g[0][None, :],
        "w_in": w_in[0].astype(BF16),
        "conv_w": conv_w[0],
        "decays": jnp.stack([ret_decay_fwd[0], ret_decay_bwd[0]]),
        "ret_gn_g": ret_gn_g[0][None, :],
        "w_o": w_o[0].astype(BF16),
        "norm_ffn_g": norm_ffn_g[0][None, :],
        "w_router_t": w_router[0].T,
        "final_norm_g": final_norm_g[None, :],
    }
    y_prompt, expert_w = _trunk(x_prompt, w, (w_gate[0], w_up[0], w_down[0]))
    y_sample, _ = _trunk(x_sample, w, expert_w)
    return (y_prompt, y_sample)
```

```python
import functools

import jax
import jax.numpy as jnp
from jax import lax
from jax.experimental import pallas as pl
from jax.experimental.pallas import tpu as pltpu

F32 = jnp.float32
BF16 = jnp.bfloat16
I32 = jnp.int32

D_MODEL = 2048
RET_WIDTH = 1024
RET_HEADS = 8
HEAD_DIM = 128
IN_COLS = 7168
COL_BLOCK = 1024
N_EXPERTS = 16
EC_CAPACITY_FACTOR = 2
EXPERT_FF = 2048
CHUNK = 256
ROPE_BASE = 10000.0
EPS = 1e-6
LANES = 128
HALO_ROWS = 16
ROW_TILES = D_MODEL // LANES
ROW_PITCH = ROW_TILES + 1
RET_CHUNKS = 4
MIX_SPLIT = 2
EXPERT_TILE = 1024
EXPERT_ROWS = 1024
DOWN_COLS = 512
ISSUE_UNROLL = 64
TOK_PRIORITY = 0
ACC_PRIORITY = 1
VMEM_LIMIT = 56 * 1024 * 1024


def _params(sem, vmem=VMEM_LIMIT):
    return pltpu.CompilerParams(dimension_semantics=sem, vmem_limit_bytes=vmem)


def _rmsnorm(x, g):
    return x * lax.rsqrt(jnp.mean(x * x, axis=-1, keepdims=True) + EPS) * g


def _split_bf16(x):
    hi = x.astype(BF16)
    lo = (x - hi.astype(F32)).astype(BF16)
    return hi, lo


def _inproj_kernel(x_ref, g_ref, w_ref, cos_ref, sin_ref, o_ref, h_ref, acc_ref):
    j = pl.program_id(1)

    @pl.when(j == 0)
    def _():
        h_ref[...] = _rmsnorm(x_ref[...], g_ref[...]).astype(BF16)

    @pl.when(j < 2)
    def _():
        acc_ref[...] = jnp.dot(h_ref[...], w_ref[...], preferred_element_type=F32)
        scale = jnp.where(j == 1, HEAD_DIM ** -0.5, 1.0).astype(F32)
        cos = cos_ref[...]
        sin = sin_ref[...]
        for h in range(COL_BLOCK // HEAD_DIM):
            sl = slice(h * HEAD_DIM, (h + 1) * HEAD_DIM)
            t = acc_ref[:, sl]
            o_ref[:, sl] = ((t * cos + pltpu.roll(t, HEAD_DIM // 2, axis=1) * sin) * scale).astype(BF16)

    @pl.when(j >= 2)
    def _():
        o_ref[...] = jnp.dot(h_ref[...], w_ref[...], preferred_element_type=F32).astype(BF16)


def _in_proj(x2d, g, w_bf16, cos_t, sin_t, seq):
    t = x2d.shape[0]
    tm = min(1024, seq)
    blocks_per_seq = seq // tm
    return pl.pallas_call(
        _inproj_kernel,
        out_shape=jax.ShapeDtypeStruct((t, IN_COLS), BF16),
        grid=(t // tm, IN_COLS // COL_BLOCK),
        in_specs=[
            pl.BlockSpec((tm, D_MODEL), lambda i, j: (i, 0)),
            pl.BlockSpec((1, D_MODEL), lambda i, j: (0, 0)),
            pl.BlockSpec((D_MODEL, COL_BLOCK), lambda i, j: (0, j)),
            pl.BlockSpec((tm, HEAD_DIM), lambda i, j: (i % blocks_per_seq, 0)),
            pl.BlockSpec((tm, HEAD_DIM), lambda i, j: (i % blocks_per_seq, 0)),
        ],
        out_specs=pl.BlockSpec((tm, COL_BLOCK), lambda i, j: (i, j)),
        scratch_shapes=[pltpu.VMEM((tm, D_MODEL), BF16), pltpu.VMEM((tm, COL_BLOCK), F32)],
        compiler_params=_params(("parallel", "arbitrary")),
        name="in_proj",
    )(x2d, g, w_bf16, cos_t, sin_t)


def _retention_kernel(dec_ref, qf_ref, kf_ref, vf_ref, qb_ref, kb_ref, vb_ref,
                      of_ref, ob_ref,
                      sf_ref, sb_ref, dtab_ref, xif_ref, xib_ref, zf_ref, zb_ref):
    first = jnp.logical_and(pl.program_id(0) == 0, pl.program_id(1) == 0)

    @pl.when(first)
    def _():
        diff = (lax.broadcasted_iota(I32, (CHUNK, CHUNK), 0)
                - lax.broadcasted_iota(I32, (CHUNK, CHUNK), 1)).astype(F32)
        row = lax.broadcasted_iota(I32, (CHUNK, HEAD_DIM), 0).astype(F32)
        for h in range(RET_HEADS):
            lgf = -jnp.exp(dec_ref[0:1, h:h + 1])
            lgb = -jnp.exp(dec_ref[1:2, h:h + 1])
            dtab_ref[h] = jnp.where(diff >= 0, jnp.exp(lgf * jnp.maximum(diff, 0.0)),
                                    jnp.exp(lgb * jnp.maximum(-diff, 0.0)))
            xif_ref[h] = jnp.exp(lgf * (row + 1.0))
            zf_ref[h] = jnp.exp(lgf * (CHUNK - 1.0 - row))
            xib_ref[h] = jnp.exp(lgb * (CHUNK - row))
            zb_ref[h] = jnp.exp(lgb * row)

    @pl.when(pl.program_id(1) == 0)
    def _():
        sf_ref[...] = jnp.zeros_like(sf_ref)
        sb_ref[...] = jnp.zeros_like(sb_ref)

    chunks = qf_ref.shape[0] // CHUNK
    nt = (((1,), (1,)), ((), ()))
    tn = (((0,), (0,)), ((), ()))
    for h in range(RET_HEADS):
        sl = slice(h * HEAD_DIM, (h + 1) * HEAD_DIM)
        gcf = jnp.exp(-jnp.exp(dec_ref[0:1, h:h + 1]) * CHUNK)
        gcb = jnp.exp(-jnp.exp(dec_ref[1:2, h:h + 1]) * CHUNK)

        for u in range(chunks):
            rows = slice(u * CHUNK, (u + 1) * CHUNK)
            q = qf_ref[rows, sl]
            k = kf_ref[rows, sl]
            v = vf_ref[rows, sl]
            s = lax.dot_general(q, k, nt, preferred_element_type=F32) * dtab_ref[h]
            sf = sf_ref[h]
            o = jnp.dot(s.astype(BF16), v, preferred_element_type=F32)
            o = o + jnp.dot(q, sf.astype(BF16), preferred_element_type=F32) * xif_ref[h]
            of_ref[rows, sl] = o
            kz = (k.astype(F32) * zf_ref[h]).astype(BF16)
            sf_ref[h] = sf * gcf + lax.dot_general(kz, v, tn, preferred_element_type=F32)

        for u in reversed(range(chunks)):
            rows = slice(u * CHUNK, (u + 1) * CHUNK)
            q = qb_ref[rows, sl]
            v = vb_ref[rows, sl]
            sb = sb_ref[h]
            ob_ref[rows, sl] = jnp.dot(q, sb.astype(BF16), preferred_element_type=F32) * xib_ref[h]
            kz = (kb_ref[rows, sl].astype(F32) * zb_ref[h]).astype(BF16)
            sb_ref[h] = sb * gcb + lax.dot_general(kz, v, tn, preferred_element_type=F32)


def _retention(proj, decays, batch, seq):
    t = proj.shape[0]
    rows = min(RET_CHUNKS * CHUNK, seq)
    n = seq // rows
    fwd = lambda col: pl.BlockSpec((rows, COL_BLOCK), lambda b, c: (b * n + c, col))
    bwd = lambda col: pl.BlockSpec((rows, COL_BLOCK), lambda b, c: (b * n + n - 1 - c, col))
    state = pltpu.VMEM((RET_HEADS, HEAD_DIM, HEAD_DIM), F32)
    decay = pltpu.VMEM((RET_HEADS, CHUNK, CHUNK), F32)
    per_row = pltpu.VMEM((RET_HEADS, CHUNK, HEAD_DIM), F32)
    return pl.pallas_call(
        _retention_kernel,
        out_shape=(jax.ShapeDtypeStruct((t, RET_WIDTH), F32), jax.ShapeDtypeStruct((t, RET_WIDTH), F32)),
        grid=(batch, n),
        in_specs=[pl.BlockSpec((2, RET_HEADS), lambda b, c: (0, 0)),
                  fwd(0), fwd(1), fwd(2), bwd(0), bwd(1), bwd(2)],
        out_specs=(pl.BlockSpec((rows, RET_WIDTH), lambda b, c: (b * n + c, 0)),
                   pl.BlockSpec((rows, RET_WIDTH), lambda b, c: (b * n + n - 1 - c, 0))),
        scratch_shapes=[state, state, decay, per_row, per_row, per_row, per_row],
        compiler_params=_params(("arbitrary", "arbitrary")),
        name="retention",
    )(decays, proj, proj, proj, proj, proj, proj)


def _mixout_kernel(of_ref, ob_ref, g_ref, cb_ref, cc_ref, ch_ref, ccp_ref, chp_ref, ccn_ref, chn_ref,
                   x_ref, gn_ref, cw_ref, wo_ref, g2_ref, wr_hi_ref, wr_lo_ref,
                   acc_ref, tok_ref, aff_ref, *, blocks_per_seq):
    i = pl.program_id(0)
    tm = x_ref.shape[0]

    gn = gn_ref[...]
    z = cc_ref[...].astype(F32) * ch_ref[...].astype(F32)
    pos = i % blocks_per_seq
    keep_prev = jnp.where(pos == 0, 0.0, 1.0).astype(F32)
    keep_next = jnp.where(pos == blocks_per_seq - 1, 0.0, 1.0).astype(F32)
    last = HALO_ROWS - 1
    halo_prev = (ccp_ref[...].astype(F32) * chp_ref[...].astype(F32))[last:, :] * keep_prev
    halo_next = (ccn_ref[...].astype(F32) * chn_ref[...].astype(F32))[0:1, :] * keep_next
    row = lax.broadcasted_iota(I32, z.shape, 0)
    z_prev = jnp.where(row == 0, halo_prev, pltpu.roll(z, 1, axis=0))
    z_next = jnp.where(row == tm - 1, halo_next, pltpu.roll(z, tm - 1, axis=0))
    cw = cw_ref[...]
    zc = z_prev * cw[0:1, :] + z * cw[1:2, :] + z_next * cw[2:3, :]
    wr_hi = wr_hi_ref[...]
    wr_lo = wr_lo_ref[...]

    rs = tm // MIX_SPLIT
    for s in range(MIX_SPLIT):
        rows = slice(s * rs, (s + 1) * rs)
        o = of_ref[rows, :] + ob_ref[rows, :]
        gate_in = g_ref[rows, :].astype(F32)
        parts = []
        for h in range(RET_HEADS):
            sl = slice(h * HEAD_DIM, (h + 1) * HEAD_DIM)
            oh = o[:, sl]
            mu = jnp.mean(oh, axis=-1, keepdims=True)
            d = oh - mu
            var = jnp.mean(d * d, axis=-1, keepdims=True)
            gh = gate_in[:, sl]
            swish = gh / (1.0 + jnp.exp(-gh))
            parts.append((swish * (d * lax.rsqrt(var + EPS) * gn[:, sl])).astype(BF16))
        ret = jnp.concatenate(parts, axis=-1)
        conv = cb_ref[rows, :].astype(F32) * zc[rows, :]

        y = jnp.dot(ret, wo_ref[0:RET_WIDTH, :], preferred_element_type=F32)
        y = y + jnp.dot(conv.astype(BF16), wo_ref[RET_WIDTH:, :], preferred_element_type=F32)
        x1 = x_ref[rows, :] + y
        h2 = _rmsnorm(x1, g2_ref[...])
        for k in range(ROW_TILES):
            cols = slice(k * LANES, (k + 1) * LANES)
            dst = pl.ds(s * rs * ROW_PITCH + k, rs, stride=ROW_PITCH)
            acc_ref[dst, :] = x1[:, cols]
            tok_ref[dst, :] = h2[:, cols]
        last_tile = pl.ds(s * rs * ROW_PITCH + ROW_TILES, rs, stride=ROW_PITCH)
        acc_ref[last_tile, :] = jnp.zeros((rs, LANES), F32)
        h_hi, h_lo = _split_bf16(h2)
        d = functools.partial(jnp.dot, preferred_element_type=F32)
        logits = d(h_hi, wr_hi) + d(h_hi, wr_lo) + d(h_lo, wr_hi)
        lane = lax.broadcasted_iota(I32, logits.shape, 1)
        logits = jnp.where(lane < N_EXPERTS, logits, -jnp.inf)
        ex = jnp.exp(logits - jnp.max(logits, axis=-1, keepdims=True))
        aff = ex / jnp.sum(ex, axis=-1, keepdims=True)
        tok_ref[last_tile, :] = aff
        aff_ref[rows, :] = aff[:, 0:N_EXPERTS]


def _mix_out(o_f, o_b, proj, x2d, gn_g, conv_w, wo_bf16, g2, wr_hi, wr_lo, seq):
    t = x2d.shape[0]
    tm = min(256, seq)
    bps = seq // tm
    r8 = tm // HALO_ROWS
    n8 = t // HALO_ROWS
    col = lambda c: pl.BlockSpec((tm, COL_BLOCK), lambda i: (i, c))
    prev = lambda c: pl.BlockSpec((HALO_ROWS, COL_BLOCK), lambda i: (jnp.maximum(i * r8 - 1, 0), c))
    nxt = lambda c: pl.BlockSpec((HALO_ROWS, COL_BLOCK), lambda i: (jnp.minimum((i + 1) * r8, n8 - 1), c))
    whole = lambda shape: pl.BlockSpec(shape, lambda i: (0,) * len(shape))
    return pl.pallas_call(
        functools.partial(_mixout_kernel, blocks_per_seq=bps),
        out_shape=(jax.ShapeDtypeStruct((t * ROW_PITCH, LANES), F32),
                   jax.ShapeDtypeStruct((t * ROW_PITCH, LANES), F32),
                   jax.ShapeDtypeStruct((t, N_EXPERTS), F32)),
        grid=(t // tm,),
        in_specs=[
            pl.BlockSpec((tm, RET_WIDTH), lambda i: (i, 0)),
            pl.BlockSpec((tm, RET_WIDTH), lambda i: (i, 0)),
            col(3), col(4), col(5), col(6), prev(5), prev(6), nxt(5), nxt(6),
            pl.BlockSpec((tm, D_MODEL), lambda i: (i, 0)),
            whole((1, RET_WIDTH)), whole((3, COL_BLOCK)), whole((D_MODEL, D_MODEL)),
            whole((1, D_MODEL)), whole((D_MODEL, LANES)), whole((D_MODEL, LANES)),
        ],
        out_specs=(pl.BlockSpec((tm * ROW_PITCH, LANES), lambda i: (i, 0)),
                   pl.BlockSpec((tm * ROW_PITCH, LANES), lambda i: (i, 0)),
                   pl.BlockSpec((tm, N_EXPERTS), lambda i: (i, 0))),
        compiler_params=_params(("parallel",)),
        name="mix_out",
    )(o_f, o_b, proj, proj, proj, proj, proj, proj, proj, proj, x2d, gn_g, conv_w, wo_bf16, g2, wr_hi, wr_lo)


def _select_kernel(aff_ref, idx_ref, sel_ref, *, cap):
    e_n, rows, _ = aff_ref.shape
    bits = lax.bitcast_convert_type(aff_ref[...], I32)

    def count(mask):
        c = jnp.sum(jnp.where(mask, 1.0, 0.0), axis=1, keepdims=True)
        return jnp.sum(c, axis=2, keepdims=True)

    thr = jnp.zeros((e_n, 1, 1), I32)
    for b in range(30, -1, -1):
        cand = thr | (1 << b)
        thr = jnp.where(count(bits >= cand) >= cap, cand, thr)

    gt = bits > thr
    eq = bits == thr
    need = cap - count(gt)

    li = lax.broadcasted_iota(I32, (LANES, LANES), 0)
    lj = lax.broadcasted_iota(I32, (LANES, LANES), 1)
    upper = (li <= lj).astype(BF16)
    ri = lax.broadcasted_iota(I32, (rows, rows), 0)
    rj = lax.broadcasted_iota(I32, (rows, rows), 1)
    lower = (rj < ri).astype(BF16)
    eqf = jnp.where(eq, 1.0, 0.0)
    for e in range(e_n):
        incl = jnp.dot(eqf[e].astype(BF16), upper, preferred_element_type=F32)
        row_tot = jnp.broadcast_to(incl[:, LANES - 1:LANES], (rows, LANES)).astype(BF16)
        row_off = jnp.dot(lower, row_tot, preferred_element_type=F32)
        rank = row_off + incl - eqf[e]
        take = jnp.logical_or(gt[e], jnp.logical_and(eq[e], rank < need[e]))
        sel_ref[e] = jnp.where(take, 1.0, 0.0)

    chunk = min(512, cap)
    lower_incl = (li >= lj).astype(BF16)
    slot0 = lax.broadcasted_iota(I32, (1, chunk), 1).astype(F32)
    rowid = lax.broadcasted_iota(I32, (rows, chunk), 0).astype(F32)

    def compact(e, carry):
        s = sel_ref[e].astype(BF16)
        q_t = lax.dot_general(lower_incl, s, (((1,), (1,)), ((), ())),
                              preferred_element_type=F32)
        q = jnp.dot(s, upper, preferred_element_type=F32)
        row_tot = q[:, LANES - 1:LANES]
        row_cum = jnp.dot(lower, jnp.broadcast_to(row_tot, (rows, LANES)).astype(BF16),
                          preferred_element_type=F32)[:, 0:1] + row_tot
        q_t = q_t.astype(BF16)
        for ch in range(cap // chunk):
            c = slot0 + float(ch * chunk)
            before = row_cum <= c
            r_c = jnp.sum(jnp.where(before, 1.0, 0.0), axis=0, keepdims=True)
            k = c - jnp.sum(jnp.where(before, row_tot, 0.0), axis=0, keepdims=True)
            onehot = jnp.where(rowid == r_c, 1.0, 0.0).astype(BF16)
            q_row = jnp.dot(q_t, onehot, preferred_element_type=F32)
            lane_c = jnp.sum(jnp.where(q_row <= k, 1.0, 0.0), axis=0, keepdims=True)
            token = r_c * LANES + lane_c
            idx_ref[pl.ds(e, 1), ch * chunk:(ch + 1) * chunk] = (token * ROW_PITCH).astype(I32)
        return carry

    lax.fori_loop(0, e_n, compact, 0)


def _select(aff3, cap):
    e_n, rows, lanes = aff3.shape
    return pl.pallas_call(
        functools.partial(_select_kernel, cap=cap),
        out_shape=jax.ShapeDtypeStruct((e_n, cap), I32),
        grid=(1,),
        in_specs=[pl.BlockSpec((e_n, rows, lanes), lambda i: (0, 0, 0))],
        out_specs=pl.BlockSpec((e_n, cap), lambda i: (0, 0)),
        scratch_shapes=[pltpu.VMEM((e_n, rows, lanes), F32)],
        compiler_params=_params(("arbitrary",)),
        name="select",
    )(aff3)


def _experts_kernel(idx_ref, idx_next_ref, tok_hbm, acc_in_hbm, wg_in, wu_in, wd_in, acc_hbm, *rest,
                    cast_weights):
    del acc_in_hbm
    if cast_weights:
        wg_ref, wu_ref, wd_ref, xbuf, xn_ref, yacc_ref, gate_ref, sem_x, sem_a, sem_s = rest
        wg_ref[...] = wg_in[...].astype(BF16)
        wu_ref[...] = wu_in[...].astype(BF16)
        wd_ref[...] = wd_in[...].astype(BF16)
    else:
        wg_ref, wu_ref, wd_ref = wg_in, wu_in, wd_in
        xbuf, xn_ref, yacc_ref, gate_ref, sem_x, sem_a, sem_s = rest
    e = pl.program_id(0)
    c = pl.program_id(1)
    f = pl.program_id(2)
    ne = pl.num_programs(0)
    nt = pl.num_programs(1)
    nf = pl.num_programs(2)
    tc = xn_ref.shape[0]
    g = e * nt + c
    slot = g % 2
    other = 1 - slot

    def start_gather(src_hbm, tiles, ids_ref, base, buf, sem, priority):
        def issue(r, carry):
            pltpu.make_async_copy(src_hbm.at[pl.ds(ids_ref[base + r], tiles), :],
                                  xbuf.at[buf, pl.ds(r * ROW_PITCH, tiles), :], sem.at[buf]).start(priority)
            return carry
        lax.fori_loop(0, tc, issue, 0, unroll=min(ISSUE_UNROLL, tc))

    def wait_gather(src_hbm, tiles, buf, sem):
        n = tc * tiles
        pltpu.make_async_copy(src_hbm.at[pl.ds(0, n), :], xbuf.at[buf, pl.ds(0, n), :], sem.at[buf]).wait()

    def wait_scatter(buf):
        n = tc * ROW_TILES
        pltpu.make_async_copy(xbuf.at[buf, pl.ds(0, n), :], acc_hbm.at[pl.ds(0, n), :], sem_s.at[buf]).wait()

    def tile_rows(r0, n, k):
        return pl.ds(r0 * ROW_PITCH + k, n, stride=ROW_PITCH)

    rb = min(EXPERT_ROWS, tc)

    @pl.when(jnp.logical_and(g == 0, f == 0))
    def _():
        start_gather(tok_hbm, ROW_PITCH, idx_ref, 0, 0, sem_x, TOK_PRIORITY)

    @pl.when(f == 0)
    def _():
        wait_gather(tok_hbm, ROW_PITCH, slot, sem_x)

        for r0 in range(0, tc, rb):
            for k in range(ROW_TILES):
                xn_ref[r0:r0 + rb, k * LANES:(k + 1) * LANES] = xbuf[slot, tile_rows(r0, rb, k), :].astype(BF16)
            aff = xbuf[slot, tile_rows(r0, rb, ROW_TILES), :]
            lane = lax.broadcasted_iota(I32, aff.shape, 1)
            gate_ref[r0:r0 + rb, :] = jnp.sum(jnp.where(lane == e, aff, 0.0), axis=-1, keepdims=True)
            yacc_ref[r0:r0 + rb, :] = jnp.zeros((rb, D_MODEL), F32)

    @pl.when(f == 1)
    def _():
        @pl.when(g > 0)
        def _():
            wait_scatter(other)

        start_gather(acc_hbm, ROW_TILES, idx_ref, c * tc, slot, sem_a, ACC_PRIORITY)

        @pl.when(c + 1 < nt)
        def _():
            start_gather(tok_hbm, ROW_PITCH, idx_ref, (c + 1) * tc, other, sem_x, TOK_PRIORITY)

        @pl.when(jnp.logical_and(c + 1 == nt, e + 1 < ne))
        def _():
            start_gather(tok_hbm, ROW_PITCH, idx_next_ref, 0, other, sem_x, TOK_PRIORITY)

    def ffn(m, carry):
        rows = pl.ds(pl.multiple_of(m * rb, rb), rb)
        xn = xn_ref[rows, :]
        hg = jnp.dot(xn, wg_ref[0], preferred_element_type=F32)
        hu = jnp.dot(xn, wu_ref[0], preferred_element_type=F32)
        hid = (hg / (1.0 + jnp.exp(-hg)) * hu).astype(BF16)
        for n0 in range(0, D_MODEL, DOWN_COLS):
            cols = slice(n0, n0 + DOWN_COLS)
            yacc_ref[rows, cols] += jnp.dot(hid, wd_ref[0, :, cols], preferred_element_type=F32)
        return carry
    lax.fori_loop(0, tc // rb, ffn, 0)

    @pl.when(f == nf - 1)
    def _():
        wait_gather(acc_hbm, ROW_TILES, slot, sem_a)

        for r0 in range(0, tc, rb):
            gate = gate_ref[r0:r0 + rb, :]
            for k in range(ROW_TILES):
                dst = tile_rows(r0, rb, k)
                xbuf[slot, dst, :] = xbuf[slot, dst, :] + yacc_ref[r0:r0 + rb, k * LANES:(k + 1) * LANES] * gate

        def issue(pair, carry):
            for p in range(2):
                r = 2 * pair + p
                pltpu.make_async_copy(xbuf.at[slot, pl.ds(r * ROW_PITCH, ROW_TILES), :],
                                      acc_hbm.at[pl.ds(idx_ref[c * tc + r], ROW_TILES), :],
                                      sem_s.at[slot]).start(p)
            return carry
        lax.fori_loop(0, tc // 2, issue, 0, unroll=min(ISSUE_UNROLL, tc) // 2)

        @pl.when(jnp.logical_and(e == ne - 1, c == nt - 1))
        def _():
            wait_scatter(slot)


def _experts(idx_flat, tok, acc, wg, wu, wd, cap):
    tc = min(EXPERT_TILE, cap)
    cast_weights = wg.dtype != BF16
    tf = 256 if cast_weights else 512
    assert not cast_weights or cap == tc, "each bf16 weight block must be written by exactly one grid step"
    w_specs = [
        pl.BlockSpec((1, D_MODEL, tf), lambda e, c, f: (e, 0, f)),
        pl.BlockSpec((1, D_MODEL, tf), lambda e, c, f: (e, 0, f)),
        pl.BlockSpec((1, tf, D_MODEL), lambda e, c, f: (e, f, 0)),
    ]
    acc_shape = jax.ShapeDtypeStruct(acc.shape, F32)
    acc_spec = pl.BlockSpec(memory_space=pl.ANY)
    w_shape = jax.ShapeDtypeStruct(wg.shape, BF16)
    out = pl.pallas_call(
        functools.partial(_experts_kernel, cast_weights=cast_weights),
        out_shape=(acc_shape, w_shape, w_shape, w_shape) if cast_weights else acc_shape,
        grid=(N_EXPERTS, cap // tc, EXPERT_FF // tf),
        in_specs=[
            pl.BlockSpec((cap,), lambda e, c, f: (e,), memory_space=pltpu.SMEM),
            pl.BlockSpec((cap,), lambda e, c, f: (jnp.minimum(e + 1, N_EXPERTS - 1),), memory_space=pltpu.SMEM),
            pl.BlockSpec(memory_space=pl.ANY),
            pl.BlockSpec(memory_space=pl.ANY),
            *w_specs,
        ],
        out_specs=(acc_spec, *w_specs) if cast_weights else acc_spec,
        scratch_shapes=[
            pltpu.VMEM((2, tc * ROW_PITCH, LANES), F32),
            pltpu.VMEM((tc, D_MODEL), BF16),
            pltpu.VMEM((tc, D_MODEL), F32),
            pltpu.VMEM((tc, 1), F32),
            pltpu.SemaphoreType.DMA((2,)),
            pltpu.SemaphoreType.DMA((2,)),
            pltpu.SemaphoreType.DMA((2,)),
        ],
        input_output_aliases={3: 0},
        compiler_params=_params(("arbitrary", "arbitrary", "arbitrary")),
        name="experts",
    )(idx_flat, idx_flat, tok, acc, wg, wu, wd)
    return out if cast_weights else (out, wg, wu, wd)


def _final_kernel(x_ref, g_ref, o_ref):
    tm = o_ref.shape[0]
    x = jnp.concatenate([x_ref[pl.ds(k, tm, stride=ROW_PITCH), :] for k in range(ROW_TILES)], axis=-1)
    o_ref[...] = _rmsnorm(x, g_ref[...])


def _final_norm(rows, g):
    t = rows.shape[0] // ROW_PITCH
    tm = min(512, t)
    return pl.pallas_call(
        _final_kernel,
        out_shape=jax.ShapeDtypeStruct((t, D_MODEL), F32),
        grid=(t // tm,),
        in_specs=[pl.BlockSpec((tm * ROW_PITCH, LANES), lambda i: (i, 0)),
                  pl.BlockSpec((1, D_MODEL), lambda i: (0, 0))],
        out_specs=pl.BlockSpec((tm, D_MODEL), lambda i: (i, 0)),
        compiler_params=_params(("parallel",)),
        name="final_norm",
    )(rows, g)


def _rotary_tables(seq):
    half = HEAD_DIM // 2
    inv = ROPE_BASE ** (-jnp.arange(half, dtype=F32) / half)
    ang = jnp.arange(seq, dtype=F32)[:, None] * inv[None, :]
    cos, sin = jnp.cos(ang), jnp.sin(ang)
    return jnp.concatenate([cos, cos], axis=-1), jnp.concatenate([-sin, sin], axis=-1)


def _trunk(x, w, expert_w):
    batch, seq, _ = x.shape
    t = batch * seq
    cap = EC_CAPACITY_FACTOR * t // N_EXPERTS
    x2d = x.reshape(t, D_MODEL)
    cos_t, sin_t = _rotary_tables(seq)

    proj = _in_proj(x2d, w["norm_mix_g"], w["w_in"], cos_t, sin_t, seq)
    o_f, o_b = _retention(proj, w["decays"], batch, seq)
    acc, tok, aff = _mix_out(o_f, o_b, proj, x2d, w["ret_gn_g"], w["conv_w"], w["w_o"],
                             w["norm_ffn_g"], *w["w_router_split"], seq)
    idx = _select(aff.T.reshape(N_EXPERTS, t // LANES, LANES), cap)
    acc, *expert_w = _experts(idx.reshape(N_EXPERTS * cap), tok, acc, *expert_w, cap)
    return _final_norm(acc, w["final_norm_g"]).reshape(batch, seq, D_MODEL), expert_w


def kernel(x_prompt, x_sample, norm_mix_g, w_in, conv_w, ret_decay_fwd, ret_decay_bwd, ret_gn_g, w_o,
           norm_ffn_g, w_router, w_gate, w_up, w_down, final_norm_g):
    w = {
        "norm_mix_g": norm_mix_g[0][None, :],
        "w_in": w_in[0].astype(BF16),
        "conv_w": conv_w[0],
        "decays": jnp.stack([ret_decay_fwd[0], ret_decay_bwd[0]]),
        "ret_gn_g": ret_gn_g[0][None, :],
        "w_o": w_o[0].astype(BF16),
        "norm_ffn_g": norm_ffn_g[0][None, :],
        "w_router_split": _split_bf16(jnp.pad(w_router[0], ((0, 0), (0, LANES - N_EXPERTS)))),
        "final_norm_g": final_norm_g[None, :],
    }
    y_prompt, expert_w = _trunk(x_prompt, w, (w_gate[0], w_up[0], w_down[0]))
    y_sample, _ = _trunk(x_sample, w, expert_w)
    return (y_prompt, y_sample)
```

```python
import functools

import jax
import jax.numpy as jnp
from jax import lax
from jax.experimental import pallas as pl
from jax.experimental.pallas import tpu as pltpu

F32 = jnp.float32
BF16 = jnp.bfloat16
I32 = jnp.int32

D_MODEL = 2048
RET_WIDTH = 1024
RET_HEADS = 8
HEAD_DIM = 128
IN_COLS = 7168
COL_BLOCK = 1024
N_EXPERTS = 16
EC_CAPACITY_FACTOR = 2
EXPERT_FF = 2048
CHUNK = 256
ROPE_BASE = 10000.0
EPS = 1e-6
LANES = 128
HALO_ROWS = 16
ROW_TILES = D_MODEL // LANES
ROW_PITCH = ROW_TILES + 1
RET_CHUNKS = 4
MIX_SPLIT = 2
EXPERT_TILE = 1024
EXPERT_ROWS = 1024
DOWN_COLS = 512
ISSUE_UNROLL = 64
TOK_PRIORITY = 0
ACC_PRIORITY = 1
VMEM_LIMIT = 56 * 1024 * 1024


def _params(sem, vmem=VMEM_LIMIT):
    return pltpu.CompilerParams(dimension_semantics=sem, vmem_limit_bytes=vmem)


def _rmsnorm(x, g):
    return x * lax.rsqrt(jnp.mean(x * x, axis=-1, keepdims=True) + EPS) * g


def _split_bf16(x):
    hi = x.astype(BF16)
    lo = (x - hi.astype(F32)).astype(BF16)
    return hi, lo


def _inproj_kernel(x_ref, g_ref, w_ref, cos_ref, sin_ref, o_ref, h_ref, acc_ref):
    j = pl.program_id(1)

    @pl.when(j == 0)
    def _():
        h_ref[...] = _rmsnorm(x_ref[...], g_ref[...]).astype(BF16)

    @pl.when(j < 2)
    def _():
        acc_ref[...] = jnp.dot(h_ref[...], w_ref[...], preferred_element_type=F32)
        scale = jnp.where(j == 1, HEAD_DIM ** -0.5, 1.0).astype(F32)
        cos = cos_ref[...]
        sin = sin_ref[...]
        for h in range(COL_BLOCK // HEAD_DIM):
            sl = slice(h * HEAD_DIM, (h + 1) * HEAD_DIM)
            t = acc_ref[:, sl]
            o_ref[:, sl] = ((t * cos + pltpu.roll(t, HEAD_DIM // 2, axis=1) * sin) * scale).astype(BF16)

    @pl.when(j >= 2)
    def _():
        o_ref[...] = jnp.dot(h_ref[...], w_ref[...], preferred_element_type=F32).astype(BF16)


def _in_proj(x2d, g, w_bf16, cos_t, sin_t, seq):
    t = x2d.shape[0]
    tm = min(1024, seq)
    blocks_per_seq = seq // tm
    return pl.pallas_call(
        _inproj_kernel,
        out_shape=jax.ShapeDtypeStruct((t, IN_COLS), BF16),
        grid=(t // tm, IN_COLS // COL_BLOCK),
        in_specs=[
            pl.BlockSpec((tm, D_MODEL), lambda i, j: (i, 0)),
            pl.BlockSpec((1, D_MODEL), lambda i, j: (0, 0)),
            pl.BlockSpec((D_MODEL, COL_BLOCK), lambda i, j: (0, j)),
            pl.BlockSpec((tm, HEAD_DIM), lambda i, j: (i % blocks_per_seq, 0)),
            pl.BlockSpec((tm, HEAD_DIM), lambda i, j: (i % blocks_per_seq, 0)),
        ],
        out_specs=pl.BlockSpec((tm, COL_BLOCK), lambda i, j: (i, j)),
        scratch_shapes=[pltpu.VMEM((tm, D_MODEL), BF16), pltpu.VMEM((tm, COL_BLOCK), F32)],
        compiler_params=_params(("parallel", "arbitrary")),
        name="in_proj",
    )(x2d, g, w_bf16, cos_t, sin_t)


def _retention_kernel(dec_ref, qf_ref, kf_ref, vf_ref, qb_ref, kb_ref, vb_ref,
                      of_ref, ob_ref,
                      sf_ref, sb_ref, dtab_ref, xif_ref, xib_ref, zf_ref, zb_ref):
    first = jnp.logical_and(pl.program_id(0) == 0, pl.program_id(1) == 0)

    @pl.when(first)
    def _():
        diff = (lax.broadcasted_iota(I32, (CHUNK, CHUNK), 0)
                - lax.broadcasted_iota(I32, (CHUNK, CHUNK), 1)).astype(F32)
        row = lax.broadcasted_iota(I32, (CHUNK, HEAD_DIM), 0).astype(F32)
        for h in range(RET_HEADS):
            lgf = -jnp.exp(dec_ref[0:1, h:h + 1])
            lgb = -jnp.exp(dec_ref[1:2, h:h + 1])
            dtab_ref[h] = jnp.where(diff >= 0, jnp.exp(lgf * jnp.maximum(diff, 0.0)),
                                    jnp.exp(lgb * jnp.maximum(-diff, 0.0)))
            xif_ref[h] = jnp.exp(lgf * (row + 1.0))
            zf_ref[h] = jnp.exp(lgf * (CHUNK - 1.0 - row))
            xib_ref[h] = jnp.exp(lgb * (CHUNK - row))
            zb_ref[h] = jnp.exp(lgb * row)

    @pl.when(pl.program_id(1) == 0)
    def _():
        sf_ref[...] = jnp.zeros_like(sf_ref)
        sb_ref[...] = jnp.zeros_like(sb_ref)

    chunks = qf_ref.shape[0] // CHUNK
    nt = (((1,), (1,)), ((), ()))
    tn = (((0,), (0,)), ((), ()))
    for h in range(RET_HEADS):
        sl = slice(h * HEAD_DIM, (h + 1) * HEAD_DIM)
        gcf = jnp.exp(-jnp.exp(dec_ref[0:1, h:h + 1]) * CHUNK)
        gcb = jnp.exp(-jnp.exp(dec_ref[1:2, h:h + 1]) * CHUNK)

        for u in range(chunks):
            rows = slice(u * CHUNK, (u + 1) * CHUNK)
            q = qf_ref[rows, sl]
            k = kf_ref[rows, sl]
            v = vf_ref[rows, sl]
            s = lax.dot_general(q, k, nt, preferred_element_type=F32) * dtab_ref[h]
            sf = sf_ref[h]
            o = jnp.dot(s.astype(BF16), v, preferred_element_type=F32)
            o = o + jnp.dot(q, sf.astype(BF16), preferred_element_type=F32) * xif_ref[h]
            of_ref[rows, sl] = o
            kz = (k.astype(F32) * zf_ref[h]).astype(BF16)
            sf_ref[h] = sf * gcf + lax.dot_general(kz, v, tn, preferred_element_type=F32)

        for u in reversed(range(chunks)):
            rows = slice(u * CHUNK, (u + 1) * CHUNK)
            q = qb_ref[rows, sl]
            v = vb_ref[rows, sl]
            sb = sb_ref[h]
            ob_ref[rows, sl] = jnp.dot(q, sb.astype(BF16), preferred_element_type=F32) * xib_ref[h]
            kz = (kb_ref[rows, sl].astype(F32) * zb_ref[h]).astype(BF16)
            sb_ref[h] = sb * gcb + lax.dot_general(kz, v, tn, preferred_element_type=F32)


def _retention(proj, decays, batch, seq):
    t = proj.shape[0]
    rows = min(RET_CHUNKS * CHUNK, seq)
    n = seq // rows
    fwd = lambda col: pl.BlockSpec((rows, COL_BLOCK), lambda b, c: (b * n + c, col))
    bwd = lambda col: pl.BlockSpec((rows, COL_BLOCK), lambda b, c: (b * n + n - 1 - c, col))
    state = pltpu.VMEM((RET_HEADS, HEAD_DIM, HEAD_DIM), F32)
    decay = pltpu.VMEM((RET_HEADS, CHUNK, CHUNK), F32)
    per_row = pltpu.VMEM((RET_HEADS, CHUNK, HEAD_DIM), F32)
    return pl.pallas_call(
        _retention_kernel,
        out_shape=(jax.ShapeDtypeStruct((t, RET_WIDTH), F32), jax.ShapeDtypeStruct((t, RET_WIDTH), F32)),
        grid=(batch, n),
        in_specs=[pl.BlockSpec((2, RET_HEADS), lambda b, c: (0, 0)),
                  fwd(0), fwd(1), fwd(2), bwd(0), bwd(1), bwd(2)],
        out_specs=(pl.BlockSpec((rows, RET_WIDTH), lambda b, c: (b * n + c, 0)),
                   pl.BlockSpec((rows, RET_WIDTH), lambda b, c: (b * n + n - 1 - c, 0))),
        scratch_shapes=[state, state, decay, per_row, per_row, per_row, per_row],
        compiler_params=_params(("arbitrary", "arbitrary")),
        name="retention",
    )(decays, proj, proj, proj, proj, proj, proj)


def _mixout_kernel(of_ref, ob_ref, g_ref, cb_ref, cc_ref, ch_ref, ccp_ref, chp_ref, ccn_ref, chn_ref,
                   x_ref, gn_ref, cw_ref, wo_ref, g2_ref, wr_hi_ref, wr_lo_ref,
                   acc_ref, tok_ref, aff_ref, *, blocks_per_seq):
    i = pl.program_id(0)
    tm = x_ref.shape[0]

    gn = gn_ref[...]
    z = cc_ref[...].astype(F32) * ch_ref[...].astype(F32)
    pos = i % blocks_per_seq
    keep_prev = jnp.where(pos == 0, 0.0, 1.0).astype(F32)
    keep_next = jnp.where(pos == blocks_per_seq - 1, 0.0, 1.0).astype(F32)
    last = HALO_ROWS - 1
    halo_prev = (ccp_ref[...].astype(F32) * chp_ref[...].astype(F32))[last:, :] * keep_prev
    halo_next = (ccn_ref[...].astype(F32) * chn_ref[...].astype(F32))[0:1, :] * keep_next
    row = lax.broadcasted_iota(I32, z.shape, 0)
    z_prev = jnp.where(row == 0, halo_prev, pltpu.roll(z, 1, axis=0))
    z_next = jnp.where(row == tm - 1, halo_next, pltpu.roll(z, tm - 1, axis=0))
    cw = cw_ref[...]
    zc = z_prev * cw[0:1, :] + z * cw[1:2, :] + z_next * cw[2:3, :]
    wr_hi = wr_hi_ref[...]
    wr_lo = wr_lo_ref[...]

    rs = tm // MIX_SPLIT
    for s in range(MIX_SPLIT):
        rows = slice(s * rs, (s + 1) * rs)
        o = of_ref[rows, :] + ob_ref[rows, :]
        gate_in = g_ref[rows, :].astype(F32)
        parts = []
        for h in range(RET_HEADS):
            sl = slice(h * HEAD_DIM, (h + 1) * HEAD_DIM)
            oh = o[:, sl]
            mu = jnp.mean(oh, axis=-1, keepdims=True)
            d = oh - mu
            var = jnp.mean(d * d, axis=-1, keepdims=True)
            gh = gate_in[:, sl]
            swish = gh / (1.0 + jnp.exp(-gh))
            parts.append((swish * (d * lax.rsqrt(var + EPS) * gn[:, sl])).astype(BF16))
        ret = jnp.concatenate(parts, axis=-1)
        conv = cb_ref[rows, :].astype(F32) * zc[rows, :]

        y = jnp.dot(ret, wo_ref[0:RET_WIDTH, :], preferred_element_type=F32)
        y = y + jnp.dot(conv.astype(BF16), wo_ref[RET_WIDTH:, :], preferred_element_type=F32)
        x1 = x_ref[rows, :] + y
        h2 = _rmsnorm(x1, g2_ref[...])
        for k in range(ROW_TILES):
            cols = slice(k * LANES, (k + 1) * LANES)
            dst = pl.ds(s * rs * ROW_PITCH + k, rs, stride=ROW_PITCH)
            acc_ref[dst, :] = x1[:, cols]
            tok_ref[dst, :] = h2[:, cols]
        last_tile = pl.ds(s * rs * ROW_PITCH + ROW_TILES, rs, stride=ROW_PITCH)
        acc_ref[last_tile, :] = jnp.zeros((rs, LANES), F32)
        h_hi, h_lo = _split_bf16(h2)
        d = functools.partial(jnp.dot, preferred_element_type=F32)
        logits = d(h_hi, wr_hi) + d(h_hi, wr_lo) + d(h_lo, wr_hi)
        lane = lax.broadcasted_iota(I32, logits.shape, 1)
        logits = jnp.where(lane < N_EXPERTS, logits, -jnp.inf)
        ex = jnp.exp(logits - jnp.max(logits, axis=-1, keepdims=True))
        aff = ex / jnp.sum(ex, axis=-1, keepdims=True)
        tok_ref[last_tile, :] = aff
        aff_ref[rows, :] = aff[:, 0:N_EXPERTS]


def _mix_out(o_f, o_b, proj, x2d, gn_g, conv_w, wo_bf16, g2, wr_hi, wr_lo, seq):
    t = x2d.shape[0]
    tm = min(256, seq)
    bps = seq // tm
    r8 = tm // HALO_ROWS
    n8 = t // HALO_ROWS
    col = lambda c: pl.BlockSpec((tm, COL_BLOCK), lambda i: (i, c))
    prev = lambda c: pl.BlockSpec((HALO_ROWS, COL_BLOCK), lambda i: (jnp.maximum(i * r8 - 1, 0), c))
    nxt = lambda c: pl.BlockSpec((HALO_ROWS, COL_BLOCK), lambda i: (jnp.minimum((i + 1) * r8, n8 - 1), c))
    whole = lambda shape: pl.BlockSpec(shape, lambda i: (0,) * len(shape))
    return pl.pallas_call(
        functools.partial(_mixout_kernel, blocks_per_seq=bps),
        out_shape=(jax.ShapeDtypeStruct((t * ROW_PITCH, LANES), F32),
                   jax.ShapeDtypeStruct((t * ROW_PITCH, LANES), F32),
                   jax.ShapeDtypeStruct((t, N_EXPERTS), F32)),
        grid=(t // tm,),
        in_specs=[
            pl.BlockSpec((tm, RET_WIDTH), lambda i: (i, 0)),
            pl.BlockSpec((tm, RET_WIDTH), lambda i: (i, 0)),
            col(3), col(4), col(5), col(6), prev(5), prev(6), nxt(5), nxt(6),
            pl.BlockSpec((tm, D_MODEL), lambda i: (i, 0)),
            whole((1, RET_WIDTH)), whole((3, COL_BLOCK)), whole((D_MODEL, D_MODEL)),
            whole((1, D_MODEL)), whole((D_MODEL, LANES)), whole((D_MODEL, LANES)),
        ],
        out_specs=(pl.BlockSpec((tm * ROW_PITCH, LANES), lambda i: (i, 0)),
                   pl.BlockSpec((tm * ROW_PITCH, LANES), lambda i: (i, 0)),
                   pl.BlockSpec((tm, N_EXPERTS), lambda i: (i, 0))),
        compiler_params=_params(("parallel",)),
        name="mix_out",
    )(o_f, o_b, proj, proj, proj, proj, proj, proj, proj, proj, x2d, gn_g, conv_w, wo_bf16, g2, wr_hi, wr_lo)


def _select_kernel(aff_ref, idx_ref, sel_ref, *, cap):
    e_n, rows, _ = aff_ref.shape
    bits = lax.bitcast_convert_type(aff_ref[...], I32)

    def count(mask):
        c = jnp.sum(jnp.where(mask, 1.0, 0.0), axis=1, keepdims=True)
        return jnp.sum(c, axis=2, keepdims=True)

    thr = jnp.zeros((e_n, 1, 1), I32)
    for b in range(30, -1, -1):
        cand = thr | (1 << b)
        thr = jnp.where(count(bits >= cand) >= cap, cand, thr)

    gt = bits > thr
    eq = bits == thr
    need = cap - count(gt)

    li = lax.broadcasted_iota(I32, (LANES, LANES), 0)
    lj = lax.broadcasted_iota(I32, (LANES, LANES), 1)
    upper = (li <= lj).astype(BF16)
    ri = lax.broadcasted_iota(I32, (rows, rows), 0)
    rj = lax.broadcasted_iota(I32, (rows, rows), 1)
    lower = (rj < ri).astype(BF16)
    eqf = jnp.where(eq, 1.0, 0.0)
    for e in range(e_n):
        incl = jnp.dot(eqf[e].astype(BF16), upper, preferred_element_type=F32)
        row_tot = jnp.broadcast_to(incl[:, LANES - 1:LANES], (rows, LANES)).astype(BF16)
        row_off = jnp.dot(lower, row_tot, preferred_element_type=F32)
        rank = row_off + incl - eqf[e]
        take = jnp.logical_or(gt[e], jnp.logical_and(eq[e], rank < need[e]))
        sel_ref[e] = jnp.where(take, 1.0, 0.0)

    chunk = min(512, cap)
    lower_incl = (li >= lj).astype(BF16)
    slot0 = lax.broadcasted_iota(I32, (1, chunk), 1).astype(F32)
    rowid = lax.broadcasted_iota(I32, (rows, chunk), 0).astype(F32)

    def compact(e, carry):
        s = sel_ref[e].astype(BF16)
        q_t = lax.dot_general(lower_incl, s, (((1,), (1,)), ((), ())),
                              preferred_element_type=F32)
        q = jnp.dot(s, upper, preferred_element_type=F32)
        row_tot = q[:, LANES - 1:LANES]
        row_cum = jnp.dot(lower, jnp.broadcast_to(row_tot, (rows, LANES)).astype(BF16),
                          preferred_element_type=F32)[:, 0:1] + row_tot
        q_t = q_t.astype(BF16)
        for ch in range(cap // chunk):
            c = slot0 + float(ch * chunk)
            before = row_cum <= c
            r_c = jnp.sum(jnp.where(before, 1.0, 0.0), axis=0, keepdims=True)
            k = c - jnp.sum(jnp.where(before, row_tot, 0.0), axis=0, keepdims=True)
            onehot = jnp.where(rowid == r_c, 1.0, 0.0).astype(BF16)
            q_row = jnp.dot(q_t, onehot, preferred_element_type=F32)
            lane_c = jnp.sum(jnp.where(q_row <= k, 1.0, 0.0), axis=0, keepdims=True)
            token = r_c * LANES + lane_c
            idx_ref[pl.ds(e, 1), ch * chunk:(ch + 1) * chunk] = (token * ROW_PITCH).astype(I32)
        return carry

    lax.fori_loop(0, e_n, compact, 0)


def _select(aff3, cap):
    e_n, rows, lanes = aff3.shape
    return pl.pallas_call(
        functools.partial(_select_kernel, cap=cap),
        out_shape=jax.ShapeDtypeStruct((e_n, cap), I32),
        grid=(1,),
        in_specs=[pl.BlockSpec((e_n, rows, lanes), lambda i: (0, 0, 0))],
        out_specs=pl.BlockSpec((e_n, cap), lambda i: (0, 0)),
        scratch_shapes=[pltpu.VMEM((e_n, rows, lanes), F32)],
        compiler_params=_params(("arbitrary",)),
        name="select",
    )(aff3)


def _experts_kernel(idx_ref, idx_next_ref, tok_hbm, acc_in_hbm, wg_in, wu_in, wd_in, acc_hbm, *rest,
                    cast_weights):
    del acc_in_hbm
    if cast_weights:
        wg_ref, wu_ref, wd_ref, xbuf, xn_ref, yacc_ref, gate_ref, sem_x, sem_a, sem_s = rest
        wg_ref[...] = wg_in[...].astype(BF16)
        wu_ref[...] = wu_in[...].astype(BF16)
        wd_ref[...] = wd_in[...].astype(BF16)
    else:
        wg_ref, wu_ref, wd_ref = wg_in, wu_in, wd_in
        xbuf, xn_ref, yacc_ref, gate_ref, sem_x, sem_a, sem_s = rest
    e = pl.program_id(0)
    c = pl.program_id(1)
    f = pl.program_id(2)
    ne = pl.num_programs(0)
    nt = pl.num_programs(1)
    nf = pl.num_programs(2)
    tc = xn_ref.shape[0]
    g = e * nt + c
    slot = g % 2
    other = 1 - slot

    def start_gather(src_hbm, tiles, ids_ref, base, buf, sem, priority):
        def issue(r, carry):
            pltpu.make_async_copy(src_hbm.at[pl.ds(ids_ref[base + r], tiles), :],
                                  xbuf.at[buf, pl.ds(r * ROW_PITCH, tiles), :], sem.at[buf]).start(priority)
            return carry
        lax.fori_loop(0, tc, issue, 0, unroll=min(ISSUE_UNROLL, tc))

    def wait_gather(src_hbm, tiles, buf, sem):
        n = tc * tiles
        pltpu.make_async_copy(src_hbm.at[pl.ds(0, n), :], xbuf.at[buf, pl.ds(0, n), :], sem.at[buf]).wait()

    def wait_scatter(buf):
        n = tc * ROW_TILES
        pltpu.make_async_copy(xbuf.at[buf, pl.ds(0, n), :], acc_hbm.at[pl.ds(0, n), :], sem_s.at[buf]).wait()

    def tile_rows(r0, n, k):
        return pl.ds(r0 * ROW_PITCH + k, n, stride=ROW_PITCH)

    rb = min(EXPERT_ROWS, tc)

    @pl.when(jnp.logical_and(g == 0, f == 0))
    def _():
        start_gather(tok_hbm, ROW_PITCH, idx_ref, 0, 0, sem_x, TOK_PRIORITY)

    @pl.when(f == 0)
    def _():
        wait_gather(tok_hbm, ROW_PITCH, slot, sem_x)

        for r0 in range(0, tc, rb):
            for k in range(ROW_TILES):
                xn_ref[r0:r0 + rb, k * LANES:(k + 1) * LANES] = xbuf[slot, tile_rows(r0, rb, k), :].astype(BF16)
            aff = xbuf[slot, tile_rows(r0, rb, ROW_TILES), :]
            lane = lax.broadcasted_iota(I32, aff.shape, 1)
            gate_ref[r0:r0 + rb, :] = jnp.sum(jnp.where(lane == e, aff, 0.0), axis=-1, keepdims=True)
            yacc_ref[r0:r0 + rb, :] = jnp.zeros((rb, D_MODEL), F32)

    @pl.when(f == 1)
    def _():
        @pl.when(g > 0)
        def _():
            wait_scatter(other)

        start_gather(acc_hbm, ROW_TILES, idx_ref, c * tc, slot, sem_a, ACC_PRIORITY)

    @pl.when(f == 2)
    def _():
        @pl.when(c + 1 < nt)
        def _():
            start_gather(tok_hbm, ROW_PITCH, idx_ref, (c + 1) * tc, other, sem_x, TOK_PRIORITY)

        @pl.when(jnp.logical_and(c + 1 == nt, e + 1 < ne))
        def _():
            start_gather(tok_hbm, ROW_PITCH, idx_next_ref, 0, other, sem_x, TOK_PRIORITY)

    def ffn(m, carry):
        rows = pl.ds(pl.multiple_of(m * rb, rb), rb)
        xn = xn_ref[rows, :]
        hg = jnp.dot(xn, wg_ref[0], preferred_element_type=F32)
        hu = jnp.dot(xn, wu_ref[0], preferred_element_type=F32)
        hid = (hg / (1.0 + jnp.exp(-hg)) * hu).astype(BF16)
        for n0 in range(0, D_MODEL, DOWN_COLS):
            cols = slice(n0, n0 + DOWN_COLS)
            yacc_ref[rows, cols] += jnp.dot(hid, wd_ref[0, :, cols], preferred_element_type=F32)
        return carry
    lax.fori_loop(0, tc // rb, ffn, 0)

    @pl.when(f == nf - 1)
    def _():
        wait_gather(acc_hbm, ROW_TILES, slot, sem_a)

        for r0 in range(0, tc, rb):
            gate = gate_ref[r0:r0 + rb, :]
            for k in range(ROW_TILES):
                dst = tile_rows(r0, rb, k)
                xbuf[slot, dst, :] = xbuf[slot, dst, :] + yacc_ref[r0:r0 + rb, k * LANES:(k + 1) * LANES] * gate

        def issue(pair, carry):
            for p in range(2):
                r = 2 * pair + p
                pltpu.make_async_copy(xbuf.at[slot, pl.ds(r * ROW_PITCH, ROW_TILES), :],
                                      acc_hbm.at[pl.ds(idx_ref[c * tc + r], ROW_TILES), :],
                                      sem_s.at[slot]).start(p)
            return carry
        lax.fori_loop(0, tc // 2, issue, 0, unroll=min(ISSUE_UNROLL, tc) // 2)

        @pl.when(jnp.logical_and(e == ne - 1, c == nt - 1))
        def _():
            wait_scatter(slot)


def _experts(idx_flat, tok, acc, wg, wu, wd, cap):
    tc = min(EXPERT_TILE, cap)
    cast_weights = wg.dtype != BF16
    tf = 256 if cast_weights else 512
    assert not cast_weights or cap == tc, "each bf16 weight block must be written by exactly one grid step"
    assert EXPERT_FF // tf >= 3, "the copy schedule uses f-steps 0, 1, 2 and the last"
    w_specs = [
        pl.BlockSpec((1, D_MODEL, tf), lambda e, c, f: (e, 0, f)),
        pl.BlockSpec((1, D_MODEL, tf), lambda e, c, f: (e, 0, f)),
        pl.BlockSpec((1, tf, D_MODEL), lambda e, c, f: (e, f, 0)),
    ]
    acc_shape = jax.ShapeDtypeStruct(acc.shape, F32)
    acc_spec = pl.BlockSpec(memory_space=pl.ANY)
    w_shape = jax.ShapeDtypeStruct(wg.shape, BF16)
    out = pl.pallas_call(
        functools.partial(_experts_kernel, cast_weights=cast_weights),
        out_shape=(acc_shape, w_shape, w_shape, w_shape) if cast_weights else acc_shape,
        grid=(N_EXPERTS, cap // tc, EXPERT_FF // tf),
        in_specs=[
            pl.BlockSpec((cap,), lambda e, c, f: (e,), memory_space=pltpu.SMEM),
            pl.BlockSpec((cap,), lambda e, c, f: (jnp.minimum(e + 1, N_EXPERTS - 1),), memory_space=pltpu.SMEM),
            pl.BlockSpec(memory_space=pl.ANY),
            pl.BlockSpec(memory_space=pl.ANY),
            *w_specs,
        ],
        out_specs=(acc_spec, *w_specs) if cast_weights else acc_spec,
        scratch_shapes=[
            pltpu.VMEM((2, tc * ROW_PITCH, LANES), F32),
            pltpu.VMEM((tc, D_MODEL), BF16),
            pltpu.VMEM((tc, D_MODEL), F32),
            pltpu.VMEM((tc, 1), F32),
            pltpu.SemaphoreType.DMA((2,)),
            pltpu.SemaphoreType.DMA((2,)),
            pltpu.SemaphoreType.DMA((2,)),
        ],
        input_output_aliases={3: 0},
        compiler_params=_params(("arbitrary", "arbitrary", "arbitrary")),
        name="experts",
    )(idx_flat, idx_flat, tok, acc, wg, wu, wd)
    return out if cast_weights else (out, wg, wu, wd)


def _final_kernel(x_ref, g_ref, o_ref):
    tm = o_ref.shape[0]
    x = jnp.concatenate([x_ref[pl.ds(k, tm, stride=ROW_PITCH), :] for k in range(ROW_TILES)], axis=-1)
    o_ref[...] = _rmsnorm(x, g_ref[...])


def _final_norm(rows, g):
    t = rows.shape[0] // ROW_PITCH
    tm = min(512, t)
    return pl.pallas_call(
        _final_kernel,
        out_shape=jax.ShapeDtypeStruct((t, D_MODEL), F32),
        grid=(t // tm,),
        in_specs=[pl.BlockSpec((tm * ROW_PITCH, LANES), lambda i: (i, 0)),
                  pl.BlockSpec((1, D_MODEL), lambda i: (0, 0))],
        out_specs=pl.BlockSpec((tm, D_MODEL), lambda i: (i, 0)),
        compiler_params=_params(("parallel",)),
        name="final_norm",
    )(rows, g)


def _rotary_tables(seq):
    half = HEAD_DIM // 2
    inv = ROPE_BASE ** (-jnp.arange(half, dtype=F32) / half)
    ang = jnp.arange(seq, dtype=F32)[:, None] * inv[None, :]
    cos, sin = jnp.cos(ang), jnp.sin(ang)
    return jnp.concatenate([cos, cos], axis=-1), jnp.concatenate([-sin, sin], axis=-1)


def _trunk(x, w, expert_w):
    batch, seq, _ = x.shape
    t = batch * seq
    cap = EC_CAPACITY_FACTOR * t // N_EXPERTS
    x2d = x.reshape(t, D_MODEL)
    cos_t, sin_t = _rotary_tables(seq)

    proj = _in_proj(x2d, w["norm_mix_g"], w["w_in"], cos_t, sin_t, seq)
    o_f, o_b = _retention(proj, w["decays"], batch, seq)
    acc, tok, aff = _mix_out(o_f, o_b, proj, x2d, w["ret_gn_g"], w["conv_w"], w["w_o"],
                             w["norm_ffn_g"], *w["w_router_split"], seq)
    idx = _select(aff.T.reshape(N_EXPERTS, t // LANES, LANES), cap)
    acc, *expert_w = _experts(idx.reshape(N_EXPERTS * cap), tok, acc, *expert_w, cap)
    return _final_norm(acc, w["final_norm_g"]).reshape(batch, seq, D_MODEL), expert_w


def kernel(x_prompt, x_sample, norm_mix_g, w_in, conv_w, ret_decay_fwd, ret_decay_bwd, ret_gn_g, w_o,
           norm_ffn_g, w_router, w_gate, w_up, w_down, final_norm_g):
    w = {
        "norm_mix_g": norm_mix_g[0][None, :],
        "w_in": w_in[0].astype(BF16),
        "conv_w": conv_w[0],
        "decays": jnp.stack([ret_decay_fwd[0], ret_decay_bwd[0]]),
        "ret_gn_g": ret_gn_g[0][None, :],
        "w_o": w_o[0].astype(BF16),
        "norm_ffn_g": norm_ffn_g[0][None, :],
        "w_router_split": _split_bf16(jnp.pad(w_router[0], ((0, 0), (0, LANES - N_EXPERTS)))),
        "final_norm_g": final_norm_g[None, :],
    }
    y_prompt, expert_w = _trunk(x_prompt, w, (w_gate[0], w_up[0], w_down[0]))
    y_sample, _ = _trunk(x_sample, w, expert_w)
    return (y_prompt, y_sample)
```

```python
import functools

import jax
import jax.numpy as jnp
from jax import lax
from jax.experimental import pallas as pl
from jax.experimental.pallas import tpu as pltpu

F32 = jnp.float32
BF16 = jnp.bfloat16
I32 = jnp.int32

D_MODEL = 2048
RET_WIDTH = 1024
RET_HEADS = 8
HEAD_DIM = 128
IN_COLS = 7168
COL_BLOCK = 1024
N_EXPERTS = 16
EC_CAPACITY_FACTOR = 2
EXPERT_FF = 2048
CHUNK = 256
ROPE_BASE = 10000.0
EPS = 1e-6
LANES = 128
HALO_ROWS = 16
ROW_TILES = D_MODEL // LANES
ROW_PITCH = ROW_TILES + 1
RET_CHUNKS = 4
MIX_SPLIT = 2
EXPERT_TILE = 1024
EXPERT_ROWS = 1024
DOWN_COLS = 512
ISSUE_UNROLL = 64
TOK_PRIORITY = 0
ACC_PRIORITY = 1
VMEM_LIMIT = 56 * 1024 * 1024


def _params(sem, vmem=VMEM_LIMIT):
    return pltpu.CompilerParams(dimension_semantics=sem, vmem_limit_bytes=vmem)


def _rmsnorm(x, g):
    return x * lax.rsqrt(jnp.mean(x * x, axis=-1, keepdims=True) + EPS) * g


def _split_bf16(x):
    hi = x.astype(BF16)
    lo = (x - hi.astype(F32)).astype(BF16)
    return hi, lo


def _inproj_kernel(x_ref, g_ref, w_ref, cos_ref, sin_ref, o_ref, h_ref, acc_ref):
    j = pl.program_id(1)

    @pl.when(j == 0)
    def _():
        h_ref[...] = _rmsnorm(x_ref[...], g_ref[...]).astype(BF16)

    @pl.when(j < 2)
    def _():
        acc_ref[...] = jnp.dot(h_ref[...], w_ref[...], preferred_element_type=F32)
        scale = jnp.where(j == 1, HEAD_DIM ** -0.5, 1.0).astype(F32)
        cos = cos_ref[...]
        sin = sin_ref[...]
        for h in range(COL_BLOCK // HEAD_DIM):
            sl = slice(h * HEAD_DIM, (h + 1) * HEAD_DIM)
            t = acc_ref[:, sl]
            o_ref[:, sl] = ((t * cos + pltpu.roll(t, HEAD_DIM // 2, axis=1) * sin) * scale).astype(BF16)

    @pl.when(j >= 2)
    def _():
        o_ref[...] = jnp.dot(h_ref[...], w_ref[...], preferred_element_type=F32).astype(BF16)


def _in_proj(x2d, g, w_bf16, cos_t, sin_t, seq):
    t = x2d.shape[0]
    tm = min(1024, seq)
    blocks_per_seq = seq // tm
    return pl.pallas_call(
        _inproj_kernel,
        out_shape=jax.ShapeDtypeStruct((t, IN_COLS), BF16),
        grid=(t // tm, IN_COLS // COL_BLOCK),
        in_specs=[
            pl.BlockSpec((tm, D_MODEL), lambda i, j: (i, 0)),
            pl.BlockSpec((1, D_MODEL), lambda i, j: (0, 0)),
            pl.BlockSpec((D_MODEL, COL_BLOCK), lambda i, j: (0, j)),
            pl.BlockSpec((tm, HEAD_DIM), lambda i, j: (i % blocks_per_seq, 0)),
            pl.BlockSpec((tm, HEAD_DIM), lambda i, j: (i % blocks_per_seq, 0)),
        ],
        out_specs=pl.BlockSpec((tm, COL_BLOCK), lambda i, j: (i, j)),
        scratch_shapes=[pltpu.VMEM((tm, D_MODEL), BF16), pltpu.VMEM((tm, COL_BLOCK), F32)],
        compiler_params=_params(("parallel", "arbitrary")),
        name="in_proj",
    )(x2d, g, w_bf16, cos_t, sin_t)


def _retention_kernel(dec_ref, qf_ref, kf_ref, vf_ref, qb_ref, kb_ref, vb_ref,
                      of_ref, ob_ref,
                      sf_ref, sb_ref, dtab_ref, xif_ref, xib_ref, zf_ref, zb_ref):
    first = jnp.logical_and(pl.program_id(0) == 0, pl.program_id(1) == 0)

    @pl.when(first)
    def _():
        diff = (lax.broadcasted_iota(I32, (CHUNK, CHUNK), 0)
                - lax.broadcasted_iota(I32, (CHUNK, CHUNK), 1)).astype(F32)
        row = lax.broadcasted_iota(I32, (CHUNK, HEAD_DIM), 0).astype(F32)
        for h in range(RET_HEADS):
            lgf = -jnp.exp(dec_ref[0:1, h:h + 1])
            lgb = -jnp.exp(dec_ref[1:2, h:h + 1])
            dtab_ref[h] = jnp.where(diff >= 0, jnp.exp(lgf * jnp.maximum(diff, 0.0)),
                                    jnp.exp(lgb * jnp.maximum(-diff, 0.0)))
            xif_ref[h] = jnp.exp(lgf * (row + 1.0))
            zf_ref[h] = jnp.exp(lgf * (CHUNK - 1.0 - row))
            xib_ref[h] = jnp.exp(lgb * (CHUNK - row))
            zb_ref[h] = jnp.exp(lgb * row)

    @pl.when(pl.program_id(1) == 0)
    def _():
        sf_ref[...] = jnp.zeros_like(sf_ref)
        sb_ref[...] = jnp.zeros_like(sb_ref)

    chunks = qf_ref.shape[0] // CHUNK
    nt = (((1,), (1,)), ((), ()))
    tn = (((0,), (0,)), ((), ()))
    for h in range(RET_HEADS):
        sl = slice(h * HEAD_DIM, (h + 1) * HEAD_DIM)
        gcf = jnp.exp(-jnp.exp(dec_ref[0:1, h:h + 1]) * CHUNK)
        gcb = jnp.exp(-jnp.exp(dec_ref[1:2, h:h + 1]) * CHUNK)

        for u in range(chunks):
            rows = slice(u * CHUNK, (u + 1) * CHUNK)
            q = qf_ref[rows, sl]
            k = kf_ref[rows, sl]
            v = vf_ref[rows, sl]
            s = lax.dot_general(q, k, nt, preferred_element_type=F32) * dtab_ref[h]
            sf = sf_ref[h]
            o = jnp.dot(s.astype(BF16), v, preferred_element_type=F32)
            o = o + jnp.dot(q, sf.astype(BF16), preferred_element_type=F32) * xif_ref[h]
            of_ref[rows, sl] = o
            kz = (k.astype(F32) * zf_ref[h]).astype(BF16)
            sf_ref[h] = sf * gcf + lax.dot_general(kz, v, tn, preferred_element_type=F32)

        for u in reversed(range(chunks)):
            rows = slice(u * CHUNK, (u + 1) * CHUNK)
            q = qb_ref[rows, sl]
            v = vb_ref[rows, sl]
            sb = sb_ref[h]
            ob_ref[rows, sl] = jnp.dot(q, sb.astype(BF16), preferred_element_type=F32) * xib_ref[h]
            kz = (kb_ref[rows, sl].astype(F32) * zb_ref[h]).astype(BF16)
            sb_ref[h] = sb * gcb + lax.dot_general(kz, v, tn, preferred_element_type=F32)


def _retention(proj, decays, batch, seq):
    t = proj.shape[0]
    rows = min(RET_CHUNKS * CHUNK, seq)
    n = seq // rows
    fwd = lambda col: pl.BlockSpec((rows, COL_BLOCK), lambda b, c: (b * n + c, col))
    bwd = lambda col: pl.BlockSpec((rows, COL_BLOCK), lambda b, c: (b * n + n - 1 - c, col))
    state = pltpu.VMEM((RET_HEADS, HEAD_DIM, HEAD_DIM), F32)
    decay = pltpu.VMEM((RET_HEADS, CHUNK, CHUNK), F32)
    per_row = pltpu.VMEM((RET_HEADS, CHUNK, HEAD_DIM), F32)
    return pl.pallas_call(
        _retention_kernel,
        out_shape=(jax.ShapeDtypeStruct((t, RET_WIDTH), F32), jax.ShapeDtypeStruct((t, RET_WIDTH), F32)),
        grid=(batch, n),
        in_specs=[pl.BlockSpec((2, RET_HEADS), lambda b, c: (0, 0)),
                  fwd(0), fwd(1), fwd(2), bwd(0), bwd(1), bwd(2)],
        out_specs=(pl.BlockSpec((rows, RET_WIDTH), lambda b, c: (b * n + c, 0)),
                   pl.BlockSpec((rows, RET_WIDTH), lambda b, c: (b * n + n - 1 - c, 0))),
        scratch_shapes=[state, state, decay, per_row, per_row, per_row, per_row],
        compiler_params=_params(("arbitrary", "arbitrary")),
        name="retention",
    )(decays, proj, proj, proj, proj, proj, proj)


def _mixout_kernel(of_ref, ob_ref, g_ref, cb_ref, cc_ref, ch_ref, ccp_ref, chp_ref, ccn_ref, chn_ref,
                   x_ref, gn_ref, cw_ref, wo_ref, g2_ref, wr_hi_ref, wr_lo_ref,
                   acc_ref, tok_ref, aff_ref, *, blocks_per_seq):
    i = pl.program_id(0)
    tm = x_ref.shape[0]

    gn = gn_ref[...]
    z = cc_ref[...].astype(F32) * ch_ref[...].astype(F32)
    pos = i % blocks_per_seq
    keep_prev = jnp.where(pos == 0, 0.0, 1.0).astype(F32)
    keep_next = jnp.where(pos == blocks_per_seq - 1, 0.0, 1.0).astype(F32)
    last = HALO_ROWS - 1
    halo_prev = (ccp_ref[...].astype(F32) * chp_ref[...].astype(F32))[last:, :] * keep_prev
    halo_next = (ccn_ref[...].astype(F32) * chn_ref[...].astype(F32))[0:1, :] * keep_next
    row = lax.broadcasted_iota(I32, z.shape, 0)
    z_prev = jnp.where(row == 0, halo_prev, pltpu.roll(z, 1, axis=0))
    z_next = jnp.where(row == tm - 1, halo_next, pltpu.roll(z, tm - 1, axis=0))
    cw = cw_ref[...]
    zc = z_prev * cw[0:1, :] + z * cw[1:2, :] + z_next * cw[2:3, :]
    wr_hi = wr_hi_ref[...]
    wr_lo = wr_lo_ref[...]

    rs = tm // MIX_SPLIT
    for s in range(MIX_SPLIT):
        rows = slice(s * rs, (s + 1) * rs)
        o = of_ref[rows, :] + ob_ref[rows, :]
        gate_in = g_ref[rows, :].astype(F32)
        parts = []
        for h in range(RET_HEADS):
            sl = slice(h * HEAD_DIM, (h + 1) * HEAD_DIM)
            oh = o[:, sl]
            mu = jnp.mean(oh, axis=-1, keepdims=True)
            d = oh - mu
            var = jnp.mean(d * d, axis=-1, keepdims=True)
            gh = gate_in[:, sl]
            swish = gh / (1.0 + jnp.exp(-gh))
            parts.append((swish * (d * lax.rsqrt(var + EPS) * gn[:, sl])).astype(BF16))
        ret = jnp.concatenate(parts, axis=-1)
        conv = cb_ref[rows, :].astype(F32) * zc[rows, :]

        y = jnp.dot(ret, wo_ref[0:RET_WIDTH, :], preferred_element_type=F32)
        y = y + jnp.dot(conv.astype(BF16), wo_ref[RET_WIDTH:, :], preferred_element_type=F32)
        x1 = x_ref[rows, :] + y
        h2 = _rmsnorm(x1, g2_ref[...])
        for k in range(ROW_TILES):
            cols = slice(k * LANES, (k + 1) * LANES)
            dst = pl.ds(s * rs * ROW_PITCH + k, rs, stride=ROW_PITCH)
            acc_ref[dst, :] = x1[:, cols]
            tok_ref[dst, :] = h2[:, cols]
        last_tile = pl.ds(s * rs * ROW_PITCH + ROW_TILES, rs, stride=ROW_PITCH)
        acc_ref[last_tile, :] = jnp.zeros((rs, LANES), F32)
        h_hi, h_lo = _split_bf16(h2)
        d = functools.partial(jnp.dot, preferred_element_type=F32)
        logits = d(h_hi, wr_hi) + d(h_hi, wr_lo) + d(h_lo, wr_hi)
        lane = lax.broadcasted_iota(I32, logits.shape, 1)
        logits = jnp.where(lane < N_EXPERTS, logits, -jnp.inf)
        ex = jnp.exp(logits - jnp.max(logits, axis=-1, keepdims=True))
        aff = ex / jnp.sum(ex, axis=-1, keepdims=True)
        tok_ref[last_tile, :] = aff
        aff_ref[rows, :] = aff[:, 0:N_EXPERTS]


def _mix_out(o_f, o_b, proj, x2d, gn_g, conv_w, wo_bf16, g2, wr_hi, wr_lo, seq):
    t = x2d.shape[0]
    tm = min(256, seq)
    bps = seq // tm
    r8 = tm // HALO_ROWS
    n8 = t // HALO_ROWS
    col = lambda c: pl.BlockSpec((tm, COL_BLOCK), lambda i: (i, c))
    prev = lambda c: pl.BlockSpec((HALO_ROWS, COL_BLOCK), lambda i: (jnp.maximum(i * r8 - 1, 0), c))
    nxt = lambda c: pl.BlockSpec((HALO_ROWS, COL_BLOCK), lambda i: (jnp.minimum((i + 1) * r8, n8 - 1), c))
    whole = lambda shape: pl.BlockSpec(shape, lambda i: (0,) * len(shape))
    return pl.pallas_call(
        functools.partial(_mixout_kernel, blocks_per_seq=bps),
        out_shape=(jax.ShapeDtypeStruct((t * ROW_PITCH, LANES), F32),
                   jax.ShapeDtypeStruct((t * ROW_PITCH, LANES), F32),
                   jax.ShapeDtypeStruct((t, N_EXPERTS), F32)),
        grid=(t // tm,),
        in_specs=[
            pl.BlockSpec((tm, RET_WIDTH), lambda i: (i, 0)),
            pl.BlockSpec((tm, RET_WIDTH), lambda i: (i, 0)),
            col(3), col(4), col(5), col(6), prev(5), prev(6), nxt(5), nxt(6),
            pl.BlockSpec((tm, D_MODEL), lambda i: (i, 0)),
            whole((1, RET_WIDTH)), whole((3, COL_BLOCK)), whole((D_MODEL, D_MODEL)),
            whole((1, D_MODEL)), whole((D_MODEL, LANES)), whole((D_MODEL, LANES)),
        ],
        out_specs=(pl.BlockSpec((tm * ROW_PITCH, LANES), lambda i: (i, 0)),
                   pl.BlockSpec((tm * ROW_PITCH, LANES), lambda i: (i, 0)),
                   pl.BlockSpec((tm, N_EXPERTS), lambda i: (i, 0))),
        compiler_params=_params(("parallel",)),
        name="mix_out",
    )(o_f, o_b, proj, proj, proj, proj, proj, proj, proj, proj, x2d, gn_g, conv_w, wo_bf16, g2, wr_hi, wr_lo)


def _select_kernel(aff_ref, idx_ref, sel_ref, *, cap):
    e_n, rows, _ = aff_ref.shape
    bits = lax.bitcast_convert_type(aff_ref[...], I32)

    def count(mask):
        c = jnp.sum(jnp.where(mask, 1.0, 0.0), axis=1, keepdims=True)
        return jnp.sum(c, axis=2, keepdims=True)

    thr = jnp.zeros((e_n, 1, 1), I32)
    for b in range(30, -1, -1):
        cand = thr | (1 << b)
        thr = jnp.where(count(bits >= cand) >= cap, cand, thr)

    gt = bits > thr
    eq = bits == thr
    need = cap - count(gt)

    li = lax.broadcasted_iota(I32, (LANES, LANES), 0)
    lj = lax.broadcasted_iota(I32, (LANES, LANES), 1)
    upper = (li <= lj).astype(BF16)
    ri = lax.broadcasted_iota(I32, (rows, rows), 0)
    rj = lax.broadcasted_iota(I32, (rows, rows), 1)
    lower = (rj < ri).astype(BF16)
    eqf = jnp.where(eq, 1.0, 0.0)
    for e in range(e_n):
        incl = jnp.dot(eqf[e].astype(BF16), upper, preferred_element_type=F32)
        row_tot = jnp.broadcast_to(incl[:, LANES - 1:LANES], (rows, LANES)).astype(BF16)
        row_off = jnp.dot(lower, row_tot, preferred_element_type=F32)
        rank = row_off + incl - eqf[e]
        take = jnp.logical_or(gt[e], jnp.logical_and(eq[e], rank < need[e]))
        sel_ref[e] = jnp.where(take, 1.0, 0.0)

    chunk = min(512, cap)
    lower_incl = (li >= lj).astype(BF16)
    slot0 = lax.broadcasted_iota(I32, (1, chunk), 1).astype(F32)
    rowid = lax.broadcasted_iota(I32, (rows, chunk), 0).astype(F32)

    def compact(e, carry):
        s = sel_ref[e].astype(BF16)
        q_t = lax.dot_general(lower_incl, s, (((1,), (1,)), ((), ())),
                              preferred_element_type=F32)
        q = jnp.dot(s, upper, preferred_element_type=F32)
        row_tot = q[:, LANES - 1:LANES]
        row_cum = jnp.dot(lower, jnp.broadcast_to(row_tot, (rows, LANES)).astype(BF16),
                          preferred_element_type=F32)[:, 0:1] + row_tot
        q_t = q_t.astype(BF16)
        for ch in range(cap // chunk):
            c = slot0 + float(ch * chunk)
            before = row_cum <= c
            r_c = jnp.sum(jnp.where(before, 1.0, 0.0), axis=0, keepdims=True)
            k = c - jnp.sum(jnp.where(before, row_tot, 0.0), axis=0, keepdims=True)
            onehot = jnp.where(rowid == r_c, 1.0, 0.0).astype(BF16)
            q_row = jnp.dot(q_t, onehot, preferred_element_type=F32)
            lane_c = jnp.sum(jnp.where(q_row <= k, 1.0, 0.0), axis=0, keepdims=True)
            token = r_c * LANES + lane_c
            idx_ref[pl.ds(e, 1), ch * chunk:(ch + 1) * chunk] = (token * ROW_PITCH).astype(I32)
        return carry

    lax.fori_loop(0, e_n, compact, 0)


def _select(aff3, cap):
    e_n, rows, lanes = aff3.shape
    return pl.pallas_call(
        functools.partial(_select_kernel, cap=cap),
        out_shape=jax.ShapeDtypeStruct((e_n, cap), I32),
        grid=(1,),
        in_specs=[pl.BlockSpec((e_n, rows, lanes), lambda i: (0, 0, 0))],
        out_specs=pl.BlockSpec((e_n, cap), lambda i: (0, 0)),
        scratch_shapes=[pltpu.VMEM((e_n, rows, lanes), F32)],
        compiler_params=_params(("arbitrary",)),
        name="select",
    )(aff3)


def _experts_kernel(idx_ref, idx_next_ref, tok_hbm, acc_in_hbm, wg_in, wu_in, wd_in, acc_hbm, *rest,
                    cast_weights):
    del acc_in_hbm
    if cast_weights:
        wg_ref, wu_ref, wd_ref, xbuf, xn_ref, yacc_ref, gate_ref, sem_x, sem_a, sem_s = rest
        wg_ref[...] = wg_in[...].astype(BF16)
        wu_ref[...] = wu_in[...].astype(BF16)
        wd_ref[...] = wd_in[...].astype(BF16)
    else:
        wg_ref, wu_ref, wd_ref = wg_in, wu_in, wd_in
        xbuf, xn_ref, yacc_ref, gate_ref, sem_x, sem_a, sem_s = rest
    e = pl.program_id(0)
    c = pl.program_id(1)
    f = pl.program_id(2)
    ne = pl.num_programs(0)
    nt = pl.num_programs(1)
    nf = pl.num_programs(2)
    tc = xn_ref.shape[0]
    g = e * nt + c
    slot = g % 2
    other = 1 - slot

    def start_gather(src_hbm, tiles, ids_ref, base, buf, sem, priority):
        def issue(r, carry):
            pltpu.make_async_copy(src_hbm.at[pl.ds(ids_ref[base + r], tiles), :],
                                  xbuf.at[buf, pl.ds(r * ROW_PITCH, tiles), :], sem.at[buf]).start(priority)
            return carry
        lax.fori_loop(0, tc, issue, 0, unroll=min(ISSUE_UNROLL, tc))

    def wait_gather(src_hbm, tiles, buf, sem):
        n = tc * tiles
        pltpu.make_async_copy(src_hbm.at[pl.ds(0, n), :], xbuf.at[buf, pl.ds(0, n), :], sem.at[buf]).wait()

    def wait_scatter(buf):
        n = tc * ROW_TILES
        pltpu.make_async_copy(xbuf.at[buf, pl.ds(0, n), :], acc_hbm.at[pl.ds(0, n), :], sem_s.at[buf]).wait()

    def tile_rows(r0, n, k):
        return pl.ds(r0 * ROW_PITCH + k, n, stride=ROW_PITCH)

    rb = min(EXPERT_ROWS, tc)

    @pl.when(jnp.logical_and(g == 0, f == 0))
    def _():
        start_gather(tok_hbm, ROW_PITCH, idx_ref, 0, 0, sem_x, TOK_PRIORITY)

    @pl.when(f == 0)
    def _():
        wait_gather(tok_hbm, ROW_PITCH, slot, sem_x)

        for r0 in range(0, tc, rb):
            for k in range(ROW_TILES):
                xn_ref[r0:r0 + rb, k * LANES:(k + 1) * LANES] = xbuf[slot, tile_rows(r0, rb, k), :].astype(BF16)
            aff = xbuf[slot, tile_rows(r0, rb, ROW_TILES), :]
            lane = lax.broadcasted_iota(I32, aff.shape, 1)
            gate_ref[r0:r0 + rb, :] = jnp.sum(jnp.where(lane == e, aff, 0.0), axis=-1, keepdims=True)
            yacc_ref[r0:r0 + rb, :] = jnp.zeros((rb, D_MODEL), F32)

        @pl.when(g > 0)
        def _():
            wait_scatter(other)

        start_gather(acc_hbm, ROW_TILES, idx_ref, c * tc, slot, sem_a, ACC_PRIORITY)

        @pl.when(c + 1 < nt)
        def _():
            start_gather(tok_hbm, ROW_PITCH, idx_ref, (c + 1) * tc, other, sem_x, TOK_PRIORITY)

        @pl.when(jnp.logical_and(c + 1 == nt, e + 1 < ne))
        def _():
            start_gather(tok_hbm, ROW_PITCH, idx_next_ref, 0, other, sem_x, TOK_PRIORITY)

    def ffn(m, carry):
        rows = pl.ds(pl.multiple_of(m * rb, rb), rb)
        xn = xn_ref[rows, :]
        hg = jnp.dot(xn, wg_ref[0], preferred_element_type=F32)
        hu = jnp.dot(xn, wu_ref[0], preferred_element_type=F32)
        hid = (hg / (1.0 + jnp.exp(-hg)) * hu).astype(BF16)
        for n0 in range(0, D_MODEL, DOWN_COLS):
            cols = slice(n0, n0 + DOWN_COLS)
            yacc_ref[rows, cols] += jnp.dot(hid, wd_ref[0, :, cols], preferred_element_type=F32)
        return carry
    lax.fori_loop(0, tc // rb, ffn, 0)

    @pl.when(f == nf - 1)
    def _():
        wait_gather(acc_hbm, ROW_TILES, slot, sem_a)

        for r0 in range(0, tc, rb):
            gate = gate_ref[r0:r0 + rb, :]
            for k in range(ROW_TILES):
                dst = tile_rows(r0, rb, k)
                xbuf[slot, dst, :] = xbuf[slot, dst, :] + yacc_ref[r0:r0 + rb, k * LANES:(k + 1) * LANES] * gate

        def issue(pair, carry):
            for p in range(2):
                r = 2 * pair + p
                pltpu.make_async_copy(xbuf.at[slot, pl.ds(r * ROW_PITCH, ROW_TILES), :],
                                      acc_hbm.at[pl.ds(idx_ref[c * tc + r], ROW_TILES), :],
                                      sem_s.at[slot]).start(p)
            return carry
        lax.fori_loop(0, tc // 2, issue, 0, unroll=min(ISSUE_UNROLL, tc) // 2)

        @pl.when(jnp.logical_and(e == ne - 1, c == nt - 1))
        def _():
            wait_scatter(slot)


def _experts(idx_flat, tok, acc, wg, wu, wd, cap):
    tc = min(EXPERT_TILE, cap)
    cast_weights = wg.dtype != BF16
    tf = 256 if cast_weights else 512
    assert not cast_weights or cap == tc, "each bf16 weight block must be written by exactly one grid step"
    w_specs = [
        pl.BlockSpec((1, D_MODEL, tf), lambda e, c, f: (e, 0, f)),
        pl.BlockSpec((1, D_MODEL, tf), lambda e, c, f: (e, 0, f)),
        pl.BlockSpec((1, tf, D_MODEL), lambda e, c, f: (e, f, 0)),
    ]
    acc_shape = jax.ShapeDtypeStruct(acc.shape, F32)
    acc_spec = pl.BlockSpec(memory_space=pl.ANY)
    w_shape = jax.ShapeDtypeStruct(wg.shape, BF16)
    out = pl.pallas_call(
        functools.partial(_experts_kernel, cast_weights=cast_weights),
        out_shape=(acc_shape, w_shape, w_shape, w_shape) if cast_weights else acc_shape,
        grid=(N_EXPERTS, cap // tc, EXPERT_FF // tf),
        in_specs=[
            pl.BlockSpec((cap,), lambda e, c, f: (e,), memory_space=pltpu.SMEM),
            pl.BlockSpec((cap,), lambda e, c, f: (jnp.minimum(e + 1, N_EXPERTS - 1),), memory_space=pltpu.SMEM),
            pl.BlockSpec(memory_space=pl.ANY),
            pl.BlockSpec(memory_space=pl.ANY),
            *w_specs,
        ],
        out_specs=(acc_spec, *w_specs) if cast_weights else acc_spec,
        scratch_shapes=[
            pltpu.VMEM((2, tc * ROW_PITCH, LANES), F32),
            pltpu.VMEM((tc, D_MODEL), BF16),
            pltpu.VMEM((tc, D_MODEL), F32),
            pltpu.VMEM((tc, 1), F32),
            pltpu.SemaphoreType.DMA((2,)),
            pltpu.SemaphoreType.DMA((2,)),
            pltpu.SemaphoreType.DMA((2,)),
        ],
        input_output_aliases={3: 0},
        compiler_params=_params(("arbitrary", "arbitrary", "arbitrary")),
        name="experts",
    )(idx_flat, idx_flat, tok, acc, wg, wu, wd)
    return out if cast_weights else (out, wg, wu, wd)


def _final_kernel(x_ref, g_ref, o_ref):
    tm = o_ref.shape[0]
    x = jnp.concatenate([x_ref[pl.ds(k, tm, stride=ROW_PITCH), :] for k in range(ROW_TILES)], axis=-1)
    o_ref[...] = _rmsnorm(x, g_ref[...])


def _final_norm(rows, g):
    t = rows.shape[0] // ROW_PITCH
    tm = min(512, t)
    return pl.pallas_call(
        _final_kernel,
        out_shape=jax.ShapeDtypeStruct((t, D_MODEL), F32),
        grid=(t // tm,),
        in_specs=[pl.BlockSpec((tm * ROW_PITCH, LANES), lambda i: (i, 0)),
                  pl.BlockSpec((1, D_MODEL), lambda i: (0, 0))],
        out_specs=pl.BlockSpec((tm, D_MODEL), lambda i: (i, 0)),
        compiler_params=_params(("parallel",)),
        name="final_norm",
    )(rows, g)


def _rotary_tables(seq):
    half = HEAD_DIM // 2
    inv = ROPE_BASE ** (-jnp.arange(half, dtype=F32) / half)
    ang = jnp.arange(seq, dtype=F32)[:, None] * inv[None, :]
    cos, sin = jnp.cos(ang), jnp.sin(ang)
    return jnp.concatenate([cos, cos], axis=-1), jnp.concatenate([-sin, sin], axis=-1)


def _trunk(x, w, expert_w):
    batch, seq, _ = x.shape
    t = batch * seq
    cap = EC_CAPACITY_FACTOR * t // N_EXPERTS
    x2d = x.reshape(t, D_MODEL)
    cos_t, sin_t = _rotary_tables(seq)

    proj = _in_proj(x2d, w["norm_mix_g"], w["w_in"], cos_t, sin_t, seq)
    o_f, o_b = _retention(proj, w["decays"], batch, seq)
    acc, tok, aff = _mix_out(o_f, o_b, proj, x2d, w["ret_gn_g"], w["conv_w"], w["w_o"],
                             w["norm_ffn_g"], *w["w_router_split"], seq)
    idx = _select(aff.T.reshape(N_EXPERTS, t // LANES, LANES), cap)
    acc, *expert_w = _experts(idx.reshape(N_EXPERTS * cap), tok, acc, *expert_w, cap)
    return _final_norm(acc, w["final_norm_g"]).reshape(batch, seq, D_MODEL), expert_w


def kernel(x_prompt, x_sample, norm_mix_g, w_in, conv_w, ret_decay_fwd, ret_decay_bwd, ret_gn_g, w_o,
           norm_ffn_g, w_router, w_gate, w_up, w_down, final_norm_g):
    w = {
        "norm_mix_g": norm_mix_g[0][None, :],
        "w_in": w_in[0].astype(BF16),
        "conv_w": conv_w[0],
        "decays": jnp.stack([ret_decay_fwd[0], ret_decay_bwd[0]]),
        "ret_gn_g": ret_gn_g[0][None, :],
        "w_o": w_o[0].astype(BF16),
        "norm_ffn_g": norm_ffn_g[0][None, :],
        "w_router_split": _split_bf16(jnp.pad(w_router[0], ((0, 0), (0, LANES - N_EXPERTS)))),
        "final_norm_g": final_norm_g[None, :],
    }
    y_prompt, expert_w = _trunk(x_prompt, w, (w_gate[0], w_up[0], w_down[0]))
    y_sample, _ = _trunk(x_sample, w, expert_w)
    return (y_prompt, y_sample)
```

```python
import functools

import jax
import jax.numpy as jnp
from jax import lax
from jax.experimental import pallas as pl
from jax.experimental.pallas import tpu as pltpu

F32 = jnp.float32
BF16 = jnp.bfloat16
I32 = jnp.int32

D_MODEL = 2048
RET_WIDTH = 1024
RET_HEADS = 8
HEAD_DIM = 128
IN_COLS = 7168
COL_BLOCK = 1024
N_EXPERTS = 16
EC_CAPACITY_FACTOR = 2
EXPERT_FF = 2048
CHUNK = 256
ROPE_BASE = 10000.0
EPS = 1e-6
LANES = 128
HALO_ROWS = 16
ROW_TILES = D_MODEL // LANES
ROW_PITCH = ROW_TILES + 1
RET_CHUNKS = 4
MIX_SPLIT = 2
EXPERT_TILE = 1024
EXPERT_ROWS = 1024
DOWN_COLS = 512
ISSUE_UNROLL = 64
TOK_PRIORITY = 0
ACC_PRIORITY = 1
VMEM_LIMIT = 56 * 1024 * 1024


def _params(sem, vmem=VMEM_LIMIT):
    return pltpu.CompilerParams(dimension_semantics=sem, vmem_limit_bytes=vmem)


def _rmsnorm(x, g):
    return x * lax.rsqrt(jnp.mean(x * x, axis=-1, keepdims=True) + EPS) * g


def _split_bf16(x):
    hi = x.astype(BF16)
    lo = (x - hi.astype(F32)).astype(BF16)
    return hi, lo


def _inproj_kernel(x_ref, g_ref, w_ref, cos_ref, sin_ref, o_ref, h_ref, acc_ref):
    j = pl.program_id(1)

    @pl.when(j == 0)
    def _():
        h_ref[...] = _rmsnorm(x_ref[...], g_ref[...]).astype(BF16)

    @pl.when(j < 2)
    def _():
        acc_ref[...] = jnp.dot(h_ref[...], w_ref[...], preferred_element_type=F32)
        scale = jnp.where(j == 1, HEAD_DIM ** -0.5, 1.0).astype(F32)
        cos = cos_ref[...]
        sin = sin_ref[...]
        for h in range(COL_BLOCK // HEAD_DIM):
            sl = slice(h * HEAD_DIM, (h + 1) * HEAD_DIM)
            t = acc_ref[:, sl]
            o_ref[:, sl] = ((t * cos + pltpu.roll(t, HEAD_DIM // 2, axis=1) * sin) * scale).astype(BF16)

    @pl.when(j >= 2)
    def _():
        o_ref[...] = jnp.dot(h_ref[...], w_ref[...], preferred_element_type=F32).astype(BF16)


def _in_proj(x2d, g, w_bf16, cos_t, sin_t, seq):
    t = x2d.shape[0]
    tm = min(1024, seq)
    blocks_per_seq = seq // tm
    return pl.pallas_call(
        _inproj_kernel,
        out_shape=jax.ShapeDtypeStruct((t, IN_COLS), BF16),
        grid=(t // tm, IN_COLS // COL_BLOCK),
        in_specs=[
            pl.BlockSpec((tm, D_MODEL), lambda i, j: (i, 0)),
            pl.BlockSpec((1, D_MODEL), lambda i, j: (0, 0)),
            pl.BlockSpec((D_MODEL, COL_BLOCK), lambda i, j: (0, j)),
            pl.BlockSpec((tm, HEAD_DIM), lambda i, j: (i % blocks_per_seq, 0)),
            pl.BlockSpec((tm, HEAD_DIM), lambda i, j: (i % blocks_per_seq, 0)),
        ],
        out_specs=pl.BlockSpec((tm, COL_BLOCK), lambda i, j: (i, j)),
        scratch_shapes=[pltpu.VMEM((tm, D_MODEL), BF16), pltpu.VMEM((tm, COL_BLOCK), F32)],
        compiler_params=_params(("parallel", "arbitrary")),
        name="in_proj",
    )(x2d, g, w_bf16, cos_t, sin_t)


def _retention_kernel(dec_ref, qf_ref, kf_ref, vf_ref, qb_ref, kb_ref, vb_ref,
                      of_ref, ob_ref,
                      sf_ref, sb_ref, dtab_ref, xif_ref, xib_ref, zf_ref, zb_ref):
    first = jnp.logical_and(pl.program_id(0) == 0, pl.program_id(1) == 0)

    @pl.when(first)
    def _():
        diff = (lax.broadcasted_iota(I32, (CHUNK, CHUNK), 0)
                - lax.broadcasted_iota(I32, (CHUNK, CHUNK), 1)).astype(F32)
        row = lax.broadcasted_iota(I32, (CHUNK, HEAD_DIM), 0).astype(F32)
        for h in range(RET_HEADS):
            lgf = -jnp.exp(dec_ref[0:1, h:h + 1])
            lgb = -jnp.exp(dec_ref[1:2, h:h + 1])
            dtab_ref[h] = jnp.where(diff >= 0, jnp.exp(lgf * jnp.maximum(diff, 0.0)),
                                    jnp.exp(lgb * jnp.maximum(-diff, 0.0)))
            xif_ref[h] = jnp.exp(lgf * (row + 1.0))
            zf_ref[h] = jnp.exp(lgf * (CHUNK - 1.0 - row))
            xib_ref[h] = jnp.exp(lgb * (CHUNK - row))
            zb_ref[h] = jnp.exp(lgb * row)

    @pl.when(pl.program_id(1) == 0)
    def _():
        sf_ref[...] = jnp.zeros_like(sf_ref)
        sb_ref[...] = jnp.zeros_like(sb_ref)

    chunks = qf_ref.shape[0] // CHUNK
    nt = (((1,), (1,)), ((), ()))
    tn = (((0,), (0,)), ((), ()))
    for h in range(RET_HEADS):
        sl = slice(h * HEAD_DIM, (h + 1) * HEAD_DIM)
        gcf = jnp.exp(-jnp.exp(dec_ref[0:1, h:h + 1]) * CHUNK)
        gcb = jnp.exp(-jnp.exp(dec_ref[1:2, h:h + 1]) * CHUNK)

        for u in range(chunks):
            rows = slice(u * CHUNK, (u + 1) * CHUNK)
            q = qf_ref[rows, sl]
            k = kf_ref[rows, sl]
            v = vf_ref[rows, sl]
            s = lax.dot_general(q, k, nt, preferred_element_type=F32) * dtab_ref[h]
            sf = sf_ref[h]
            o = jnp.dot(s.astype(BF16), v, preferred_element_type=F32)
            o = o + jnp.dot(q, sf.astype(BF16), preferred_element_type=F32) * xif_ref[h]
            of_ref[rows, sl] = o
            kz = (k.astype(F32) * zf_ref[h]).astype(BF16)
            sf_ref[h] = sf * gcf + lax.dot_general(kz, v, tn, preferred_element_type=F32)

        for u in reversed(range(chunks)):
            rows = slice(u * CHUNK, (u + 1) * CHUNK)
            q = qb_ref[rows, sl]
            v = vb_ref[rows, sl]
            sb = sb_ref[h]
            ob_ref[rows, sl] = jnp.dot(q, sb.astype(BF16), preferred_element_type=F32) * xib_ref[h]
            kz = (kb_ref[rows, sl].astype(F32) * zb_ref[h]).astype(BF16)
            sb_ref[h] = sb * gcb + lax.dot_general(kz, v, tn, preferred_element_type=F32)


def _retention(proj, decays, batch, seq):
    t = proj.shape[0]
    rows = min(RET_CHUNKS * CHUNK, seq)
    n = seq // rows
    fwd = lambda col: pl.BlockSpec((rows, COL_BLOCK), lambda b, c: (b * n + c, col))
    bwd = lambda col: pl.BlockSpec((rows, COL_BLOCK), lambda b, c: (b * n + n - 1 - c, col))
    state = pltpu.VMEM((RET_HEADS, HEAD_DIM, HEAD_DIM), F32)
    decay = pltpu.VMEM((RET_HEADS, CHUNK, CHUNK), F32)
    per_row = pltpu.VMEM((RET_HEADS, CHUNK, HEAD_DIM), F32)
    return pl.pallas_call(
        _retention_kernel,
        out_shape=(jax.ShapeDtypeStruct((t, RET_WIDTH), F32), jax.ShapeDtypeStruct((t, RET_WIDTH), F32)),
        grid=(batch, n),
        in_specs=[pl.BlockSpec((2, RET_HEADS), lambda b, c: (0, 0)),
                  fwd(0), fwd(1), fwd(2), bwd(0), bwd(1), bwd(2)],
        out_specs=(pl.BlockSpec((rows, RET_WIDTH), lambda b, c: (b * n + c, 0)),
                   pl.BlockSpec((rows, RET_WIDTH), lambda b, c: (b * n + n - 1 - c, 0))),
        scratch_shapes=[state, state, decay, per_row, per_row, per_row, per_row],
        compiler_params=_params(("arbitrary", "arbitrary")),
        name="retention",
    )(decays, proj, proj, proj, proj, proj, proj)


def _mixout_kernel(of_ref, ob_ref, g_ref, cb_ref, cc_ref, ch_ref, ccp_ref, chp_ref, ccn_ref, chn_ref,
                   x_ref, gn_ref, cw_ref, wo_ref, g2_ref, wr_hi_ref, wr_lo_ref,
                   acc_ref, tok_ref, aff_ref, *, blocks_per_seq):
    i = pl.program_id(0)
    tm = x_ref.shape[0]

    gn = gn_ref[...]
    z = cc_ref[...].astype(F32) * ch_ref[...].astype(F32)
    pos = i % blocks_per_seq
    last = HALO_ROWS - 1
    halo_prev = (ccp_ref[...].astype(F32) * chp_ref[...].astype(F32))[last:, :]
    halo_next = (ccn_ref[...].astype(F32) * chn_ref[...].astype(F32))[0:1, :]
    halo_prev = jnp.where(pos == 0, 0.0, halo_prev)
    halo_next = jnp.where(pos == blocks_per_seq - 1, 0.0, halo_next)
    row = lax.broadcasted_iota(I32, z.shape, 0)
    z_prev = jnp.where(row == 0, halo_prev, pltpu.roll(z, 1, axis=0))
    z_next = jnp.where(row == tm - 1, halo_next, pltpu.roll(z, tm - 1, axis=0))
    cw = cw_ref[...]
    zc = z_prev * cw[0:1, :] + z * cw[1:2, :] + z_next * cw[2:3, :]
    wr_hi = wr_hi_ref[...]
    wr_lo = wr_lo_ref[...]

    rs = tm // MIX_SPLIT
    for s in range(MIX_SPLIT):
        rows = slice(s * rs, (s + 1) * rs)
        o = of_ref[rows, :] + ob_ref[rows, :]
        gate_in = g_ref[rows, :].astype(F32)
        parts = []
        for h in range(RET_HEADS):
            sl = slice(h * HEAD_DIM, (h + 1) * HEAD_DIM)
            oh = o[:, sl]
            mu = jnp.mean(oh, axis=-1, keepdims=True)
            d = oh - mu
            var = jnp.mean(d * d, axis=-1, keepdims=True)
            gh = gate_in[:, sl]
            swish = gh / (1.0 + jnp.exp(-gh))
            parts.append((swish * (d * lax.rsqrt(var + EPS) * gn[:, sl])).astype(BF16))
        ret = jnp.concatenate(parts, axis=-1)
        conv = cb_ref[rows, :].astype(F32) * zc[rows, :]

        y = jnp.dot(ret, wo_ref[0:RET_WIDTH, :], preferred_element_type=F32)
        y = y + jnp.dot(conv.astype(BF16), wo_ref[RET_WIDTH:, :], preferred_element_type=F32)
        x1 = x_ref[rows, :] + y
        h2 = _rmsnorm(x1, g2_ref[...])
        for k in range(ROW_TILES):
            cols = slice(k * LANES, (k + 1) * LANES)
            dst = pl.ds(s * rs * ROW_PITCH + k, rs, stride=ROW_PITCH)
            acc_ref[dst, :] = x1[:, cols]
            tok_ref[dst, :] = h2[:, cols]
        last_tile = pl.ds(s * rs * ROW_PITCH + ROW_TILES, rs, stride=ROW_PITCH)
        acc_ref[last_tile, :] = jnp.zeros((rs, LANES), F32)
        h_hi, h_lo = _split_bf16(h2)
        d = functools.partial(jnp.dot, preferred_element_type=F32)
        logits = d(h_hi, wr_hi) + d(h_hi, wr_lo) + d(h_lo, wr_hi)
        lane = lax.broadcasted_iota(I32, logits.shape, 1)
        logits = jnp.where(lane < N_EXPERTS, logits, -jnp.inf)
        ex = jnp.exp(logits - jnp.max(logits, axis=-1, keepdims=True))
        aff = ex / jnp.sum(ex, axis=-1, keepdims=True)
        tok_ref[last_tile, :] = aff
        aff_ref[rows, :] = aff[:, 0:N_EXPERTS]


def _mix_out(o_f, o_b, proj, x2d, gn_g, conv_w, wo_bf16, g2, wr_hi, wr_lo, seq):
    t = x2d.shape[0]
    tm = min(256, seq)
    bps = seq // tm
    r8 = tm // HALO_ROWS
    n8 = t // HALO_ROWS
    col = lambda c: pl.BlockSpec((tm, COL_BLOCK), lambda i: (i, c))
    prev = lambda c: pl.BlockSpec((HALO_ROWS, COL_BLOCK), lambda i: (jnp.maximum(i * r8 - 1, 0), c))
    nxt = lambda c: pl.BlockSpec((HALO_ROWS, COL_BLOCK), lambda i: (jnp.minimum((i + 1) * r8, n8 - 1), c))
    whole = lambda shape: pl.BlockSpec(shape, lambda i: (0,) * len(shape))
    return pl.pallas_call(
        functools.partial(_mixout_kernel, blocks_per_seq=bps),
        out_shape=(jax.ShapeDtypeStruct((t * ROW_PITCH, LANES), F32),
                   jax.ShapeDtypeStruct((t * ROW_PITCH, LANES), F32),
                   jax.ShapeDtypeStruct((t, N_EXPERTS), F32)),
        grid=(t // tm,),
        in_specs=[
            pl.BlockSpec((tm, RET_WIDTH), lambda i: (i, 0)),
            pl.BlockSpec((tm, RET_WIDTH), lambda i: (i, 0)),
            col(3), col(4), col(5), col(6), prev(5), prev(6), nxt(5), nxt(6),
            pl.BlockSpec((tm, D_MODEL), lambda i: (i, 0)),
            whole((1, RET_WIDTH)), whole((3, COL_BLOCK)), whole((D_MODEL, D_MODEL)),
            whole((1, D_MODEL)), whole((D_MODEL, LANES)), whole((D_MODEL, LANES)),
        ],
        out_specs=(pl.BlockSpec((tm * ROW_PITCH, LANES), lambda i: (i, 0)),
                   pl.BlockSpec((tm * ROW_PITCH, LANES), lambda i: (i, 0)),
                   pl.BlockSpec((tm, N_EXPERTS), lambda i: (i, 0))),
        compiler_params=_params(("parallel",)),
        name="mix_out",
    )(o_f, o_b, proj, proj, proj, proj, proj, proj, proj, proj, x2d, gn_g, conv_w, wo_bf16, g2, wr_hi, wr_lo)


def _select_kernel(aff_ref, idx_ref, sel_ref, *, cap):
    e_n, rows, _ = aff_ref.shape
    bits = lax.bitcast_convert_type(aff_ref[...], I32)

    def count(mask):
        c = jnp.sum(jnp.where(mask, 1.0, 0.0), axis=1, keepdims=True)
        return jnp.sum(c, axis=2, keepdims=True)

    thr = jnp.zeros((e_n, 1, 1), I32)
    for b in range(30, -1, -1):
        cand = thr | (1 << b)
        thr = jnp.where(count(bits >= cand) >= cap, cand, thr)

    gt = bits > thr
    eq = bits == thr
    need = cap - count(gt)

    li = lax.broadcasted_iota(I32, (LANES, LANES), 0)
    lj = lax.broadcasted_iota(I32, (LANES, LANES), 1)
    upper = (li <= lj).astype(BF16)
    ri = lax.broadcasted_iota(I32, (rows, rows), 0)
    rj = lax.broadcasted_iota(I32, (rows, rows), 1)
    lower = (rj < ri).astype(BF16)
    eqf = jnp.where(eq, 1.0, 0.0)
    for e in range(e_n):
        incl = jnp.dot(eqf[e].astype(BF16), upper, preferred_element_type=F32)
        row_tot = jnp.broadcast_to(incl[:, LANES - 1:LANES], (rows, LANES)).astype(BF16)
        row_off = jnp.dot(lower, row_tot, preferred_element_type=F32)
        rank = row_off + incl - eqf[e]
        take = jnp.logical_or(gt[e], jnp.logical_and(eq[e], rank < need[e]))
        sel_ref[e] = jnp.where(take, 1.0, 0.0)

    chunk = min(512, cap)
    lower_incl = (li >= lj).astype(BF16)
    slot0 = lax.broadcasted_iota(I32, (1, chunk), 1).astype(F32)
    rowid = lax.broadcasted_iota(I32, (rows, chunk), 0).astype(F32)

    def compact(e, carry):
        s = sel_ref[e].astype(BF16)
        q_t = lax.dot_general(lower_incl, s, (((1,), (1,)), ((), ())),
                              preferred_element_type=F32)
        q = jnp.dot(s, upper, preferred_element_type=F32)
        row_tot = q[:, LANES - 1:LANES]
        row_cum = jnp.dot(lower, jnp.broadcast_to(row_tot, (rows, LANES)).astype(BF16),
                          preferred_element_type=F32)[:, 0:1] + row_tot
        q_t = q_t.astype(BF16)
        for ch in range(cap // chunk):
            c = slot0 + float(ch * chunk)
            before = row_cum <= c
            r_c = jnp.sum(jnp.where(before, 1.0, 0.0), axis=0, keepdims=True)
            k = c - jnp.sum(jnp.where(before, row_tot, 0.0), axis=0, keepdims=True)
            onehot = jnp.where(rowid == r_c, 1.0, 0.0).astype(BF16)
            q_row = jnp.dot(q_t, onehot, preferred_element_type=F32)
            lane_c = jnp.sum(jnp.where(q_row <= k, 1.0, 0.0), axis=0, keepdims=True)
            token = r_c * LANES + lane_c
            idx_ref[pl.ds(e, 1), ch * chunk:(ch + 1) * chunk] = (token * ROW_PITCH).astype(I32)
        return carry

    lax.fori_loop(0, e_n, compact, 0)


def _select(aff3, cap):
    e_n, rows, lanes = aff3.shape
    return pl.pallas_call(
        functools.partial(_select_kernel, cap=cap),
        out_shape=jax.ShapeDtypeStruct((e_n, cap), I32),
        grid=(1,),
        in_specs=[pl.BlockSpec((e_n, rows, lanes), lambda i: (0, 0, 0))],
        out_specs=pl.BlockSpec((e_n, cap), lambda i: (0, 0)),
        scratch_shapes=[pltpu.VMEM((e_n, rows, lanes), F32)],
        compiler_params=_params(("arbitrary",)),
        name="select",
    )(aff3)


def _experts_kernel(idx_ref, idx_next_ref, tok_hbm, acc_in_hbm, wg_in, wu_in, wd_in, acc_hbm, *rest,
                    cast_weights):
    del acc_in_hbm
    if cast_weights:
        wg_ref, wu_ref, wd_ref, xbuf, xn_ref, yacc_ref, gate_ref, sem_x, sem_a, sem_s = rest
        wg_ref[...] = wg_in[...].astype(BF16)
        wu_ref[...] = wu_in[...].astype(BF16)
        wd_ref[...] = wd_in[...].astype(BF16)
    else:
        wg_ref, wu_ref, wd_ref = wg_in, wu_in, wd_in
        xbuf, xn_ref, yacc_ref, gate_ref, sem_x, sem_a, sem_s = rest
    e = pl.program_id(0)
    c = pl.program_id(1)
    f = pl.program_id(2)
    ne = pl.num_programs(0)
    nt = pl.num_programs(1)
    nf = pl.num_programs(2)
    tc = xn_ref.shape[0]
    g = e * nt + c
    slot = g % 2
    other = 1 - slot

    def start_gather(src_hbm, tiles, ids_ref, base, buf, sem, priority):
        def issue(r, carry):
            pltpu.make_async_copy(src_hbm.at[pl.ds(ids_ref[base + r], tiles), :],
                                  xbuf.at[buf, pl.ds(r * ROW_PITCH, tiles), :], sem.at[buf]).start(priority)
            return carry
        lax.fori_loop(0, tc, issue, 0, unroll=min(ISSUE_UNROLL, tc))

    def wait_gather(src_hbm, tiles, buf, sem):
        n = tc * tiles
        pltpu.make_async_copy(src_hbm.at[pl.ds(0, n), :], xbuf.at[buf, pl.ds(0, n), :], sem.at[buf]).wait()

    def wait_scatter(buf):
        n = tc * ROW_TILES
        pltpu.make_async_copy(xbuf.at[buf, pl.ds(0, n), :], acc_hbm.at[pl.ds(0, n), :], sem_s.at[buf]).wait()

    def tile_rows(r0, n, k):
        return pl.ds(r0 * ROW_PITCH + k, n, stride=ROW_PITCH)

    rb = min(EXPERT_ROWS, tc)

    @pl.when(jnp.logical_and(g == 0, f == 0))
    def _():
        start_gather(tok_hbm, ROW_PITCH, idx_ref, 0, 0, sem_x, TOK_PRIORITY)

    @pl.when(f == 0)
    def _():
        wait_gather(tok_hbm, ROW_PITCH, slot, sem_x)

        for r0 in range(0, tc, rb):
            for k in range(ROW_TILES):
                xn_ref[r0:r0 + rb, k * LANES:(k + 1) * LANES] = xbuf[slot, tile_rows(r0, rb, k), :].astype(BF16)
            aff = xbuf[slot, tile_rows(r0, rb, ROW_TILES), :]
            lane = lax.broadcasted_iota(I32, aff.shape, 1)
            gate_ref[r0:r0 + rb, :] = jnp.sum(jnp.where(lane == e, aff, 0.0), axis=-1, keepdims=True)
            yacc_ref[r0:r0 + rb, :] = jnp.zeros((rb, D_MODEL), F32)

        @pl.when(g > 0)
        def _():
            wait_scatter(other)

        start_gather(acc_hbm, ROW_TILES, idx_ref, c * tc, slot, sem_a, ACC_PRIORITY)

        @pl.when(c + 1 < nt)
        def _():
            start_gather(tok_hbm, ROW_PITCH, idx_ref, (c + 1) * tc, other, sem_x, TOK_PRIORITY)

        @pl.when(jnp.logical_and(c + 1 == nt, e + 1 < ne))
        def _():
            start_gather(tok_hbm, ROW_PITCH, idx_next_ref, 0, other, sem_x, TOK_PRIORITY)

    def ffn(m, carry):
        rows = pl.ds(pl.multiple_of(m * rb, rb), rb)
        xn = xn_ref[rows, :]
        hg = jnp.dot(xn, wg_ref[0], preferred_element_type=F32)
        hu = jnp.dot(xn, wu_ref[0], preferred_element_type=F32)
        hid = (hg / (1.0 + jnp.exp(-hg)) * hu).astype(BF16)
        for n0 in range(0, D_MODEL, DOWN_COLS):
            cols = slice(n0, n0 + DOWN_COLS)
            yacc_ref[rows, cols] += jnp.dot(hid, wd_ref[0, :, cols], preferred_element_type=F32)
        return carry
    lax.fori_loop(0, tc // rb, ffn, 0)

    @pl.when(f == nf - 1)
    def _():
        wait_gather(acc_hbm, ROW_TILES, slot, sem_a)

        for r0 in range(0, tc, rb):
            gate = gate_ref[r0:r0 + rb, :]
            for k in range(ROW_TILES):
                dst = tile_rows(r0, rb, k)
                xbuf[slot, dst, :] = xbuf[slot, dst, :] + yacc_ref[r0:r0 + rb, k * LANES:(k + 1) * LANES] * gate

        def issue(pair, carry):
            for p in range(2):
                r = 2 * pair + p
                pltpu.make_async_copy(xbuf.at[slot, pl.ds(r * ROW_PITCH, ROW_TILES), :],
                                      acc_hbm.at[pl.ds(idx_ref[c * tc + r], ROW_TILES), :],
                                      sem_s.at[slot]).start(p)
            return carry
        lax.fori_loop(0, tc // 2, issue, 0, unroll=min(ISSUE_UNROLL, tc) // 2)

        @pl.when(jnp.logical_and(e == ne - 1, c == nt - 1))
        def _():
            wait_scatter(slot)


def _experts(idx_flat, tok, acc, wg, wu, wd, cap):
    tc = min(EXPERT_TILE, cap)
    cast_weights = wg.dtype != BF16
    tf = 256 if cast_weights else 512
    assert not cast_weights or cap == tc, "each bf16 weight block must be written by exactly one grid step"
    w_specs = [
        pl.BlockSpec((1, D_MODEL, tf), lambda e, c, f: (e, 0, f)),
        pl.BlockSpec((1, D_MODEL, tf), lambda e, c, f: (e, 0, f)),
        pl.BlockSpec((1, tf, D_MODEL), lambda e, c, f: (e, f, 0)),
    ]
    acc_shape = jax.ShapeDtypeStruct(acc.shape, F32)
    acc_spec = pl.BlockSpec(memory_space=pl.ANY)
    w_shape = jax.ShapeDtypeStruct(wg.shape, BF16)
    out = pl.pallas_call(
        functools.partial(_experts_kernel, cast_weights=cast_weights),
        out_shape=(acc_shape, w_shape, w_shape, w_shape) if cast_weights else acc_shape,
        grid=(N_EXPERTS, cap // tc, EXPERT_FF // tf),
        in_specs=[
            pl.BlockSpec((cap,), lambda e, c, f: (e,), memory_space=pltpu.SMEM),
            pl.BlockSpec((cap,), lambda e, c, f: (jnp.minimum(e + 1, N_EXPERTS - 1),), memory_space=pltpu.SMEM),
            pl.BlockSpec(memory_space=pl.ANY),
            pl.BlockSpec(memory_space=pl.ANY),
            *w_specs,
        ],
        out_specs=(acc_spec, *w_specs) if cast_weights else acc_spec,
        scratch_shapes=[
            pltpu.VMEM((2, tc * ROW_PITCH, LANES), F32),
            pltpu.VMEM((tc, D_MODEL), BF16),
            pltpu.VMEM((tc, D_MODEL), F32),
            pltpu.VMEM((tc, 1), F32),
            pltpu.SemaphoreType.DMA((2,)),
            pltpu.SemaphoreType.DMA((2,)),
            pltpu.SemaphoreType.DMA((2,)),
        ],
        input_output_aliases={3: 0},
        compiler_params=_params(("arbitrary", "arbitrary", "arbitrary")),
        name="experts",
    )(idx_flat, idx_flat, tok, acc, wg, wu, wd)
    return out if cast_weights else (out, wg, wu, wd)


def _final_kernel(x_ref, g_ref, o_ref):
    tm = o_ref.shape[0]
    x = jnp.concatenate([x_ref[pl.ds(k, tm, stride=ROW_PITCH), :] for k in range(ROW_TILES)], axis=-1)
    o_ref[...] = _rmsnorm(x, g_ref[...])


def _final_norm(rows, g):
    t = rows.shape[0] // ROW_PITCH
    tm = min(512, t)
    return pl.pallas_call(
        _final_kernel,
        out_shape=jax.ShapeDtypeStruct((t, D_MODEL), F32),
        grid=(t // tm,),
        in_specs=[pl.BlockSpec((tm * ROW_PITCH, LANES), lambda i: (i, 0)),
                  pl.BlockSpec((1, D_MODEL), lambda i: (0, 0))],
        out_specs=pl.BlockSpec((tm, D_MODEL), lambda i: (i, 0)),
        compiler_params=_params(("parallel",)),
        name="final_norm",
    )(rows, g)


def _rotary_tables(seq):
    half = HEAD_DIM // 2
    inv = ROPE_BASE ** (-jnp.arange(half, dtype=F32) / half)
    ang = jnp.arange(seq, dtype=F32)[:, None] * inv[None, :]
    cos, sin = jnp.cos(ang), jnp.sin(ang)
    return jnp.concatenate([cos, cos], axis=-1), jnp.concatenate([-sin, sin], axis=-1)


def _trunk(x, w, expert_w):
    batch, seq, _ = x.shape
    t = batch * seq
    cap = EC_CAPACITY_FACTOR * t // N_EXPERTS
    x2d = x.reshape(t, D_MODEL)
    cos_t, sin_t = _rotary_tables(seq)

    proj = _in_proj(x2d, w["norm_mix_g"], w["w_in"], cos_t, sin_t, seq)
    o_f, o_b = _retention(proj, w["decays"], batch, seq)
    acc, tok, aff = _mix_out(o_f, o_b, proj, x2d, w["ret_gn_g"], w["conv_w"], w["w_o"],
                             w["norm_ffn_g"], *w["w_router_split"], seq)
    idx = _select(aff.T.reshape(N_EXPERTS, t // LANES, LANES), cap)
    acc, *expert_w = _experts(idx.reshape(N_EXPERTS * cap), tok, acc, *expert_w, cap)
    return _final_norm(acc, w["final_norm_g"]).reshape(batch, seq, D_MODEL), expert_w


def kernel(x_prompt, x_sample, norm_mix_g, w_in, conv_w, ret_decay_fwd, ret_decay_bwd, ret_gn_g, w_o,
           norm_ffn_g, w_router, w_gate, w_up, w_down, final_norm_g):
    assert w_in.shape[0] == 1, "one encoder layer"
    w = {
        "norm_mix_g": norm_mix_g[0][None, :],
        "w_in": w_in[0].astype(BF16),
        "conv_w": conv_w[0],
        "decays": jnp.stack([ret_decay_fwd[0], ret_decay_bwd[0]]),
        "ret_gn_g": ret_gn_g[0][None, :],
        "w_o": w_o[0].astype(BF16),
        "norm_ffn_g": norm_ffn_g[0][None, :],
        "w_router_split": _split_bf16(jnp.pad(w_router[0], ((0, 0), (0, LANES - N_EXPERTS)))),
        "final_norm_g": final_norm_g[None, :],
    }
    y_prompt, expert_w = _trunk(x_prompt, w, (w_gate[0], w_up[0], w_down[0]))
    y_sample, _ = _trunk(x_sample, w, expert_w)
    return (y_prompt, y_sample)
```

```python
import functools

import jax
import jax.numpy as jnp
from jax import lax
from jax.experimental import pallas as pl
from jax.experimental.pallas import tpu as pltpu

F32 = jnp.float32
BF16 = jnp.bfloat16
I32 = jnp.int32

D_MODEL = 2048
RET_WIDTH = 1024
RET_HEADS = 8
HEAD_DIM = 128
IN_COLS = 7168
COL_BLOCK = 1024
N_EXPERTS = 16
EC_CAPACITY_FACTOR = 2
EXPERT_FF = 2048
CHUNK = 256
ROPE_BASE = 10000.0
EPS = 1e-6
LANES = 128
HALO_ROWS = 16
ROW_TILES = D_MODEL // LANES
ROW_PITCH = ROW_TILES + 1
RET_CHUNKS = 4
MIX_SPLIT = 2
EXPERT_TILE = 1024
EXPERT_ROWS = 1024
DOWN_COLS = 512
COMBINE_ROWS = 256
ISSUE_UNROLL = 64
TOK_PRIORITY = 0
ACC_PRIORITY = 1
VMEM_LIMIT = 56 * 1024 * 1024


def _params(sem, vmem=VMEM_LIMIT):
    return pltpu.CompilerParams(dimension_semantics=sem, vmem_limit_bytes=vmem)


def _rmsnorm(x, g):
    return x * lax.rsqrt(jnp.mean(x * x, axis=-1, keepdims=True) + EPS) * g


def _split_bf16(x):
    hi = x.astype(BF16)
    lo = (x - hi.astype(F32)).astype(BF16)
    return hi, lo


def _inproj_kernel(x_ref, g_ref, w_ref, cos_ref, sin_ref, o_ref, h_ref, acc_ref):
    j = pl.program_id(1)

    @pl.when(j == 0)
    def _():
        h_ref[...] = _rmsnorm(x_ref[...], g_ref[...]).astype(BF16)

    @pl.when(j < 2)
    def _():
        acc_ref[...] = jnp.dot(h_ref[...], w_ref[...], preferred_element_type=F32)
        scale = jnp.where(j == 1, HEAD_DIM ** -0.5, 1.0).astype(F32)
        cos = cos_ref[...]
        sin = sin_ref[...]
        for h in range(COL_BLOCK // HEAD_DIM):
            sl = slice(h * HEAD_DIM, (h + 1) * HEAD_DIM)
            t = acc_ref[:, sl]
            o_ref[:, sl] = ((t * cos + pltpu.roll(t, HEAD_DIM // 2, axis=1) * sin) * scale).astype(BF16)

    @pl.when(j >= 2)
    def _():
        o_ref[...] = jnp.dot(h_ref[...], w_ref[...], preferred_element_type=F32).astype(BF16)


def _in_proj(x2d, g, w_bf16, cos_t, sin_t, seq):
    t = x2d.shape[0]
    tm = min(1024, seq)
    blocks_per_seq = seq // tm
    return pl.pallas_call(
        _inproj_kernel,
        out_shape=jax.ShapeDtypeStruct((t, IN_COLS), BF16),
        grid=(t // tm, IN_COLS // COL_BLOCK),
        in_specs=[
            pl.BlockSpec((tm, D_MODEL), lambda i, j: (i, 0)),
            pl.BlockSpec((1, D_MODEL), lambda i, j: (0, 0)),
            pl.BlockSpec((D_MODEL, COL_BLOCK), lambda i, j: (0, j)),
            pl.BlockSpec((tm, HEAD_DIM), lambda i, j: (i % blocks_per_seq, 0)),
            pl.BlockSpec((tm, HEAD_DIM), lambda i, j: (i % blocks_per_seq, 0)),
        ],
        out_specs=pl.BlockSpec((tm, COL_BLOCK), lambda i, j: (i, j)),
        scratch_shapes=[pltpu.VMEM((tm, D_MODEL), BF16), pltpu.VMEM((tm, COL_BLOCK), F32)],
        compiler_params=_params(("parallel", "arbitrary")),
        name="in_proj",
    )(x2d, g, w_bf16, cos_t, sin_t)


def _retention_kernel(dec_ref, qf_ref, kf_ref, vf_ref, qb_ref, kb_ref, vb_ref,
                      of_ref, ob_ref,
                      sf_ref, sb_ref, dtab_ref, xif_ref, xib_ref, zf_ref, zb_ref):
    first = jnp.logical_and(pl.program_id(0) == 0, pl.program_id(1) == 0)

    @pl.when(first)
    def _():
        diff = (lax.broadcasted_iota(I32, (CHUNK, CHUNK), 0)
                - lax.broadcasted_iota(I32, (CHUNK, CHUNK), 1)).astype(F32)
        row = lax.broadcasted_iota(I32, (CHUNK, HEAD_DIM), 0).astype(F32)
        for h in range(RET_HEADS):
            lgf = -jnp.exp(dec_ref[0:1, h:h + 1])
            lgb = -jnp.exp(dec_ref[1:2, h:h + 1])
            dtab_ref[h] = jnp.where(diff >= 0, jnp.exp(lgf * jnp.maximum(diff, 0.0)),
                                    jnp.exp(lgb * jnp.maximum(-diff, 0.0)))
            xif_ref[h] = jnp.exp(lgf * (row + 1.0))
            zf_ref[h] = jnp.exp(lgf * (CHUNK - 1.0 - row))
            xib_ref[h] = jnp.exp(lgb * (CHUNK - row))
            zb_ref[h] = jnp.exp(lgb * row)

    @pl.when(pl.program_id(1) == 0)
    def _():
        sf_ref[...] = jnp.zeros_like(sf_ref)
        sb_ref[...] = jnp.zeros_like(sb_ref)

    chunks = qf_ref.shape[0] // CHUNK
    nt = (((1,), (1,)), ((), ()))
    tn = (((0,), (0,)), ((), ()))
    for h in range(RET_HEADS):
        sl = slice(h * HEAD_DIM, (h + 1) * HEAD_DIM)
        gcf = jnp.exp(-jnp.exp(dec_ref[0:1, h:h + 1]) * CHUNK)
        gcb = jnp.exp(-jnp.exp(dec_ref[1:2, h:h + 1]) * CHUNK)

        for u in range(chunks):
            rows = slice(u * CHUNK, (u + 1) * CHUNK)
            q = qf_ref[rows, sl]
            k = kf_ref[rows, sl]
            v = vf_ref[rows, sl]
            s = lax.dot_general(q, k, nt, preferred_element_type=F32) * dtab_ref[h]
            sf = sf_ref[h]
            o = jnp.dot(s.astype(BF16), v, preferred_element_type=F32)
            o = o + jnp.dot(q, sf.astype(BF16), preferred_element_type=F32) * xif_ref[h]
            of_ref[rows, sl] = o
            kz = (k.astype(F32) * zf_ref[h]).astype(BF16)
            sf_ref[h] = sf * gcf + lax.dot_general(kz, v, tn, preferred_element_type=F32)

        for u in reversed(range(chunks)):
            rows = slice(u * CHUNK, (u + 1) * CHUNK)
            q = qb_ref[rows, sl]
            v = vb_ref[rows, sl]
            sb = sb_ref[h]
            ob_ref[rows, sl] = jnp.dot(q, sb.astype(BF16), preferred_element_type=F32) * xib_ref[h]
            kz = (kb_ref[rows, sl].astype(F32) * zb_ref[h]).astype(BF16)
            sb_ref[h] = sb * gcb + lax.dot_general(kz, v, tn, preferred_element_type=F32)


def _retention(proj, decays, batch, seq):
    t = proj.shape[0]
    rows = min(RET_CHUNKS * CHUNK, seq)
    n = seq // rows
    fwd = lambda col: pl.BlockSpec((rows, COL_BLOCK), lambda b, c: (b * n + c, col))
    bwd = lambda col: pl.BlockSpec((rows, COL_BLOCK), lambda b, c: (b * n + n - 1 - c, col))
    state = pltpu.VMEM((RET_HEADS, HEAD_DIM, HEAD_DIM), F32)
    decay = pltpu.VMEM((RET_HEADS, CHUNK, CHUNK), F32)
    per_row = pltpu.VMEM((RET_HEADS, CHUNK, HEAD_DIM), F32)
    return pl.pallas_call(
        _retention_kernel,
        out_shape=(jax.ShapeDtypeStruct((t, RET_WIDTH), F32), jax.ShapeDtypeStruct((t, RET_WIDTH), F32)),
        grid=(batch, n),
        in_specs=[pl.BlockSpec((2, RET_HEADS), lambda b, c: (0, 0)),
                  fwd(0), fwd(1), fwd(2), bwd(0), bwd(1), bwd(2)],
        out_specs=(pl.BlockSpec((rows, RET_WIDTH), lambda b, c: (b * n + c, 0)),
                   pl.BlockSpec((rows, RET_WIDTH), lambda b, c: (b * n + n - 1 - c, 0))),
        scratch_shapes=[state, state, decay, per_row, per_row, per_row, per_row],
        compiler_params=_params(("arbitrary", "arbitrary")),
        name="retention",
    )(decays, proj, proj, proj, proj, proj, proj)


def _mixout_kernel(of_ref, ob_ref, g_ref, cb_ref, cc_ref, ch_ref, ccp_ref, chp_ref, ccn_ref, chn_ref,
                   x_ref, gn_ref, cw_ref, wo_ref, g2_ref, wr_hi_ref, wr_lo_ref,
                   acc_ref, tok_ref, aff_ref, *, blocks_per_seq):
    i = pl.program_id(0)
    tm = x_ref.shape[0]

    gn = gn_ref[...]
    z = cc_ref[...].astype(F32) * ch_ref[...].astype(F32)
    pos = i % blocks_per_seq
    last = HALO_ROWS - 1
    halo_prev = (ccp_ref[...].astype(F32) * chp_ref[...].astype(F32))[last:, :]
    halo_next = (ccn_ref[...].astype(F32) * chn_ref[...].astype(F32))[0:1, :]
    halo_prev = jnp.where(pos == 0, 0.0, halo_prev)
    halo_next = jnp.where(pos == blocks_per_seq - 1, 0.0, halo_next)
    row = lax.broadcasted_iota(I32, z.shape, 0)
    z_prev = jnp.where(row == 0, halo_prev, pltpu.roll(z, 1, axis=0))
    z_next = jnp.where(row == tm - 1, halo_next, pltpu.roll(z, tm - 1, axis=0))
    cw = cw_ref[...]
    zc = z_prev * cw[0:1, :] + z * cw[1:2, :] + z_next * cw[2:3, :]
    wr_hi = wr_hi_ref[...]
    wr_lo = wr_lo_ref[...]

    rs = tm // MIX_SPLIT
    for s in range(MIX_SPLIT):
        rows = slice(s * rs, (s + 1) * rs)
        o = of_ref[rows, :] + ob_ref[rows, :]
        gate_in = g_ref[rows, :].astype(F32)
        parts = []
        for h in range(RET_HEADS):
            sl = slice(h * HEAD_DIM, (h + 1) * HEAD_DIM)
            oh = o[:, sl]
            mu = jnp.mean(oh, axis=-1, keepdims=True)
            d = oh - mu
            var = jnp.mean(d * d, axis=-1, keepdims=True)
            gh = gate_in[:, sl]
            swish = gh / (1.0 + jnp.exp(-gh))
            parts.append((swish * (d * lax.rsqrt(var + EPS) * gn[:, sl])).astype(BF16))
        ret = jnp.concatenate(parts, axis=-1)
        conv = cb_ref[rows, :].astype(F32) * zc[rows, :]

        y = jnp.dot(ret, wo_ref[0:RET_WIDTH, :], preferred_element_type=F32)
        y = y + jnp.dot(conv.astype(BF16), wo_ref[RET_WIDTH:, :], preferred_element_type=F32)
        x1 = x_ref[rows, :] + y
        h2 = _rmsnorm(x1, g2_ref[...])
        for k in range(ROW_TILES):
            cols = slice(k * LANES, (k + 1) * LANES)
            dst = pl.ds(s * rs * ROW_PITCH + k, rs, stride=ROW_PITCH)
            acc_ref[dst, :] = x1[:, cols]
            tok_ref[dst, :] = h2[:, cols]
        last_tile = pl.ds(s * rs * ROW_PITCH + ROW_TILES, rs, stride=ROW_PITCH)
        acc_ref[last_tile, :] = jnp.zeros((rs, LANES), F32)
        h_hi, h_lo = _split_bf16(h2)
        d = functools.partial(jnp.dot, preferred_element_type=F32)
        logits = d(h_hi, wr_hi) + d(h_hi, wr_lo) + d(h_lo, wr_hi)
        lane = lax.broadcasted_iota(I32, logits.shape, 1)
        logits = jnp.where(lane < N_EXPERTS, logits, -jnp.inf)
        ex = jnp.exp(logits - jnp.max(logits, axis=-1, keepdims=True))
        aff = ex / jnp.sum(ex, axis=-1, keepdims=True)
        tok_ref[last_tile, :] = aff
        aff_ref[rows, :] = aff[:, 0:N_EXPERTS]


def _mix_out(o_f, o_b, proj, x2d, gn_g, conv_w, wo_bf16, g2, wr_hi, wr_lo, seq):
    t = x2d.shape[0]
    tm = min(256, seq)
    bps = seq // tm
    r8 = tm // HALO_ROWS
    n8 = t // HALO_ROWS
    col = lambda c: pl.BlockSpec((tm, COL_BLOCK), lambda i: (i, c))
    prev = lambda c: pl.BlockSpec((HALO_ROWS, COL_BLOCK), lambda i: (jnp.maximum(i * r8 - 1, 0), c))
    nxt = lambda c: pl.BlockSpec((HALO_ROWS, COL_BLOCK), lambda i: (jnp.minimum((i + 1) * r8, n8 - 1), c))
    whole = lambda shape: pl.BlockSpec(shape, lambda i: (0,) * len(shape))
    return pl.pallas_call(
        functools.partial(_mixout_kernel, blocks_per_seq=bps),
        out_shape=(jax.ShapeDtypeStruct((t * ROW_PITCH, LANES), F32),
                   jax.ShapeDtypeStruct((t * ROW_PITCH, LANES), F32),
                   jax.ShapeDtypeStruct((t, N_EXPERTS), F32)),
        grid=(t // tm,),
        in_specs=[
            pl.BlockSpec((tm, RET_WIDTH), lambda i: (i, 0)),
            pl.BlockSpec((tm, RET_WIDTH), lambda i: (i, 0)),
            col(3), col(4), col(5), col(6), prev(5), prev(6), nxt(5), nxt(6),
            pl.BlockSpec((tm, D_MODEL), lambda i: (i, 0)),
            whole((1, RET_WIDTH)), whole((3, COL_BLOCK)), whole((D_MODEL, D_MODEL)),
            whole((1, D_MODEL)), whole((D_MODEL, LANES)), whole((D_MODEL, LANES)),
        ],
        out_specs=(pl.BlockSpec((tm * ROW_PITCH, LANES), lambda i: (i, 0)),
                   pl.BlockSpec((tm * ROW_PITCH, LANES), lambda i: (i, 0)),
                   pl.BlockSpec((tm, N_EXPERTS), lambda i: (i, 0))),
        compiler_params=_params(("parallel",)),
        name="mix_out",
    )(o_f, o_b, proj, proj, proj, proj, proj, proj, proj, proj, x2d, gn_g, conv_w, wo_bf16, g2, wr_hi, wr_lo)


def _select_kernel(aff_ref, idx_ref, sel_ref, *, cap):
    e_n, rows, _ = aff_ref.shape
    bits = lax.bitcast_convert_type(aff_ref[...], I32)

    def count(mask):
        c = jnp.sum(jnp.where(mask, 1.0, 0.0), axis=1, keepdims=True)
        return jnp.sum(c, axis=2, keepdims=True)

    thr = jnp.zeros((e_n, 1, 1), I32)
    for b in range(30, -1, -1):
        cand = thr | (1 << b)
        thr = jnp.where(count(bits >= cand) >= cap, cand, thr)

    gt = bits > thr
    eq = bits == thr
    need = cap - count(gt)

    li = lax.broadcasted_iota(I32, (LANES, LANES), 0)
    lj = lax.broadcasted_iota(I32, (LANES, LANES), 1)
    upper = (li <= lj).astype(BF16)
    ri = lax.broadcasted_iota(I32, (rows, rows), 0)
    rj = lax.broadcasted_iota(I32, (rows, rows), 1)
    lower = (rj < ri).astype(BF16)
    eqf = jnp.where(eq, 1.0, 0.0)
    for e in range(e_n):
        incl = jnp.dot(eqf[e].astype(BF16), upper, preferred_element_type=F32)
        row_tot = jnp.broadcast_to(incl[:, LANES - 1:LANES], (rows, LANES)).astype(BF16)
        row_off = jnp.dot(lower, row_tot, preferred_element_type=F32)
        rank = row_off + incl - eqf[e]
        take = jnp.logical_or(gt[e], jnp.logical_and(eq[e], rank < need[e]))
        sel_ref[e] = jnp.where(take, 1.0, 0.0)

    chunk = min(512, cap)
    lower_incl = (li >= lj).astype(BF16)
    slot0 = lax.broadcasted_iota(I32, (1, chunk), 1).astype(F32)
    rowid = lax.broadcasted_iota(I32, (rows, chunk), 0).astype(F32)

    def compact(e, carry):
        s = sel_ref[e].astype(BF16)
        q_t = lax.dot_general(lower_incl, s, (((1,), (1,)), ((), ())),
                              preferred_element_type=F32)
        q = jnp.dot(s, upper, preferred_element_type=F32)
        row_tot = q[:, LANES - 1:LANES]
        row_cum = jnp.dot(lower, jnp.broadcast_to(row_tot, (rows, LANES)).astype(BF16),
                          preferred_element_type=F32)[:, 0:1] + row_tot
        q_t = q_t.astype(BF16)
        for ch in range(cap // chunk):
            c = slot0 + float(ch * chunk)
            before = row_cum <= c
            r_c = jnp.sum(jnp.where(before, 1.0, 0.0), axis=0, keepdims=True)
            k = c - jnp.sum(jnp.where(before, row_tot, 0.0), axis=0, keepdims=True)
            onehot = jnp.where(rowid == r_c, 1.0, 0.0).astype(BF16)
            q_row = jnp.dot(q_t, onehot, preferred_element_type=F32)
            lane_c = jnp.sum(jnp.where(q_row <= k, 1.0, 0.0), axis=0, keepdims=True)
            token = r_c * LANES + lane_c
            idx_ref[pl.ds(e, 1), ch * chunk:(ch + 1) * chunk] = (token * ROW_PITCH).astype(I32)
        return carry

    lax.fori_loop(0, e_n, compact, 0)


def _select(aff3, cap):
    e_n, rows, lanes = aff3.shape
    return pl.pallas_call(
        functools.partial(_select_kernel, cap=cap),
        out_shape=jax.ShapeDtypeStruct((e_n, cap), I32),
        grid=(1,),
        in_specs=[pl.BlockSpec((e_n, rows, lanes), lambda i: (0, 0, 0))],
        out_specs=pl.BlockSpec((e_n, cap), lambda i: (0, 0)),
        scratch_shapes=[pltpu.VMEM((e_n, rows, lanes), F32)],
        compiler_params=_params(("arbitrary",)),
        name="select",
    )(aff3)


def _experts_kernel(idx_ref, idx_next_ref, tok_hbm, acc_in_hbm, wg_in, wu_in, wd_in, acc_hbm, *rest,
                    cast_weights):
    del acc_in_hbm
    if cast_weights:
        wg_ref, wu_ref, wd_ref, xbuf, xn_ref, yacc_ref, gate_ref, sem_x, sem_a, sem_s = rest
        wg_ref[...] = wg_in[...].astype(BF16)
        wu_ref[...] = wu_in[...].astype(BF16)
        wd_ref[...] = wd_in[...].astype(BF16)
    else:
        wg_ref, wu_ref, wd_ref = wg_in, wu_in, wd_in
        xbuf, xn_ref, yacc_ref, gate_ref, sem_x, sem_a, sem_s = rest
    e = pl.program_id(0)
    c = pl.program_id(1)
    f = pl.program_id(2)
    ne = pl.num_programs(0)
    nt = pl.num_programs(1)
    nf = pl.num_programs(2)
    tc = xn_ref.shape[0]
    g = e * nt + c
    slot = g % 2
    other = 1 - slot

    def start_gather(src_hbm, tiles, ids_ref, base, buf, sem, priority):
        def issue(r, carry):
            pltpu.make_async_copy(src_hbm.at[pl.ds(ids_ref[base + r], tiles), :],
                                  xbuf.at[buf, pl.ds(r * ROW_PITCH, tiles), :], sem.at[buf]).start(priority)
            return carry
        lax.fori_loop(0, tc, issue, 0, unroll=min(ISSUE_UNROLL, tc))

    def wait_gather(src_hbm, tiles, buf, sem):
        n = tc * tiles
        pltpu.make_async_copy(src_hbm.at[pl.ds(0, n), :], xbuf.at[buf, pl.ds(0, n), :], sem.at[buf]).wait()

    def wait_scatter(buf):
        n = tc * ROW_TILES
        pltpu.make_async_copy(xbuf.at[buf, pl.ds(0, n), :], acc_hbm.at[pl.ds(0, n), :], sem_s.at[buf]).wait()

    def tile_rows(r0, n, k):
        return pl.ds(r0 * ROW_PITCH + k, n, stride=ROW_PITCH)

    rb = min(EXPERT_ROWS, tc)

    @pl.when(jnp.logical_and(g == 0, f == 0))
    def _():
        start_gather(tok_hbm, ROW_PITCH, idx_ref, 0, 0, sem_x, TOK_PRIORITY)

    @pl.when(f == 0)
    def _():
        wait_gather(tok_hbm, ROW_PITCH, slot, sem_x)

        for r0 in range(0, tc, rb):
            for k in range(ROW_TILES):
                xn_ref[r0:r0 + rb, k * LANES:(k + 1) * LANES] = xbuf[slot, tile_rows(r0, rb, k), :].astype(BF16)
            aff = xbuf[slot, tile_rows(r0, rb, ROW_TILES), :]
            lane = lax.broadcasted_iota(I32, aff.shape, 1)
            gate_ref[r0:r0 + rb, :] = jnp.sum(jnp.where(lane == e, aff, 0.0), axis=-1, keepdims=True)
            yacc_ref[r0:r0 + rb, :] = jnp.zeros((rb, D_MODEL), F32)

        @pl.when(g > 0)
        def _():
            wait_scatter(other)

        start_gather(acc_hbm, ROW_TILES, idx_ref, c * tc, slot, sem_a, ACC_PRIORITY)

        @pl.when(c + 1 < nt)
        def _():
            start_gather(tok_hbm, ROW_PITCH, idx_ref, (c + 1) * tc, other, sem_x, TOK_PRIORITY)

        @pl.when(jnp.logical_and(c + 1 == nt, e + 1 < ne))
        def _():
            start_gather(tok_hbm, ROW_PITCH, idx_next_ref, 0, other, sem_x, TOK_PRIORITY)

    def ffn(m, carry):
        rows = pl.ds(pl.multiple_of(m * rb, rb), rb)
        xn = xn_ref[rows, :]
        hg = jnp.dot(xn, wg_ref[0], preferred_element_type=F32)
        hu = jnp.dot(xn, wu_ref[0], preferred_element_type=F32)
        hid = (hg / (1.0 + jnp.exp(-hg)) * hu).astype(BF16)
        for n0 in range(0, D_MODEL, DOWN_COLS):
            cols = slice(n0, n0 + DOWN_COLS)
            yacc_ref[rows, cols] += jnp.dot(hid, wd_ref[0, :, cols], preferred_element_type=F32)
        return carry
    lax.fori_loop(0, tc // rb, ffn, 0)

    @pl.when(f == nf - 1)
    def _():
        wait_gather(acc_hbm, ROW_TILES, slot, sem_a)

        def issue(pair, carry):
            for p in range(2):
                r = 2 * pair + p
                pltpu.make_async_copy(xbuf.at[slot, pl.ds(r * ROW_PITCH, ROW_TILES), :],
                                      acc_hbm.at[pl.ds(idx_ref[c * tc + r], ROW_TILES), :],
                                      sem_s.at[slot]).start(p)
            return carry

        cb = min(COMBINE_ROWS, tc)
        for r0 in range(0, tc, cb):
            gate = gate_ref[r0:r0 + cb, :]
            for k in range(ROW_TILES):
                dst = tile_rows(r0, cb, k)
                xbuf[slot, dst, :] = xbuf[slot, dst, :] + yacc_ref[r0:r0 + cb, k * LANES:(k + 1) * LANES] * gate
            lax.fori_loop(r0 // 2, (r0 + cb) // 2, issue, 0, unroll=min(ISSUE_UNROLL, cb) // 2)

        @pl.when(jnp.logical_and(e == ne - 1, c == nt - 1))
        def _():
            wait_scatter(slot)


def _experts(idx_flat, tok, acc, wg, wu, wd, cap):
    tc = min(EXPERT_TILE, cap)
    cast_weights = wg.dtype != BF16
    tf = 256 if cast_weights else 512
    assert not cast_weights or cap == tc, "each bf16 weight block must be written by exactly one grid step"
    w_specs = [
        pl.BlockSpec((1, D_MODEL, tf), lambda e, c, f: (e, 0, f)),
        pl.BlockSpec((1, D_MODEL, tf), lambda e, c, f: (e, 0, f)),
        pl.BlockSpec((1, tf, D_MODEL), lambda e, c, f: (e, f, 0)),
    ]
    acc_shape = jax.ShapeDtypeStruct(acc.shape, F32)
    acc_spec = pl.BlockSpec(memory_space=pl.ANY)
    w_shape = jax.ShapeDtypeStruct(wg.shape, BF16)
    out = pl.pallas_call(
        functools.partial(_experts_kernel, cast_weights=cast_weights),
        out_shape=(acc_shape, w_shape, w_shape, w_shape) if cast_weights else acc_shape,
        grid=(N_EXPERTS, cap // tc, EXPERT_FF // tf),
        in_specs=[
            pl.BlockSpec((cap,), lambda e, c, f: (e,), memory_space=pltpu.SMEM),
            pl.BlockSpec((cap,), lambda e, c, f: (jnp.minimum(e + 1, N_EXPERTS - 1),), memory_space=pltpu.SMEM),
            pl.BlockSpec(memory_space=pl.ANY),
            pl.BlockSpec(memory_space=pl.ANY),
            *w_specs,
        ],
        out_specs=(acc_spec, *w_specs) if cast_weights else acc_spec,
        scratch_shapes=[
            pltpu.VMEM((2, tc * ROW_PITCH, LANES), F32),
            pltpu.VMEM((tc, D_MODEL), BF16),
            pltpu.VMEM((tc, D_MODEL), F32),
            pltpu.VMEM((tc, 1), F32),
            pltpu.SemaphoreType.DMA((2,)),
            pltpu.SemaphoreType.DMA((2,)),
            pltpu.SemaphoreType.DMA((2,)),
        ],
        input_output_aliases={3: 0},
        compiler_params=_params(("arbitrary", "arbitrary", "arbitrary")),
        name="experts",
    )(idx_flat, idx_flat, tok, acc, wg, wu, wd)
    return out if cast_weights else (out, wg, wu, wd)


def _final_kernel(x_ref, g_ref, o_ref):
    tm = o_ref.shape[0]
    x = jnp.concatenate([x_ref[pl.ds(k, tm, stride=ROW_PITCH), :] for k in range(ROW_TILES)], axis=-1)
    o_ref[...] = _rmsnorm(x, g_ref[...])


def _final_norm(rows, g):
    t = rows.shape[0] // ROW_PITCH
    tm = min(512, t)
    return pl.pallas_call(
        _final_kernel,
        out_shape=jax.ShapeDtypeStruct((t, D_MODEL), F32),
        grid=(t // tm,),
        in_specs=[pl.BlockSpec((tm * ROW_PITCH, LANES), lambda i: (i, 0)),
                  pl.BlockSpec((1, D_MODEL), lambda i: (0, 0))],
        out_specs=pl.BlockSpec((tm, D_MODEL), lambda i: (i, 0)),
        compiler_params=_params(("parallel",)),
        name="final_norm",
    )(rows, g)


def _rotary_tables(seq):
    half = HEAD_DIM // 2
    inv = ROPE_BASE ** (-jnp.arange(half, dtype=F32) / half)
    ang = jnp.arange(seq, dtype=F32)[:, None] * inv[None, :]
    cos, sin = jnp.cos(ang), jnp.sin(ang)
    return jnp.concatenate([cos, cos], axis=-1), jnp.concatenate([-sin, sin], axis=-1)


def _trunk(x, w, expert_w):
    batch, seq, _ = x.shape
    t = batch * seq
    cap = EC_CAPACITY_FACTOR * t // N_EXPERTS
    x2d = x.reshape(t, D_MODEL)
    cos_t, sin_t = _rotary_tables(seq)

    proj = _in_proj(x2d, w["norm_mix_g"], w["w_in"], cos_t, sin_t, seq)
    o_f, o_b = _retention(proj, w["decays"], batch, seq)
    acc, tok, aff = _mix_out(o_f, o_b, proj, x2d, w["ret_gn_g"], w["conv_w"], w["w_o"],
                             w["norm_ffn_g"], *w["w_router_split"], seq)
    idx = _select(aff.T.reshape(N_EXPERTS, t // LANES, LANES), cap)
    acc, *expert_w = _experts(idx.reshape(N_EXPERTS * cap), tok, acc, *expert_w, cap)
    return _final_norm(acc, w["final_norm_g"]).reshape(batch, seq, D_MODEL), expert_w


def kernel(x_prompt, x_sample, norm_mix_g, w_in, conv_w, ret_decay_fwd, ret_decay_bwd, ret_gn_g, w_o,
           norm_ffn_g, w_router, w_gate, w_up, w_down, final_norm_g):
    assert w_in.shape[0] == 1, "one encoder layer"
    w = {
        "norm_mix_g": norm_mix_g[0][None, :],
        "w_in": w_in[0].astype(BF16),
        "conv_w": conv_w[0],
        "decays": jnp.stack([ret_decay_fwd[0], ret_decay_bwd[0]]),
        "ret_gn_g": ret_gn_g[0][None, :],
        "w_o": w_o[0].astype(BF16),
        "norm_ffn_g": norm_ffn_g[0][None, :],
        "w_router_split": _split_bf16(jnp.pad(w_router[0], ((0, 0), (0, LANES - N_EXPERTS)))),
        "final_norm_g": final_norm_g[None, :],
    }
    y_prompt, expert_w = _trunk(x_prompt, w, (w_gate[0], w_up[0], w_down[0]))
    y_sample, _ = _trunk(x_sample, w, expert_w)
    return (y_prompt, y_sample)
```

```python
import functools

import jax
import jax.numpy as jnp
from jax import lax
from jax.experimental import pallas as pl
from jax.experimental.pallas import tpu as pltpu

F32 = jnp.float32
BF16 = jnp.bfloat16
I32 = jnp.int32

D_MODEL = 2048
RET_WIDTH = 1024
RET_HEADS = 8
HEAD_DIM = 128
IN_COLS = 7168
COL_BLOCK = 1024
N_EXPERTS = 16
EC_CAPACITY_FACTOR = 2
EXPERT_FF = 2048
CHUNK = 256
ROPE_BASE = 10000.0
EPS = 1e-6
LANES = 128
HALO_ROWS = 16
ROW_TILES = D_MODEL // LANES
ROW_PITCH = ROW_TILES + 1
RET_CHUNKS = 4
MIX_SPLIT = 2
EXPERT_TILE = 1024
EXPERT_ROWS = 1024
DOWN_COLS = 512
COMBINE_ROWS = 256
ISSUE_UNROLL = 64
TOK_PRIORITY = 0
ACC_PRIORITY = 1
VMEM_LIMIT = 56 * 1024 * 1024


def _params(sem, vmem=VMEM_LIMIT):
    return pltpu.CompilerParams(dimension_semantics=sem, vmem_limit_bytes=vmem)


def _rmsnorm(x, g):
    return x * lax.rsqrt(jnp.mean(x * x, axis=-1, keepdims=True) + EPS) * g


def _split_bf16(x):
    hi = x.astype(BF16)
    lo = (x - hi.astype(F32)).astype(BF16)
    return hi, lo


def _inproj_kernel(x_ref, g_ref, w_ref, cos_ref, sin_ref, o_ref, h_ref, acc_ref):
    j = pl.program_id(1)

    @pl.when(j == 0)
    def _():
        h_ref[...] = _rmsnorm(x_ref[...], g_ref[...]).astype(BF16)

    @pl.when(j < 2)
    def _():
        acc_ref[...] = jnp.dot(h_ref[...], w_ref[...], preferred_element_type=F32)
        scale = jnp.where(j == 1, HEAD_DIM ** -0.5, 1.0).astype(F32)
        cos = cos_ref[...]
        sin = sin_ref[...]
        for h in range(COL_BLOCK // HEAD_DIM):
            sl = slice(h * HEAD_DIM, (h + 1) * HEAD_DIM)
            t = acc_ref[:, sl]
            o_ref[:, sl] = ((t * cos + pltpu.roll(t, HEAD_DIM // 2, axis=1) * sin) * scale).astype(BF16)

    @pl.when(j >= 2)
    def _():
        o_ref[...] = jnp.dot(h_ref[...], w_ref[...], preferred_element_type=F32).astype(BF16)


def _in_proj(x2d, g, w_bf16, cos_t, sin_t, seq):
    t = x2d.shape[0]
    tm = min(1024, seq)
    blocks_per_seq = seq // tm
    return pl.pallas_call(
        _inproj_kernel,
        out_shape=jax.ShapeDtypeStruct((t, IN_COLS), BF16),
        grid=(t // tm, IN_COLS // COL_BLOCK),
        in_specs=[
            pl.BlockSpec((tm, D_MODEL), lambda i, j: (i, 0)),
            pl.BlockSpec((1, D_MODEL), lambda i, j: (0, 0)),
            pl.BlockSpec((D_MODEL, COL_BLOCK), lambda i, j: (0, j)),
            pl.BlockSpec((tm, HEAD_DIM), lambda i, j: (i % blocks_per_seq, 0)),
            pl.BlockSpec((tm, HEAD_DIM), lambda i, j: (i % blocks_per_seq, 0)),
        ],
        out_specs=pl.BlockSpec((tm, COL_BLOCK), lambda i, j: (i, j)),
        scratch_shapes=[pltpu.VMEM((tm, D_MODEL), BF16), pltpu.VMEM((tm, COL_BLOCK), F32)],
        compiler_params=_params(("parallel", "arbitrary")),
        name="in_proj",
    )(x2d, g, w_bf16, cos_t, sin_t)


def _retention_kernel(dec_ref, qf_ref, kf_ref, vf_ref, qb_ref, kb_ref, vb_ref,
                      of_ref, ob_ref,
                      sf_ref, sb_ref, dtab_ref, xif_ref, xib_ref, zf_ref, zb_ref):
    first = jnp.logical_and(pl.program_id(0) == 0, pl.program_id(1) == 0)

    @pl.when(first)
    def _():
        diff = (lax.broadcasted_iota(I32, (CHUNK, CHUNK), 0)
                - lax.broadcasted_iota(I32, (CHUNK, CHUNK), 1)).astype(F32)
        row = lax.broadcasted_iota(I32, (CHUNK, HEAD_DIM), 0).astype(F32)
        for h in range(RET_HEADS):
            lgf = -jnp.exp(dec_ref[0:1, h:h + 1])
            lgb = -jnp.exp(dec_ref[1:2, h:h + 1])
            dtab_ref[h] = jnp.where(diff >= 0, jnp.exp(lgf * jnp.maximum(diff, 0.0)),
                                    jnp.exp(lgb * jnp.maximum(-diff, 0.0)))
            xif_ref[h] = jnp.exp(lgf * (row + 1.0))
            zf_ref[h] = jnp.exp(lgf * (CHUNK - 1.0 - row))
            xib_ref[h] = jnp.exp(lgb * (CHUNK - row))
            zb_ref[h] = jnp.exp(lgb * row)

    @pl.when(pl.program_id(1) == 0)
    def _():
        sf_ref[...] = jnp.zeros_like(sf_ref)
        sb_ref[...] = jnp.zeros_like(sb_ref)

    chunks = qf_ref.shape[0] // CHUNK
    nt = (((1,), (1,)), ((), ()))
    tn = (((0,), (0,)), ((), ()))
    for h in range(RET_HEADS):
        sl = slice(h * HEAD_DIM, (h + 1) * HEAD_DIM)
        gcf = jnp.exp(-jnp.exp(dec_ref[0:1, h:h + 1]) * CHUNK)
        gcb = jnp.exp(-jnp.exp(dec_ref[1:2, h:h + 1]) * CHUNK)

        for u in range(chunks):
            rows = slice(u * CHUNK, (u + 1) * CHUNK)
            q = qf_ref[rows, sl]
            k = kf_ref[rows, sl]
            v = vf_ref[rows, sl]
            s = lax.dot_general(q, k, nt, preferred_element_type=F32) * dtab_ref[h]
            sf = sf_ref[h]
            o = jnp.dot(s.astype(BF16), v, preferred_element_type=F32)
            o = o + jnp.dot(q, sf.astype(BF16), preferred_element_type=F32) * xif_ref[h]
            of_ref[rows, sl] = o
            kz = (k.astype(F32) * zf_ref[h]).astype(BF16)
            sf_ref[h] = sf * gcf + lax.dot_general(kz, v, tn, preferred_element_type=F32)

        for u in reversed(range(chunks)):
            rows = slice(u * CHUNK, (u + 1) * CHUNK)
            q = qb_ref[rows, sl]
            v = vb_ref[rows, sl]
            sb = sb_ref[h]
            ob_ref[rows, sl] = jnp.dot(q, sb.astype(BF16), preferred_element_type=F32) * xib_ref[h]
            kz = (kb_ref[rows, sl].astype(F32) * zb_ref[h]).astype(BF16)
            sb_ref[h] = sb * gcb + lax.dot_general(kz, v, tn, preferred_element_type=F32)


def _retention(proj, decays, batch, seq):
    t = proj.shape[0]
    rows = min(RET_CHUNKS * CHUNK, seq)
    n = seq // rows
    fwd = lambda col: pl.BlockSpec((rows, COL_BLOCK), lambda b, c: (b * n + c, col))
    bwd = lambda col: pl.BlockSpec((rows, COL_BLOCK), lambda b, c: (b * n + n - 1 - c, col))
    state = pltpu.VMEM((RET_HEADS, HEAD_DIM, HEAD_DIM), F32)
    decay = pltpu.VMEM((RET_HEADS, CHUNK, CHUNK), F32)
    per_row = pltpu.VMEM((RET_HEADS, CHUNK, HEAD_DIM), F32)
    return pl.pallas_call(
        _retention_kernel,
        out_shape=(jax.ShapeDtypeStruct((t, RET_WIDTH), F32), jax.ShapeDtypeStruct((t, RET_WIDTH), F32)),
        grid=(batch, n),
        in_specs=[pl.BlockSpec((2, RET_HEADS), lambda b, c: (0, 0)),
                  fwd(0), fwd(1), fwd(2), bwd(0), bwd(1), bwd(2)],
        out_specs=(pl.BlockSpec((rows, RET_WIDTH), lambda b, c: (b * n + c, 0)),
                   pl.BlockSpec((rows, RET_WIDTH), lambda b, c: (b * n + n - 1 - c, 0))),
        scratch_shapes=[state, state, decay, per_row, per_row, per_row, per_row],
        compiler_params=_params(("arbitrary", "arbitrary")),
        name="retention",
    )(decays, proj, proj, proj, proj, proj, proj)


def _mixout_kernel(of_ref, ob_ref, g_ref, cb_ref, cc_ref, ch_ref, ccp_ref, chp_ref, ccn_ref, chn_ref,
                   x_ref, gn_ref, cw_ref, wo_ref, g2_ref, wr_hi_ref, wr_lo_ref,
                   acc_ref, tok_ref, aff_ref, *, blocks_per_seq):
    i = pl.program_id(0)
    tm = x_ref.shape[0]

    gn = gn_ref[...]
    z = cc_ref[...].astype(F32) * ch_ref[...].astype(F32)
    pos = i % blocks_per_seq
    last = HALO_ROWS - 1
    halo_prev = (ccp_ref[...].astype(F32) * chp_ref[...].astype(F32))[last:, :]
    halo_next = (ccn_ref[...].astype(F32) * chn_ref[...].astype(F32))[0:1, :]
    halo_prev = jnp.where(pos == 0, 0.0, halo_prev)
    halo_next = jnp.where(pos == blocks_per_seq - 1, 0.0, halo_next)
    row = lax.broadcasted_iota(I32, z.shape, 0)
    z_prev = jnp.where(row == 0, halo_prev, pltpu.roll(z, 1, axis=0))
    z_next = jnp.where(row == tm - 1, halo_next, pltpu.roll(z, tm - 1, axis=0))
    cw = cw_ref[...]
    zc = z_prev * cw[0:1, :] + z * cw[1:2, :] + z_next * cw[2:3, :]
    wr_hi = wr_hi_ref[...]
    wr_lo = wr_lo_ref[...]

    rs = tm // MIX_SPLIT
    for s in range(MIX_SPLIT):
        rows = slice(s * rs, (s + 1) * rs)
        parts = []
        for h in range(RET_HEADS):
            sl = slice(h * HEAD_DIM, (h + 1) * HEAD_DIM)
            oh = of_ref[rows, sl] + ob_ref[rows, sl]
            mu = jnp.mean(oh, axis=-1, keepdims=True)
            d = oh - mu
            var = jnp.mean(d * d, axis=-1, keepdims=True)
            gh = g_ref[rows, sl].astype(F32)
            swish = gh / (1.0 + jnp.exp(-gh))
            parts.append((swish * (d * lax.rsqrt(var + EPS) * gn[:, sl])).astype(BF16))
        ret = jnp.concatenate(parts, axis=-1)
        conv = cb_ref[rows, :].astype(F32) * zc[rows, :]

        y = jnp.dot(ret, wo_ref[0:RET_WIDTH, :], preferred_element_type=F32)
        y = y + jnp.dot(conv.astype(BF16), wo_ref[RET_WIDTH:, :], preferred_element_type=F32)
        x1 = x_ref[rows, :] + y
        h2 = _rmsnorm(x1, g2_ref[...])
        for k in range(ROW_TILES):
            cols = slice(k * LANES, (k + 1) * LANES)
            dst = pl.ds(s * rs * ROW_PITCH + k, rs, stride=ROW_PITCH)
            acc_ref[dst, :] = x1[:, cols]
            tok_ref[dst, :] = h2[:, cols]
        last_tile = pl.ds(s * rs * ROW_PITCH + ROW_TILES, rs, stride=ROW_PITCH)
        acc_ref[last_tile, :] = jnp.zeros((rs, LANES), F32)
        h_hi, h_lo = _split_bf16(h2)
        d = functools.partial(jnp.dot, preferred_element_type=F32)
        logits = d(h_hi, wr_hi) + d(h_hi, wr_lo) + d(h_lo, wr_hi)
        lane = lax.broadcasted_iota(I32, logits.shape, 1)
        logits = jnp.where(lane < N_EXPERTS, logits, -jnp.inf)
        ex = jnp.exp(logits - jnp.max(logits, axis=-1, keepdims=True))
        aff = ex / jnp.sum(ex, axis=-1, keepdims=True)
        tok_ref[last_tile, :] = aff
        aff_ref[rows, :] = aff[:, 0:N_EXPERTS]


def _mix_out(o_f, o_b, proj, x2d, gn_g, conv_w, wo_bf16, g2, wr_hi, wr_lo, seq):
    t = x2d.shape[0]
    tm = min(256, seq)
    bps = seq // tm
    r8 = tm // HALO_ROWS
    n8 = t // HALO_ROWS
    col = lambda c: pl.BlockSpec((tm, COL_BLOCK), lambda i: (i, c))
    prev = lambda c: pl.BlockSpec((HALO_ROWS, COL_BLOCK), lambda i: (jnp.maximum(i * r8 - 1, 0), c))
    nxt = lambda c: pl.BlockSpec((HALO_ROWS, COL_BLOCK), lambda i: (jnp.minimum((i + 1) * r8, n8 - 1), c))
    whole = lambda shape: pl.BlockSpec(shape, lambda i: (0,) * len(shape))
    return pl.pallas_call(
        functools.partial(_mixout_kernel, blocks_per_seq=bps),
        out_shape=(jax.ShapeDtypeStruct((t * ROW_PITCH, LANES), F32),
                   jax.ShapeDtypeStruct((t * ROW_PITCH, LANES), F32),
                   jax.ShapeDtypeStruct((t, N_EXPERTS), F32)),
        grid=(t // tm,),
        in_specs=[
            pl.BlockSpec((tm, RET_WIDTH), lambda i: (i, 0)),
            pl.BlockSpec((tm, RET_WIDTH), lambda i: (i, 0)),
            col(3), col(4), col(5), col(6), prev(5), prev(6), nxt(5), nxt(6),
            pl.BlockSpec((tm, D_MODEL), lambda i: (i, 0)),
            whole((1, RET_WIDTH)), whole((3, COL_BLOCK)), whole((D_MODEL, D_MODEL)),
            whole((1, D_MODEL)), whole((D_MODEL, LANES)), whole((D_MODEL, LANES)),
        ],
        out_specs=(pl.BlockSpec((tm * ROW_PITCH, LANES), lambda i: (i, 0)),
                   pl.BlockSpec((tm * ROW_PITCH, LANES), lambda i: (i, 0)),
                   pl.BlockSpec((tm, N_EXPERTS), lambda i: (i, 0))),
        compiler_params=_params(("parallel",)),
        name="mix_out",
    )(o_f, o_b, proj, proj, proj, proj, proj, proj, proj, proj, x2d, gn_g, conv_w, wo_bf16, g2, wr_hi, wr_lo)


def _select_kernel(aff_ref, idx_ref, sel_ref, *, cap):
    e_n, rows, _ = aff_ref.shape
    bits = lax.bitcast_convert_type(aff_ref[...], I32)

    def count(mask):
        c = jnp.sum(jnp.where(mask, 1.0, 0.0), axis=1, keepdims=True)
        return jnp.sum(c, axis=2, keepdims=True)

    thr = jnp.zeros((e_n, 1, 1), I32)
    for b in range(30, -1, -1):
        cand = thr | (1 << b)
        thr = jnp.where(count(bits >= cand) >= cap, cand, thr)

    gt = bits > thr
    eq = bits == thr
    need = cap - count(gt)

    li = lax.broadcasted_iota(I32, (LANES, LANES), 0)
    lj = lax.broadcasted_iota(I32, (LANES, LANES), 1)
    upper = (li <= lj).astype(BF16)
    ri = lax.broadcasted_iota(I32, (rows, rows), 0)
    rj = lax.broadcasted_iota(I32, (rows, rows), 1)
    lower = (rj < ri).astype(BF16)
    eqf = jnp.where(eq, 1.0, 0.0)
    for e in range(e_n):
        incl = jnp.dot(eqf[e].astype(BF16), upper, preferred_element_type=F32)
        row_tot = jnp.broadcast_to(incl[:, LANES - 1:LANES], (rows, LANES)).astype(BF16)
        row_off = jnp.dot(lower, row_tot, preferred_element_type=F32)
        rank = row_off + incl - eqf[e]
        take = jnp.logical_or(gt[e], jnp.logical_and(eq[e], rank < need[e]))
        sel_ref[e] = jnp.where(take, 1.0, 0.0)

    chunk = min(512, cap)
    lower_incl = (li >= lj).astype(BF16)
    slot0 = lax.broadcasted_iota(I32, (1, chunk), 1).astype(F32)
    rowid = lax.broadcasted_iota(I32, (rows, chunk), 0).astype(F32)

    def compact(e, carry):
        s = sel_ref[e].astype(BF16)
        q_t = lax.dot_general(lower_incl, s, (((1,), (1,)), ((), ())),
                              preferred_element_type=F32)
        q = jnp.dot(s, upper, preferred_element_type=F32)
        row_tot = q[:, LANES - 1:LANES]
        row_cum = jnp.dot(lower, jnp.broadcast_to(row_tot, (rows, LANES)).astype(BF16),
                          preferred_element_type=F32)[:, 0:1] + row_tot
        q_t = q_t.astype(BF16)
        for ch in range(cap // chunk):
            c = slot0 + float(ch * chunk)
            before = row_cum <= c
            r_c = jnp.sum(jnp.where(before, 1.0, 0.0), axis=0, keepdims=True)
            k = c - jnp.sum(jnp.where(before, row_tot, 0.0), axis=0, keepdims=True)
            onehot = jnp.where(rowid == r_c, 1.0, 0.0).astype(BF16)
            q_row = jnp.dot(q_t, onehot, preferred_element_type=F32)
            lane_c = jnp.sum(jnp.where(q_row <= k, 1.0, 0.0), axis=0, keepdims=True)
            token = r_c * LANES + lane_c
            idx_ref[pl.ds(e, 1), ch * chunk:(ch + 1) * chunk] = (token * ROW_PITCH).astype(I32)
        return carry

    lax.fori_loop(0, e_n, compact, 0)


def _select(aff3, cap):
    e_n, rows, lanes = aff3.shape
    return pl.pallas_call(
        functools.partial(_select_kernel, cap=cap),
        out_shape=jax.ShapeDtypeStruct((e_n, cap), I32),
        grid=(1,),
        in_specs=[pl.BlockSpec((e_n, rows, lanes), lambda i: (0, 0, 0))],
        out_specs=pl.BlockSpec((e_n, cap), lambda i: (0, 0)),
        scratch_shapes=[pltpu.VMEM((e_n, rows, lanes), F32)],
        compiler_params=_params(("arbitrary",)),
        name="select",
    )(aff3)


def _experts_kernel(idx_ref, idx_next_ref, tok_hbm, acc_in_hbm, wg_in, wu_in, wd_in, acc_hbm, *rest,
                    cast_weights):
    del acc_in_hbm
    if cast_weights:
        wg_ref, wu_ref, wd_ref, xbuf, xn_ref, yacc_ref, gate_ref, sem_x, sem_a, sem_s = rest
        wg_ref[...] = wg_in[...].astype(BF16)
        wu_ref[...] = wu_in[...].astype(BF16)
        wd_ref[...] = wd_in[...].astype(BF16)
    else:
        wg_ref, wu_ref, wd_ref = wg_in, wu_in, wd_in
        xbuf, xn_ref, yacc_ref, gate_ref, sem_x, sem_a, sem_s = rest
    e = pl.program_id(0)
    c = pl.program_id(1)
    f = pl.program_id(2)
    ne = pl.num_programs(0)
    nt = pl.num_programs(1)
    nf = pl.num_programs(2)
    tc = xn_ref.shape[0]
    g = e * nt + c
    slot = g % 2
    other = 1 - slot

    def start_gather(src_hbm, tiles, ids_ref, base, buf, sem, priority):
        def issue(r, carry):
            pltpu.make_async_copy(src_hbm.at[pl.ds(ids_ref[base + r], tiles), :],
                                  xbuf.at[buf, pl.ds(r * ROW_PITCH, tiles), :], sem.at[buf]).start(priority)
            return carry
        lax.fori_loop(0, tc, issue, 0, unroll=min(ISSUE_UNROLL, tc))

    def wait_gather(src_hbm, tiles, buf, sem):
        n = tc * tiles
        pltpu.make_async_copy(src_hbm.at[pl.ds(0, n), :], xbuf.at[buf, pl.ds(0, n), :], sem.at[buf]).wait()

    def wait_scatter(buf):
        n = tc * ROW_TILES
        pltpu.make_async_copy(xbuf.at[buf, pl.ds(0, n), :], acc_hbm.at[pl.ds(0, n), :], sem_s.at[buf]).wait()

    def tile_rows(r0, n, k):
        return pl.ds(r0 * ROW_PITCH + k, n, stride=ROW_PITCH)

    rb = min(EXPERT_ROWS, tc)

    @pl.when(jnp.logical_and(g == 0, f == 0))
    def _():
        start_gather(tok_hbm, ROW_PITCH, idx_ref, 0, 0, sem_x, TOK_PRIORITY)

    @pl.when(f == 0)
    def _():
        wait_gather(tok_hbm, ROW_PITCH, slot, sem_x)

        for r0 in range(0, tc, rb):
            for k in range(ROW_TILES):
                xn_ref[r0:r0 + rb, k * LANES:(k + 1) * LANES] = xbuf[slot, tile_rows(r0, rb, k), :].astype(BF16)
            aff = xbuf[slot, tile_rows(r0, rb, ROW_TILES), :]
            lane = lax.broadcasted_iota(I32, aff.shape, 1)
            gate_ref[r0:r0 + rb, :] = jnp.sum(jnp.where(lane == e, aff, 0.0), axis=-1, keepdims=True)
            yacc_ref[r0:r0 + rb, :] = jnp.zeros((rb, D_MODEL), F32)

        @pl.when(g > 0)
        def _():
            wait_scatter(other)

        start_gather(acc_hbm, ROW_TILES, idx_ref, c * tc, slot, sem_a, ACC_PRIORITY)

        @pl.when(c + 1 < nt)
        def _():
            start_gather(tok_hbm, ROW_PITCH, idx_ref, (c + 1) * tc, other, sem_x, TOK_PRIORITY)

        @pl.when(jnp.logical_and(c + 1 == nt, e + 1 < ne))
        def _():
            start_gather(tok_hbm, ROW_PITCH, idx_next_ref, 0, other, sem_x, TOK_PRIORITY)

    def ffn(m, carry):
        rows = pl.ds(pl.multiple_of(m * rb, rb), rb)
        xn = xn_ref[rows, :]
        hg = jnp.dot(xn, wg_ref[0], preferred_element_type=F32)
        hu = jnp.dot(xn, wu_ref[0], preferred_element_type=F32)
        hid = (hg / (1.0 + jnp.exp(-hg)) * hu).astype(BF16)
        for n0 in range(0, D_MODEL, DOWN_COLS):
            cols = slice(n0, n0 + DOWN_COLS)
            yacc_ref[rows, cols] += jnp.dot(hid, wd_ref[0, :, cols], preferred_element_type=F32)
        return carry
    lax.fori_loop(0, tc // rb, ffn, 0)

    @pl.when(f == nf - 1)
    def _():
        wait_gather(acc_hbm, ROW_TILES, slot, sem_a)

        def issue(pair, carry):
            for p in range(2):
                r = 2 * pair + p
                pltpu.make_async_copy(xbuf.at[slot, pl.ds(r * ROW_PITCH, ROW_TILES), :],
                                      acc_hbm.at[pl.ds(idx_ref[c * tc + r], ROW_TILES), :],
                                      sem_s.at[slot]).start(p)
            return carry

        cb = min(COMBINE_ROWS, tc)
        for r0 in range(0, tc, cb):
            gate = gate_ref[r0:r0 + cb, :]
            for k in range(ROW_TILES):
                dst = tile_rows(r0, cb, k)
                xbuf[slot, dst, :] = xbuf[slot, dst, :] + yacc_ref[r0:r0 + cb, k * LANES:(k + 1) * LANES] * gate
            lax.fori_loop(r0 // 2, (r0 + cb) // 2, issue, 0, unroll=min(ISSUE_UNROLL, cb) // 2)

        @pl.when(jnp.logical_and(e == ne - 1, c == nt - 1))
        def _():
            wait_scatter(slot)


def _experts(idx_flat, tok, acc, wg, wu, wd, cap):
    tc = min(EXPERT_TILE, cap)
    cast_weights = wg.dtype != BF16
    tf = 256 if cast_weights else 512
    assert not cast_weights or cap == tc, "each bf16 weight block must be written by exactly one grid step"
    w_specs = [
        pl.BlockSpec((1, D_MODEL, tf), lambda e, c, f: (e, 0, f)),
        pl.BlockSpec((1, D_MODEL, tf), lambda e, c, f: (e, 0, f)),
        pl.BlockSpec((1, tf, D_MODEL), lambda e, c, f: (e, f, 0)),
    ]
    acc_shape = jax.ShapeDtypeStruct(acc.shape, F32)
    acc_spec = pl.BlockSpec(memory_space=pl.ANY)
    w_shape = jax.ShapeDtypeStruct(wg.shape, BF16)
    out = pl.pallas_call(
        functools.partial(_experts_kernel, cast_weights=cast_weights),
        out_shape=(acc_shape, w_shape, w_shape, w_shape) if cast_weights else acc_shape,
        grid=(N_EXPERTS, cap // tc, EXPERT_FF // tf),
        in_specs=[
            pl.BlockSpec((cap,), lambda e, c, f: (e,), memory_space=pltpu.SMEM),
            pl.BlockSpec((cap,), lambda e, c, f: (jnp.minimum(e + 1, N_EXPERTS - 1),), memory_space=pltpu.SMEM),
            pl.BlockSpec(memory_space=pl.ANY),
            pl.BlockSpec(memory_space=pl.ANY),
            *w_specs,
        ],
        out_specs=(acc_spec, *w_specs) if cast_weights else acc_spec,
        scratch_shapes=[
            pltpu.VMEM((2, tc * ROW_PITCH, LANES), F32),
            pltpu.VMEM((tc, D_MODEL), BF16),
            pltpu.VMEM((tc, D_MODEL), F32),
            pltpu.VMEM((tc, 1), F32),
            pltpu.SemaphoreType.DMA((2,)),
            pltpu.SemaphoreType.DMA((2,)),
            pltpu.SemaphoreType.DMA((2,)),
        ],
        input_output_aliases={3: 0},
        compiler_params=_params(("arbitrary", "arbitrary", "arbitrary")),
        name="experts",
    )(idx_flat, idx_flat, tok, acc, wg, wu, wd)
    return out if cast_weights else (out, wg, wu, wd)


def _final_kernel(x_ref, g_ref, o_ref):
    tm = o_ref.shape[0]
    x = jnp.concatenate([x_ref[pl.ds(k, tm, stride=ROW_PITCH), :] for k in range(ROW_TILES)], axis=-1)
    o_ref[...] = _rmsnorm(x, g_ref[...])


def _final_norm(rows, g):
    t = rows.shape[0] // ROW_PITCH
    tm = min(512, t)
    return pl.pallas_call(
        _final_kernel,
        out_shape=jax.ShapeDtypeStruct((t, D_MODEL), F32),
        grid=(t // tm,),
        in_specs=[pl.BlockSpec((tm * ROW_PITCH, LANES), lambda i: (i, 0)),
                  pl.BlockSpec((1, D_MODEL), lambda i: (0, 0))],
        out_specs=pl.BlockSpec((tm, D_MODEL), lambda i: (i, 0)),
        compiler_params=_params(("parallel",)),
        name="final_norm",
    )(rows, g)


def _rotary_tables(seq):
    half = HEAD_DIM // 2
    inv = ROPE_BASE ** (-jnp.arange(half, dtype=F32) / half)
    ang = jnp.arange(seq, dtype=F32)[:, None] * inv[None, :]
    cos, sin = jnp.cos(ang), jnp.sin(ang)
    return jnp.concatenate([cos, cos], axis=-1), jnp.concatenate([-sin, sin], axis=-1)


def _trunk(x, w, expert_w):
    batch, seq, _ = x.shape
    t = batch * seq
    cap = EC_CAPACITY_FACTOR * t // N_EXPERTS
    x2d = x.reshape(t, D_MODEL)
    cos_t, sin_t = _rotary_tables(seq)

    proj = _in_proj(x2d, w["norm_mix_g"], w["w_in"], cos_t, sin_t, seq)
    o_f, o_b = _retention(proj, w["decays"], batch, seq)
    acc, tok, aff = _mix_out(o_f, o_b, proj, x2d, w["ret_gn_g"], w["conv_w"], w["w_o"],
                             w["norm_ffn_g"], *w["w_router_split"], seq)
    idx = _select(aff.T.reshape(N_EXPERTS, t // LANES, LANES), cap)
    acc, *expert_w = _experts(idx.reshape(N_EXPERTS * cap), tok, acc, *expert_w, cap)
    return _final_norm(acc, w["final_norm_g"]).reshape(batch, seq, D_MODEL), expert_w


def kernel(x_prompt, x_sample, norm_mix_g, w_in, conv_w, ret_decay_fwd, ret_decay_bwd, ret_gn_g, w_o,
           norm_ffn_g, w_router, w_gate, w_up, w_down, final_norm_g):
    assert w_in.shape[0] == 1, "one encoder layer"
    w = {
        "norm_mix_g": norm_mix_g[0][None, :],
        "w_in": w_in[0].astype(BF16),
        "conv_w": conv_w[0],
        "decays": jnp.stack([ret_decay_fwd[0], ret_decay_bwd[0]]),
        "ret_gn_g": ret_gn_g[0][None, :],
        "w_o": w_o[0].astype(BF16),
        "norm_ffn_g": norm_ffn_g[0][None, :],
        "w_router_split": _split_bf16(jnp.pad(w_router[0], ((0, 0), (0, LANES - N_EXPERTS)))),
        "final_norm_g": final_norm_g[None, :],
    }
    y_prompt, expert_w = _trunk(x_prompt, w, (w_gate[0], w_up[0], w_down[0]))
    y_sample, _ = _trunk(x_sample, w, expert_w)
    return (y_prompt, y_sample)
```
